```python
import math
import jax
import jax.numpy as jnp
from jax import lax
import numpy as np

D_MODEL = 1024
BATCH = 4
SEQ = 8192
DEPTH = 2
DEC_BATCH = 32
DEC_SEQ = 32
PAST_LEN = 4096

CHUNK = 64
BAND_CHUNKS = 8
ATT_WINDOW = BAND_CHUNKS * CHUNK
A_HEADS = 8
A_HEAD_DIM = D_MODEL // 16
A_WIDTH = A_HEADS * A_HEAD_DIM
REL_MAX = 128
N_REL = (CHUNK - 1) + REL_MAX + 1
B_HEADS = 8
B_HEAD_DIM = D_MODEL // 16
B_WIDTH = B_HEADS * B_HEAD_DIM
B_QKV = 3 * B_WIDTH
CONV_W = 4
GDN_CHUNK = 64
SPLIT_SIZES = (A_WIDTH, A_WIDTH, A_WIDTH, A_WIDTH, B_QKV, B_WIDTH, B_HEADS, B_HEADS, D_MODEL, D_MODEL)
IN_DIM = 4 * A_WIDTH + B_QKV + B_WIDTH + 2 * B_HEADS + 2 * D_MODEL
ALPHA = (2 * DEPTH) ** 0.25
BETA = (8 * DEPTH) ** -0.25
LN_EPS = 1e-5
NORM_EPS = 1e-6

kernel_name = 'hybrid_chunkband_gdn_streaming_step'


def layer_norm(x, g, b):
    xf = x.astype(jnp.float32)
    mu = jnp.mean(xf, axis=-1, keepdims=True)
    var = jnp.mean(jnp.square(xf - mu), axis=-1, keepdims=True)
    y = (xf - mu) * lax.rsqrt(var + LN_EPS) * g.astype(jnp.float32) + b.astype(jnp.float32)
    return y.astype(x.dtype)


def split_projection(p):
    cuts = np.cumsum(SPLIT_SIZES)[:-1].tolist()
    return jnp.split(p, cuts, axis=-1)


def l2_normalize(z):
    zf = z.astype(jnp.float32)
    return zf * lax.rsqrt(jnp.sum(zf * zf, axis=-1, keepdims=True) + NORM_EPS)


def band_attention(q, k, v, q_pos, k_pos, k_valid, rel_bias):
    s = jnp.einsum('bqhd,bkhd->bhqk', q, k).astype(jnp.float32) * (A_HEAD_DIM ** -0.5)
    rel = jnp.clip(q_pos[:, None] - k_pos[None, :], -(CHUNK - 1), REL_MAX) + (CHUNK - 1)
    s = s + rel_bias[:, rel].astype(jnp.float32)[None]
    s = jnp.where(k_valid[None, None, None, :], s, -1e30)
    p = jax.nn.softmax(s, axis=-1)
    return jnp.einsum('bhqk,bkhd->bqhd', p.astype(v.dtype), v)


def chunk_band_attention(q, k, v, rel_bias):
    bsz, t, h, d = q.shape
    n_chunks = t // CHUNK
    band = ATT_WINDOW + CHUNK
    pad = ((0, 0), (ATT_WINDOW, 0), (0, 0), (0, 0))
    kp = jnp.pad(k, pad)
    vp = jnp.pad(v, pad)
    q_blocks = jnp.moveaxis(q.reshape(bsz, n_chunks, CHUNK, h, d), 1, 0)
    k_offs = jnp.arange(band, dtype=jnp.int32)
    q_offs = jnp.arange(CHUNK, dtype=jnp.int32)

    def one_chunk(args):
        c, q_blk = args
        start = c * CHUNK
        k_blk = lax.dynamic_slice_in_dim(kp, start, band, axis=1)
        v_blk = lax.dynamic_slice_in_dim(vp, start, band, axis=1)
        k_pos = start - ATT_WINDOW + k_offs
        q_pos = start + q_offs
        return band_attention(q_blk, k_blk, v_blk, q_pos, k_pos, k_pos >= 0, rel_bias)

    o = lax.map(one_chunk, (jnp.arange(n_chunks, dtype=jnp.int32), q_blocks))
    return jnp.moveaxis(o, 0, 1).reshape(bsz, t, h, d)


def causal_conv(x_full, w, t):
    acc = x_full[:, 0:t] * w[0]
    for i in range(1, CONV_W):
        acc = acc + x_full[:, i:i + t] * w[i]
    return acc


def gated_delta_rule(q, k, v, g, beta, s0):
    f32 = jnp.float32
    bsz, t, h, dk = q.shape
    dv = v.shape[-1]
    blk = min(t, GDN_CHUNK)
    nb = t // blk

    def to_blocks(z):
        z = z.astype(f32).reshape((bsz, nb, blk, h) + z.shape[3:])
        return jnp.moveaxis(jnp.swapaxes(z, 2, 3), 1, 0)

    qb = to_blocks(q)
    kb = to_blocks(k)
    vb = to_blocks(v)
    gb = to_blocks(g)
    bb = to_blocks(beta)
    gc = jnp.cumsum(gb, axis=-1)
    causal = jnp.tril(jnp.ones((blk, blk), dtype=bool))
    strict = jnp.tril(jnp.ones((blk, blk), dtype=f32), -1)
    decay = jnp.exp(jnp.where(causal, gc[..., :, None] - gc[..., None, :], -jnp.inf))
    k_beta = kb * bb[..., None]
    v_beta = vb * bb[..., None]
    m = jnp.einsum('nbhid,nbhjd->nbhij', k_beta, kb) * decay * strict
    eye = jnp.eye(blk, dtype=f32)
    t_inv = lax.linalg.triangular_solve(eye + m, jnp.broadcast_to(eye, m.shape),
                                        left_side=True, lower=True, unit_diagonal=True)
    u = jnp.einsum('nbhij,nbhjd->nbhid', t_inv, v_beta)
    w = jnp.einsum('nbhij,nbhjd->nbhid', t_inv, k_beta * jnp.exp(gc)[..., None])
    qk = jnp.einsum('nbhid,nbhjd->nbhij', qb, kb) * decay

    def step(s, blk_in):
        q_c, k_c, u_c, w_c, gc_c, qk_c = blk_in
        v_new = u_c - jnp.einsum('bhld,bhde->bhle', w_c, s)
        o_c = (jnp.einsum('bhld,bhde->bhle', q_c * jnp.exp(gc_c)[..., None], s)
               + jnp.einsum('bhij,bhje->bhie', qk_c, v_new))
        g_last = gc_c[..., -1]
        k_dec = k_c * jnp.exp(g_last[..., None] - gc_c)[..., None]
        s = s * jnp.exp(g_last)[..., None, None] + jnp.einsum('bhld,bhle->bhde', k_dec, v_new)
        return s, o_c

    s_final, o = lax.scan(step, s0.astype(f32), (qb, kb, u, w, gc, qk))
    o = jnp.swapaxes(jnp.moveaxis(o, 0, 1), 2, 3).reshape(bsz, t, h, dv)
    return o, s_final


def trunk_layer(x, kv_past, conv_prev, s_prev, w_in, rel_bias, conv_w, a_log, dt_bias,
                gdn_norm_w, w_branch_a, w_branch_b, w_out, ln_g, ln_b):
    f32 = jnp.float32
    bsz, t, _ = x.shape
    qa, ka, va, za, qkv_b, zb, a_b, b_b, gate_a, gate_b = split_projection(x @ w_in)

    qa = qa.reshape(bsz, t, A_HEADS, A_HEAD_DIM)
    ka = ka.reshape(bsz, t, A_HEADS, A_HEAD_DIM)
    va = va.reshape(bsz, t, A_HEADS, A_HEAD_DIM)
    if kv_past is None:
        oa = chunk_band_attention(qa, ka, va, rel_bias)
        keep = min(ATT_WINDOW, t)
        new_k = ka[:, t - keep:]
        new_v = va[:, t - keep:]
    else:
        k_past, v_past = kv_past
        n_past = k_past.shape[1]
        k_all = jnp.concatenate([k_past.astype(ka.dtype), ka], axis=1)
        v_all = jnp.concatenate([v_past.astype(va.dtype), va], axis=1)
        q_pos = PAST_LEN + jnp.arange(t, dtype=jnp.int32)
        k_pos = PAST_LEN - n_past + jnp.arange(n_past + t, dtype=jnp.int32)
        oa = band_attention(qa, k_all, v_all, q_pos, k_pos, jnp.ones((n_past + t,), dtype=bool), rel_bias)
        new_k = ka
        new_v = va
    ya = (oa.reshape(bsz, t, A_WIDTH) * jax.nn.silu(za)) @ w_branch_a

    xp = jnp.concatenate([conv_prev.astype(qkv_b.dtype), qkv_b], axis=1)
    new_conv = xp[:, -(CONV_W - 1):]
    c = jax.nn.silu(causal_conv(xp, conv_w, t))
    qb, kb, vb = jnp.split(c, 3, axis=-1)
    qb = l2_normalize(qb.reshape(bsz, t, B_HEADS, B_HEAD_DIM)) * (B_HEAD_DIM ** -0.5)
    kb = l2_normalize(kb.reshape(bsz, t, B_HEADS, B_HEAD_DIM))
    vb = vb.reshape(bsz, t, B_HEADS, B_HEAD_DIM)
    g = -jnp.exp(a_log.astype(f32)) * jax.nn.softplus(a_b.astype(f32) + dt_bias.astype(f32))
    beta = jax.nn.sigmoid(b_b.astype(f32))
    ob, s_new = gated_delta_rule(qb, kb, vb, g, beta, s_prev)
    ob = ob * lax.rsqrt(jnp.mean(ob * ob, axis=-1, keepdims=True) + NORM_EPS) * gdn_norm_w.astype(f32)
    ob = ob.astype(x.dtype).reshape(bsz, t, B_WIDTH)
    yb = (ob * jax.nn.silu(zb)) @ w_branch_b

    mix = jax.nn.sigmoid(gate_a) * ya + jax.nn.sigmoid(gate_b) * yb
    y = layer_norm(ALPHA * x + mix @ w_out, ln_g, ln_b)
    return y, (new_k, new_v, new_conv, s_new.astype(x.dtype))


def setup_inputs(seed: int = 0) -> dict:
    key = jax.random.key(seed)
    ks = jax.random.split(key, 24)
    nrm = jax.random.normal
    n_cache = min(ATT_WINDOW, PAST_LEN)
    x_prompt = nrm(ks[0], (BATCH, SEQ, D_MODEL), jnp.float32)
    x_sample = nrm(ks[1], (DEC_BATCH, DEC_SEQ, D_MODEL), jnp.float32)
    cache_attn_k = nrm(ks[2], (DEPTH, DEC_BATCH, n_cache, A_HEADS, A_HEAD_DIM), jnp.float32)
    cache_attn_v = nrm(ks[3], (DEPTH, DEC_BATCH, n_cache, A_HEADS, A_HEAD_DIM), jnp.float32) * BETA
    state_conv = nrm(ks[4], (DEPTH, DEC_BATCH, CONV_W - 1, B_QKV), jnp.float32)
    state_gdn = nrm(ks[5], (DEPTH, DEC_BATCH, B_HEADS, B_HEAD_DIM, B_HEAD_DIM), jnp.float32) * 0.05
    ln0_g = 1.0 + 0.05 * nrm(ks[6], (D_MODEL,), jnp.float32)
    ln0_b = 0.02 * nrm(ks[7], (D_MODEL,), jnp.float32)
    col_scale = np.ones((IN_DIM,), np.float32)
    col_scale[2 * A_WIDTH:3 * A_WIDTH] = BETA
    v_off = 4 * A_WIDTH + 2 * B_WIDTH
    col_scale[v_off:v_off + B_WIDTH] = BETA
    w_in = nrm(ks[8], (DEPTH, D_MODEL, IN_DIM), jnp.float32) * (D_MODEL ** -0.5) * jnp.asarray(col_scale)
    rel_bias = 0.5 * nrm(ks[9], (DEPTH, A_HEADS, N_REL), jnp.float32)
    conv_w = nrm(ks[10], (DEPTH, CONV_W, B_QKV), jnp.float32) * (CONV_W ** -0.5)
    a_log = jnp.log(jax.random.uniform(ks[11], (DEPTH, B_HEADS), jnp.float32, 1.0, 16.0))
    dt = jnp.exp(jax.random.uniform(ks[12], (DEPTH, B_HEADS), jnp.float32, math.log(1e-3), math.log(1e-1)))
    dt_bias = dt + jnp.log(-jnp.expm1(-dt))
    gdn_norm_w = 1.0 + 0.05 * nrm(ks[13], (DEPTH, B_HEAD_DIM), jnp.float32)
    w_branch_a = nrm(ks[14], (DEPTH, A_WIDTH, D_MODEL), jnp.float32) * (A_WIDTH ** -0.5) * BETA
    w_branch_b = nrm(ks[15], (DEPTH, B_WIDTH, D_MODEL), jnp.float32) * (B_WIDTH ** -0.5) * BETA
    w_out = nrm(ks[16], (DEPTH, D_MODEL, D_MODEL), jnp.float32) * (D_MODEL ** -0.5) * BETA
    ln_g = 1.0 + 0.05 * nrm(ks[17], (DEPTH, D_MODEL), jnp.float32)
    ln_b = 0.02 * nrm(ks[18], (DEPTH, D_MODEL), jnp.float32)
    return {'x_prompt': x_prompt, 'x_sample': x_sample,
            'cache_attn_k': cache_attn_k, 'cache_attn_v': cache_attn_v,
            'state_conv': state_conv, 'state_gdn': state_gdn,
            'ln0_g': ln0_g, 'ln0_b': ln0_b, 'w_in': w_in, 'rel_bias': rel_bias,
            'conv_w': conv_w, 'a_log': a_log, 'dt_bias': dt_bias, 'gdn_norm_w': gdn_norm_w,
            'w_branch_a': w_branch_a, 'w_branch_b': w_branch_b, 'w_out': w_out,
            'ln_g': ln_g, 'ln_b': ln_b}


def reference(x_prompt, x_sample, cache_attn_k, cache_attn_v, state_conv, state_gdn,
              ln0_g, ln0_b, w_in, rel_bias, conv_w, a_log, dt_bias, gdn_norm_w,
              w_branch_a, w_branch_b, w_out, ln_g, ln_b):
    hp = layer_norm(x_prompt, ln0_g, ln0_b)
    hs = layer_norm(x_sample, ln0_g, ln0_b)
    n_prompt = x_prompt.shape[0]
    conv_zero = jnp.zeros((n_prompt, CONV_W - 1, B_QKV), x_prompt.dtype)
    s_zero = jnp.zeros((n_prompt, B_HEADS, B_HEAD_DIM, B_HEAD_DIM), jnp.float32)
    kp_list = []
    vp_list = []
    cp_list = []
    sp_list = []
    ks_list = []
    vs_list = []
    cs_list = []
    ss_list = []
    for l in range(DEPTH):
        params = (w_in[l], rel_bias[l], conv_w[l], a_log[l], dt_bias[l], gdn_norm_w[l],
                  w_branch_a[l], w_branch_b[l], w_out[l], ln_g[l], ln_b[l])
        hp, (k_p, v_p, c_p, s_p) = trunk_layer(hp, None, conv_zero, s_zero, *params)
        hs, (k_s, v_s, c_s, s_s) = trunk_layer(hs, (cache_attn_k[l], cache_attn_v[l]),
                                               state_conv[l], state_gdn[l], *params)
        kp_list.append(k_p)
        vp_list.append(v_p)
        cp_list.append(c_p)
        sp_list.append(s_p)
        ks_list.append(k_s)
        vs_list.append(v_s)
        cs_list.append(c_s)
        ss_list.append(s_s)
    return (hp, hs,
            jnp.stack(kp_list), jnp.stack(vp_list), jnp.stack(cp_list), jnp.stack(sp_list),
            jnp.stack(ks_list), jnp.stack(vs_list), jnp.stack(cs_list), jnp.stack(ss_list))
```

```python
import functools
import math

import numpy as np
import jax
import jax.numpy as jnp
from jax import lax
from jax.experimental import pallas as pl
from jax.experimental.pallas import tpu as pltpu

D_MODEL = 1024
DEPTH = 2
PAST_LEN = 4096
CHUNK = 64
BAND_CHUNKS = 8
ATT_WINDOW = BAND_CHUNKS * CHUNK
A_HEADS = 8
A_HEAD_DIM = 64
A_WIDTH = A_HEADS * A_HEAD_DIM
REL_MAX = 128
B_HEADS = 8
B_HEAD_DIM = 64
B_WIDTH = B_HEADS * B_HEAD_DIM
B_QKV = 3 * B_WIDTH
CONV_W = 4
GDN_CHUNK = 64
ALPHA = (2 * DEPTH) ** 0.25
LN_EPS = 1e-5
NORM_EPS = 1e-6

LANES = 128
HEAD_PAIRS = A_HEADS // 2
NEG = -1e30
V7X_VMEM_LIMIT = 56 * 1024 * 1024

BF16 = jnp.bfloat16
F32 = jnp.float32


def _mm(a, b):
    return jnp.dot(a.astype(BF16), b.astype(BF16), preferred_element_type=F32)


def _mm_nt(a, b):
    return lax.dot_general(a.astype(BF16), b.astype(BF16), (((1,), (1,)), ((), ())),
                           preferred_element_type=F32)


def _mm_tn(a, b):
    return lax.dot_general(a.astype(BF16), b.astype(BF16), (((0,), (0,)), ((), ())),
                           preferred_element_type=F32)


def _mm_exact01(w01, x):
    x1 = x.astype(BF16)
    r1 = x - x1.astype(F32)
    x2 = r1.astype(BF16)
    x3 = (r1 - x2.astype(F32)).astype(BF16)
    w = w01.astype(BF16)
    dot = functools.partial(jnp.dot, preferred_element_type=F32)
    return dot(w, x1) + dot(w, x2) + dot(w, x3)


def _silu(x):
    return x * jax.nn.sigmoid(x)


def _params(sem, est_bytes):
    limit = int(min(V7X_VMEM_LIMIT, max(32 * 1024 * 1024, est_bytes)))
    return pltpu.CompilerParams(dimension_semantics=sem, vmem_limit_bytes=limit)


def _const_spec(shape):
    nd = len(shape)
    return pl.BlockSpec(shape, lambda *_: (0,) * nd, pipeline_mode=pl.Buffered(1))


def _ln_kernel(x_ref, g_ref, b_ref, o_ref):
    x = x_ref[...]
    mu = jnp.mean(x, axis=-1, keepdims=True)
    xc = x - mu
    var = jnp.mean(xc * xc, axis=-1, keepdims=True)
    o_ref[...] = xc * lax.rsqrt(var + LN_EPS) * g_ref[...] + b_ref[...]


def _layer_norm(x2d, g, b, tm=512):
    n, d = x2d.shape
    return pl.pallas_call(
        _ln_kernel,
        grid=(n // tm,),
        in_specs=[pl.BlockSpec((tm, d), lambda i: (i, 0)),
                  _const_spec((1, d)), _const_spec((1, d))],
        out_specs=pl.BlockSpec((tm, d), lambda i: (i, 0)),
        out_shape=jax.ShapeDtypeStruct((n, d), F32),
        compiler_params=_params(("parallel",), 4 * tm * d * 4 * 2),
        name="layer_norm",
    )(x2d, g.reshape(1, d), b.reshape(1, d))


def _inproj_kernel(x_ref, wa_ref, wb_ref, wz_ref, wab_ref, wg_ref,
                   q_ref, k_ref, v_ref, za_ref, xb_ref, zb_ref, ab_ref, ga_ref, gb_ref,
                   kt_ref, vt_ref):
    x = x_ref[0].astype(BF16)
    dot = functools.partial(jnp.dot, preferred_element_type=F32)
    ra = dot(x, wa_ref[...])
    q_ref[0] = ra[:, 0:A_WIDTH].astype(BF16)
    kf = ra[:, A_WIDTH:2 * A_WIDTH]
    vf = ra[:, 2 * A_WIDTH:3 * A_WIDTH]
    k_ref[0] = kf.astype(BF16)
    v_ref[0] = vf.astype(BF16)
    kt_ref[0] = kf
    vt_ref[0] = vf
    za_ref[0] = ra[:, 3 * A_WIDTH:4 * A_WIDTH]
    xb_ref[0] = dot(x, wb_ref[...])
    zb_ref[0] = dot(x, wz_ref[...])
    ab_ref[0] = dot(x, wab_ref[...])
    rg = dot(x, wg_ref[...])
    ga_ref[0] = rg[:, 0:D_MODEL]
    gb_ref[0] = rg[:, D_MODEL:2 * D_MODEL]


def _in_projection(h, w, *, tm, kv_pad, tail):
    bsz, t, d = h.shape
    nt = t // tm
    pad_blocks = kv_pad // tm
    tail_blocks = tail // tm

    def row(width):
        return pl.BlockSpec((1, tm, width), lambda b, i: (b, i, 0))

    kv_spec = pl.BlockSpec((1, tm, A_WIDTH), lambda b, i: (b, i + pad_blocks, 0))
    tail_spec = pl.BlockSpec((1, tm, A_WIDTH),
                             lambda b, i: (b, jnp.maximum(i - (nt - tail_blocks), 0), 0))
    out_shapes = [
        jax.ShapeDtypeStruct((bsz, t, A_WIDTH), BF16),
        jax.ShapeDtypeStruct((bsz, t + kv_pad, A_WIDTH), BF16),
        jax.ShapeDtypeStruct((bsz, t + kv_pad, A_WIDTH), BF16),
        jax.ShapeDtypeStruct((bsz, t, A_WIDTH), F32),
        jax.ShapeDtypeStruct((bsz, t, B_QKV), F32),
        jax.ShapeDtypeStruct((bsz, t, B_WIDTH), F32),
        jax.ShapeDtypeStruct((bsz, t, LANES), F32),
        jax.ShapeDtypeStruct((bsz, t, D_MODEL), F32),
        jax.ShapeDtypeStruct((bsz, t, D_MODEL), F32),
        jax.ShapeDtypeStruct((bsz, tail, A_WIDTH), F32),
        jax.ShapeDtypeStruct((bsz, tail, A_WIDTH), F32),
    ]
    out_specs = [row(A_WIDTH), kv_spec, kv_spec, row(A_WIDTH), row(B_QKV), row(B_WIDTH),
                 row(LANES), row(D_MODEL), row(D_MODEL), tail_spec, tail_spec]
    weights = (w["wa"], w["wb"], w["wz"], w["wab"], w["wg"])
    in_specs = [row(d)] + [_const_spec(x.shape) for x in weights]
    kz = jnp.zeros((bsz, t + kv_pad, A_WIDTH), BF16)
    vz = jnp.zeros((bsz, t + kv_pad, A_WIDTH), BF16)
    n_w = sum(int(np.prod(x.shape)) for x in weights)
    est = 2 * n_w + 2 * tm * (d * 4 + 3 * A_WIDTH * 2 + (2 * A_WIDTH + B_QKV + B_WIDTH + LANES
                                                        + 2 * D_MODEL + 2 * A_WIDTH) * 4)
    est += tm * (4 * A_WIDTH + 2 * D_MODEL) * 4 * 2
    return pl.pallas_call(
        _inproj_kernel_aliased,
        grid=(bsz, nt),
        in_specs=in_specs + [pl.BlockSpec(memory_space=pl.ANY), pl.BlockSpec(memory_space=pl.ANY)],
        out_specs=out_specs,
        out_shape=out_shapes,
        input_output_aliases={6: 1, 7: 2},
        compiler_params=_params(("parallel", "arbitrary"), est + (8 << 20)),
        name="in_projection",
    )(h, *weights, kz, vz)


def _inproj_kernel_aliased(x_ref, wa_ref, wb_ref, wz_ref, wab_ref, wg_ref, kz_ref, vz_ref, *outs):
    del kz_ref, vz_ref
    _inproj_kernel(x_ref, wa_ref, wb_ref, wz_ref, wab_ref, wg_ref, *outs)


def _attention_heads(q, k, v, za, bias_ref, kmask, o_ref):
    lane = lax.broadcasted_iota(jnp.int32, (1, LANES), 1)
    first = lane < A_HEAD_DIM
    scale = A_HEAD_DIM ** -0.5
    for p in range(HEAD_PAIRS):
        sl = slice(p * LANES, (p + 1) * LANES)
        qp, kp, vp = q[:, sl], k[:, sl], v[:, sl]
        outs = []
        for hh in range(2):
            keep = first if hh == 0 else jnp.logical_not(first)
            qm = jnp.where(keep, qp, jnp.zeros_like(qp))
            s = _mm_nt(qm, kp) * scale + bias_ref[2 * p + hh]
            if kmask is not None:
                s = s + kmask
            mx = jnp.max(s, axis=-1, keepdims=True)
            e = jnp.exp(s - mx)
            den = jnp.sum(e, axis=-1, keepdims=True)
            outs.append(_mm(e, vp) / den)
        o = jnp.where(first, outs[0], outs[1])
        o_ref[:, sl] = (o * _silu(za[:, sl])).astype(o_ref.dtype)


def _attn_prompt_kernel(q_ref, k_ref, v_ref, za_ref, bias_ref, o_ref, *, qblk, band):
    start = pl.multiple_of(pl.program_id(1) * qblk, qblk)
    kb = k_ref[0, pl.ds(start, band), :]
    vb = v_ref[0, pl.ds(start, band), :]
    col = lax.broadcasted_iota(jnp.int32, (1, band), 1) + start
    kmask = jnp.where(col >= ATT_WINDOW, 0.0, NEG).astype(F32)
    _attention_heads(q_ref[0], kb, vb, za_ref[0], bias_ref, kmask, o_ref.at[0])


def _prompt_bias_table(rel_bias, qblk, band):
    i = np.arange(qblk)[:, None]
    j = np.arange(band)[None, :]
    rel = np.clip(i - j + ATT_WINDOW, -(CHUNK - 1), REL_MAX) + (CHUNK - 1)
    qc, kc = i // CHUNK, j // CHUNK
    visible = (kc >= qc) & (kc <= qc + BAND_CHUNKS)
    tab = rel_bias[:, rel]
    return jnp.where(jnp.asarray(visible)[None], tab, NEG).astype(F32)


def _attention_prompt(q, kpad, vpad, za, rel_bias, qblk=2 * CHUNK):
    bsz, t, _ = q.shape
    band = ATT_WINDOW + qblk
    tab = _prompt_bias_table(rel_bias, qblk, band)
    tpad = kpad.shape[1]
    row = pl.BlockSpec((1, qblk, A_WIDTH), lambda b, i: (b, i, 0))
    full = pl.BlockSpec((1, tpad, A_WIDTH), lambda b, i: (b, 0, 0), pipeline_mode=pl.Buffered(1))
    est = 2 * tpad * A_WIDTH * 2 + tab.size * 4 + 8 * qblk * A_WIDTH * 4 + 16 * qblk * band * 4
    return pl.pallas_call(
        functools.partial(_attn_prompt_kernel, qblk=qblk, band=band),
        grid=(bsz, t // qblk),
        in_specs=[row, full, full, row, _const_spec(tab.shape)],
        out_specs=row,
        out_shape=jax.ShapeDtypeStruct((bsz, t, A_WIDTH), BF16),
        compiler_params=_params(("parallel", "arbitrary"), est + (8 << 20)),
        name="attention_prompt",
    )(q, kpad, vpad, za, tab)


def _attn_sample_kernel(q_ref, kc_ref, vc_ref, kn_ref, vn_ref, za_ref, bias_ref, o_ref, *, nk):
    n_past, t = kc_ref.shape[1], kn_ref.shape[1]
    zeros = jnp.zeros((nk - n_past - t, A_WIDTH), BF16)
    k = jnp.concatenate([kc_ref[0].astype(BF16), kn_ref[0].astype(BF16), zeros], axis=0)
    v = jnp.concatenate([vc_ref[0].astype(BF16), vn_ref[0].astype(BF16), zeros], axis=0)
    _attention_heads(q_ref[0], k, v, za_ref[0], bias_ref, None, o_ref.at[0])


def _attention_sample(q, k_cache, v_cache, k_new, v_new, za, rel_bias):
    bsz, t, _ = q.shape
    n_past = k_cache.shape[1]
    nk = -(-(n_past + t) // LANES) * LANES
    i = np.arange(t)[:, None]
    j = np.arange(nk)[None, :]
    rel = np.clip(i - j + n_past, -(CHUNK - 1), REL_MAX) + (CHUNK - 1)
    tab = jnp.where(jnp.asarray(j < n_past + t)[None], rel_bias[:, rel], NEG).astype(F32)

    def blk(rows):
        return pl.BlockSpec((1, rows, A_WIDTH), lambda b: (b, 0, 0))

    return pl.pallas_call(
        functools.partial(_attn_sample_kernel, nk=nk),
        grid=(bsz,),
        in_specs=[blk(t), blk(n_past), blk(n_past), blk(t), blk(t), blk(t), _const_spec(tab.shape)],
        out_specs=blk(t),
        out_shape=jax.ShapeDtypeStruct((bsz, t, A_WIDTH), BF16),
        compiler_params=_params(("parallel",), 32 << 20),
        name="attention_sample",
    )(q, k_cache, v_cache, k_new, v_new, za, tab)


def _gdn_kernel(xb_ref, ab_ref, zb_ref, conv0_ref, s0_ref, convw_ref, arow_ref, dtrow_ref, normw_ref,
                o_ref, s_out_ref, xbuf, cbuf, s_scr, *, tb, blk):
    t_idx = pl.program_id(1)
    hist = 8
    two = 2 * blk

    @pl.when(t_idx == 0)
    def _():
        xbuf[0:hist, :] = conv0_ref[0]
        s_scr[...] = s0_ref[0]

    xbuf[hist:hist + tb, :] = xb_ref[0]
    acc = xbuf[hist - 3:hist - 3 + tb, :] * convw_ref[0:1, :]
    for i in range(1, CONV_W):
        acc = acc + xbuf[hist - 3 + i:hist - 3 + i + tb, :] * convw_ref[i:i + 1, :]
    cbuf[...] = _silu(acc)
    tail = xbuf[tb:tb + hist, :]
    xbuf[0:hist, :] = tail

    lane = lax.broadcasted_iota(jnp.int32, (1, LANES), 1)
    first = lane < B_HEAD_DIM
    r = lax.broadcasted_iota(jnp.int32, (two, two), 0)
    c = lax.broadcasted_iota(jnp.int32, (two, two), 1)
    same_head = (r // blk) == (c // blk)
    causal = same_head & (r >= c)
    eye = (r == c).astype(F32)

    def level_mask(bs):
        return ((r // bs) == (c // bs)) & ((r % bs) >= bs // 2) & ((c % bs) < bs // 2)

    ri = lax.broadcasted_iota(jnp.int32, (blk, blk), 0)
    ci = lax.broadcasted_iota(jnp.int32, (blk, blk), 1)
    lcum = (ri >= ci).astype(F32)

    def stack(x):
        zero = jnp.zeros_like(x)
        return jnp.concatenate([jnp.where(first, x, zero), jnp.where(first, zero, x)], axis=0)

    def pair_col(x, h0):
        return jnp.concatenate([x[:, h0:h0 + 1], x[:, h0 + 1:h0 + 2]], axis=0)

    def chunk_body(ci_, carry):
        r0 = pl.multiple_of(ci_ * blk, blk)
        ab = ab_ref[0, pl.ds(r0, blk), :]
        g = arow_ref[...] * jax.nn.softplus(ab + dtrow_ref[...])
        beta = jax.nn.sigmoid(ab)
        gc = _mm_exact01(lcum, g)
        for p in range(HEAD_PAIRS):
            sl = slice(p * LANES, (p + 1) * LANES)
            h0 = 2 * p
            gcol = pair_col(gc, h0)
            bcol = pair_col(beta, B_HEADS + h0)
            glast = jnp.concatenate([jnp.broadcast_to(gc[blk - 1:blk, h0:h0 + 1], (blk, 1)),
                                     jnp.broadcast_to(gc[blk - 1:blk, h0 + 1:h0 + 2], (blk, 1))], axis=0)
            sdec = jnp.concatenate(
                [jnp.broadcast_to(gc[blk - 1:blk, h0:h0 + 1], (B_HEAD_DIM, 1)),
                 jnp.broadcast_to(gc[blk - 1:blk, h0 + 1:h0 + 2], (B_HEAD_DIM, 1))], axis=0)
            qraw = stack(cbuf[pl.ds(r0, blk), p * LANES:(p + 1) * LANES])
            kraw = stack(cbuf[pl.ds(r0, blk), B_WIDTH + p * LANES:B_WIDTH + (p + 1) * LANES])
            vst = stack(cbuf[pl.ds(r0, blk), 2 * B_WIDTH + p * LANES:2 * B_WIDTH + (p + 1) * LANES])
            qst = qraw * (lax.rsqrt(jnp.sum(qraw * qraw, axis=-1, keepdims=True) + NORM_EPS)
                          * (B_HEAD_DIM ** -0.5))
            kst = kraw * lax.rsqrt(jnp.sum(kraw * kraw, axis=-1, keepdims=True) + NORM_EPS)
            kbeta = kst * bcol
            vbeta = vst * bcol
            gb = jnp.broadcast_to(gcol, (two, two))
            decay = jnp.exp(jnp.where(causal, gb - gb.T, NEG))
            gram = _mm_nt(jnp.concatenate([kbeta, qst], axis=0), kst)
            m = gram[0:two] * decay * (1.0 - eye)
            qk = gram[two:2 * two] * decay
            x = eye - jnp.where(level_mask(2), m, 0.0)
            bs = 4
            while bs <= blk:
                cl = jnp.where(level_mask(bs), m, 0.0)
                x = x - _mm(x, _mm(cl, x))
                bs *= 2
            uw = _mm(x, jnp.concatenate([vbeta, kbeta * jnp.exp(gcol)], axis=1))
            u, w = uw[:, 0:LANES], uw[:, LANES:2 * LANES]
            s = s_scr[p]
            ws_qs = _mm(jnp.concatenate([w, qst * jnp.exp(gcol)], axis=0), s)
            v_new = u - ws_qs[0:two]
            o = ws_qs[two:2 * two] + _mm(qk, v_new)
            k_dec = kst * jnp.exp(glast - gcol)
            s_scr[p] = s * jnp.exp(sdec) + _mm_tn(k_dec, v_new)
            ms = jnp.sum(o * o, axis=-1, keepdims=True) * (1.0 / B_HEAD_DIM)
            on = o * lax.rsqrt(ms + NORM_EPS) * normw_ref[...]
            o_pair = on[0:blk] + on[blk:two]
            zb = zb_ref[0, pl.ds(r0, blk), p * LANES:(p + 1) * LANES]
            o_ref[0, pl.ds(r0, blk), p * LANES:(p + 1) * LANES] = (o_pair * _silu(zb)).astype(o_ref.dtype)
        return carry

    lax.fori_loop(0, tb // blk, chunk_body, 0)

    @pl.when(t_idx == pl.num_programs(1) - 1)
    def _():
        s_out_ref[0] = s_scr[...]


def _pair_state(s):
    bsz = s.shape[0]
    s = s.reshape(bsz, HEAD_PAIRS, 2, B_HEAD_DIM, B_HEAD_DIM)
    z = jnp.zeros_like(s[:, :, 0])
    top = jnp.concatenate([s[:, :, 0], z], axis=-1)
    bot = jnp.concatenate([z, s[:, :, 1]], axis=-1)
    return jnp.concatenate([top, bot], axis=-2)


def _unpair_state(sp):
    d = B_HEAD_DIM
    s = jnp.stack([sp[:, :, 0:d, 0:d], sp[:, :, d:2 * d, d:2 * d]], axis=2)
    return s.reshape(sp.shape[0], B_HEADS, d, d)


def _gated_delta(xb, ab, zb, conv_prev, s_prev, lw, *, tb, blk):
    bsz, t, _ = xb.shape
    hist = 8
    conv0 = jnp.concatenate([jnp.zeros((bsz, hist - (CONV_W - 1), B_QKV), F32), conv_prev.astype(F32)], axis=1)
    s0 = _pair_state(s_prev.astype(F32))

    def row(width):
        return pl.BlockSpec((1, tb, width), lambda b, i: (b, i, 0))

    def per_batch(shape):
        nd = len(shape)
        return pl.BlockSpec((1,) + shape, lambda b, i: (b,) + (0,) * nd)

    est = 2 * tb * (B_QKV + LANES + B_WIDTH) * 4 + (2 * tb + hist) * B_QKV * 4 + 6 * HEAD_PAIRS * LANES * LANES * 4
    o, s_out = pl.pallas_call(
        functools.partial(_gdn_kernel, tb=tb, blk=blk),
        grid=(bsz, t // tb),
        in_specs=[row(B_QKV), row(LANES), row(B_WIDTH), per_batch((hist, B_QKV)),
                  per_batch((HEAD_PAIRS, LANES, LANES)),
                  _const_spec((CONV_W, B_QKV)), _const_spec((1, LANES)), _const_spec((1, LANES)),
                  _const_spec((1, LANES))],
        out_specs=[row(B_WIDTH), per_batch((HEAD_PAIRS, LANES, LANES))],
        out_shape=[jax.ShapeDtypeStruct((bsz, t, B_WIDTH), BF16),
                   jax.ShapeDtypeStruct((bsz, HEAD_PAIRS, LANES, LANES), F32)],
        scratch_shapes=[pltpu.VMEM((hist + tb, B_QKV), F32), pltpu.VMEM((tb, B_QKV), F32),
                        pltpu.VMEM((HEAD_PAIRS, LANES, LANES), F32)],
        compiler_params=_params(("parallel", "arbitrary"), est + (16 << 20)),
        name="gated_delta",
    )(xb, ab, zb, conv0, s0, lw["conv_w"], lw["a_row"], lw["dt_row"], lw["normw_row"])
    return o, _unpair_state(s_out)


def _out_kernel(x_ref, a_ref, b_ref, ga_ref, gb_ref, wa_ref, wb_ref, wo_ref, g_ref, beta_ref, y_ref):
    dot = functools.partial(jnp.dot, preferred_element_type=F32)
    ya = dot(a_ref[...], wa_ref[...])
    yb = dot(b_ref[...], wb_ref[...])
    mix = jax.nn.sigmoid(ga_ref[...]) * ya + jax.nn.sigmoid(gb_ref[...]) * yb
    out = dot(mix.astype(BF16), wo_ref[...])
    z = ALPHA * x_ref[...] + out
    mu = jnp.mean(z, axis=-1, keepdims=True)
    zc = z - mu
    var = jnp.mean(zc * zc, axis=-1, keepdims=True)
    y_ref[...] = zc * lax.rsqrt(var + LN_EPS) * g_ref[...] + beta_ref[...]


def _out_projection(x, a, b, ga, gb, w, tm):
    n, d = x.shape

    def row(width):
        return pl.BlockSpec((tm, width), lambda i: (i, 0))

    weights = (w["w_branch_a"], w["w_branch_b"], w["w_out"], w["ln_g"], w["ln_b"])
    est = 2 * tm * (4 * d * 4 + 2 * A_WIDTH * 2) + 2 * (2 * A_WIDTH * d + d * d) + 6 * tm * d * 4
    return pl.pallas_call(
        _out_kernel,
        grid=(n // tm,),
        in_specs=[row(d), row(A_WIDTH), row(B_WIDTH), row(d), row(d)] + [_const_spec(x.shape) for x in weights],
        out_specs=row(d),
        out_shape=jax.ShapeDtypeStruct((n, d), F32),
        compiler_params=_params(("parallel",), est + (8 << 20)),
        name="out_projection",
    )(x, a, b, ga, gb, *weights)


def _layer_weights(l, w_in, conv_w, a_log, dt_bias, gdn_norm_w, w_branch_a, w_branch_b, w_out, ln_g, ln_b):
    wi = w_in[l]
    c0 = 4 * A_WIDTH
    c1 = c0 + B_QKV
    c2 = c1 + B_WIDTH
    c3 = c2 + 2 * B_HEADS
    wab = jnp.pad(wi[:, c2:c3], ((0, 0), (0, LANES - 2 * B_HEADS)))
    pad_h = (0, LANES - B_HEADS)
    return {
        "wa": wi[:, 0:c0].astype(BF16), "wb": wi[:, c0:c1].astype(BF16), "wz": wi[:, c1:c2].astype(BF16),
        "wab": wab.astype(BF16), "wg": wi[:, c3:].astype(BF16),
        "conv_w": conv_w[l].astype(F32),
        "a_row": jnp.pad(-jnp.exp(a_log[l].astype(F32)), pad_h).reshape(1, LANES),
        "dt_row": jnp.pad(dt_bias[l].astype(F32), pad_h).reshape(1, LANES),
        "normw_row": jnp.tile(gdn_norm_w[l].astype(F32), 2).reshape(1, LANES),
        "w_branch_a": w_branch_a[l].astype(BF16), "w_branch_b": w_branch_b[l].astype(BF16),
        "w_out": w_out[l].astype(BF16),
        "ln_g": ln_g[l].astype(F32).reshape(1, D_MODEL), "ln_b": ln_b[l].astype(F32).reshape(1, D_MODEL),
    }


def _prompt_layer(h, lw, rel_bias, *, tm, tb):
    bsz, t, d = h.shape
    keep = min(ATT_WINDOW, t)
    q, kpad, vpad, za, xb, zb, ab, ga, gb, k_tail, v_tail = _in_projection(
        h, lw, tm=tm, kv_pad=ATT_WINDOW, tail=keep)
    oa = _attention_prompt(q, kpad, vpad, za, rel_bias)
    conv_zero = jnp.zeros((bsz, CONV_W - 1, B_QKV), F32)
    s_zero = jnp.zeros((bsz, B_HEADS, B_HEAD_DIM, B_HEAD_DIM), F32)
    ob, s_new = _gated_delta(xb, ab, zb, conv_zero, s_zero, lw, tb=tb, blk=min(t, GDN_CHUNK))
    y = _out_projection(h.reshape(bsz * t, d), oa.reshape(bsz * t, A_WIDTH), ob.reshape(bsz * t, B_WIDTH),
                        ga.reshape(bsz * t, d), gb.reshape(bsz * t, d), lw, tm)
    new_k = k_tail.reshape(bsz, keep, A_HEADS, A_HEAD_DIM)
    new_v = v_tail.reshape(bsz, keep, A_HEADS, A_HEAD_DIM)
    return y.reshape(bsz, t, d), (new_k, new_v, xb[:, t - (CONV_W - 1):], s_new)


def _sample_layer(h, k_cache, v_cache, conv_prev, s_prev, lw, rel_bias, *, tm):
    bsz, t, d = h.shape
    n = bsz * t
    n_past = k_cache.shape[1]
    q, _, _, za, xb, zb, ab, ga, gb, k_new, v_new = _in_projection(
        h.reshape(1, n, d), lw, tm=tm, kv_pad=0, tail=n)

    def per_seq(x):
        return x.reshape(bsz, t, x.shape[-1])

    oa = _attention_sample(per_seq(q), k_cache.reshape(bsz, n_past, A_WIDTH), v_cache.reshape(bsz, n_past, A_WIDTH),
                           per_seq(k_new), per_seq(v_new), per_seq(za), rel_bias)
    xb_s = per_seq(xb)
    xp_tail = jnp.concatenate([conv_prev.astype(F32), xb_s], axis=1)[:, -(CONV_W - 1):]
    ob, s_new = _gated_delta(xb_s, per_seq(ab), per_seq(zb), conv_prev, s_prev, lw, tb=t, blk=min(t, GDN_CHUNK))
    y = _out_projection(h.reshape(n, d), oa.reshape(n, A_WIDTH), ob.reshape(n, B_WIDTH),
                        ga.reshape(n, d), gb.reshape(n, d), lw, tm)
    new_k = per_seq(k_new).reshape(bsz, t, A_HEADS, A_HEAD_DIM)
    new_v = per_seq(v_new).reshape(bsz, t, A_HEADS, A_HEAD_DIM)
    return y.reshape(bsz, t, d), (new_k, new_v, xp_tail, s_new)


def kernel(x_prompt, x_sample, cache_attn_k, cache_attn_v, state_conv, state_gdn, ln0_g, ln0_b, w_in, rel_bias,
           conv_w, a_log, dt_bias, gdn_norm_w, w_branch_a, w_branch_b, w_out, ln_g, ln_b):
    bp, tp, d = x_prompt.shape
    bs, ts, _ = x_sample.shape
    hp = _layer_norm(x_prompt.reshape(bp * tp, d), ln0_g, ln0_b).reshape(bp, tp, d)
    hs = _layer_norm(x_sample.reshape(bs * ts, d), ln0_g, ln0_b).reshape(bs, ts, d)
    outs_p, outs_s = [], []
    for l in range(DEPTH):
        lw = _layer_weights(l, w_in, conv_w, a_log, dt_bias, gdn_norm_w, w_branch_a, w_branch_b, w_out, ln_g, ln_b)
        hp, st_p = _prompt_layer(hp, lw, rel_bias[l].astype(F32), tm=256, tb=256)
        hs, st_s = _sample_layer(hs, cache_attn_k[l], cache_attn_v[l], state_conv[l], state_gdn[l], lw,
                                 rel_bias[l].astype(F32), tm=256)
        outs_p.append(st_p)
        outs_s.append(st_s)

    def stacked(outs, i):
        return jnp.stack([o[i] for o in outs])

    return (hp, hs,
            stacked(outs_p, 0), stacked(outs_p, 1), stacked(outs_p, 2), stacked(outs_p, 3),
            stacked(outs_s, 0), stacked(outs_s, 1), stacked(outs_s, 2), stacked(outs_s, 3))
```

```python
import functools
import math

import numpy as np
import jax
import jax.numpy as jnp
from jax import lax
from jax.experimental import pallas as pl
from jax.experimental.pallas import tpu as pltpu

D_MODEL = 1024
DEPTH = 2
PAST_LEN = 4096
CHUNK = 64
BAND_CHUNKS = 8
ATT_WINDOW = BAND_CHUNKS * CHUNK
A_HEADS = 8
A_HEAD_DIM = 64
A_WIDTH = A_HEADS * A_HEAD_DIM
REL_MAX = 128
B_HEADS = 8
B_HEAD_DIM = 64
B_WIDTH = B_HEADS * B_HEAD_DIM
B_QKV = 3 * B_WIDTH
CONV_W = 4
GDN_CHUNK = 64
ALPHA = (2 * DEPTH) ** 0.25
LN_EPS = 1e-5
NORM_EPS = 1e-6

LANES = 128
HEAD_PAIRS = A_HEADS // 2
NEG = -1e30
V7X_VMEM_LIMIT = 56 * 1024 * 1024

BF16 = jnp.bfloat16
F32 = jnp.float32


def _mm(a, b):
    return jnp.dot(a.astype(BF16), b.astype(BF16), preferred_element_type=F32)


def _mm_nt(a, b):
    return lax.dot_general(a.astype(BF16), b.astype(BF16), (((1,), (1,)), ((), ())),
                           preferred_element_type=F32)


def _mm_tn(a, b):
    return lax.dot_general(a.astype(BF16), b.astype(BF16), (((0,), (0,)), ((), ())),
                           preferred_element_type=F32)


def _mm_exact01(w01, x):
    x1 = x.astype(BF16)
    r1 = x - x1.astype(F32)
    x2 = r1.astype(BF16)
    x3 = (r1 - x2.astype(F32)).astype(BF16)
    w = w01.astype(BF16)
    dot = functools.partial(jnp.dot, preferred_element_type=F32)
    return dot(w, x1) + dot(w, x2) + dot(w, x3)


def _silu(x):
    return x * jax.nn.sigmoid(x)


def _params(sem, est_bytes):
    limit = int(min(V7X_VMEM_LIMIT, max(32 * 1024 * 1024, est_bytes)))
    return pltpu.CompilerParams(dimension_semantics=sem, vmem_limit_bytes=limit)


def _const_spec(shape):
    nd = len(shape)
    return pl.BlockSpec(shape, lambda *_: (0,) * nd, pipeline_mode=pl.Buffered(1))


def _ln_kernel(x_ref, g_ref, b_ref, o_ref):
    x = x_ref[...]
    mu = jnp.mean(x, axis=-1, keepdims=True)
    xc = x - mu
    var = jnp.mean(xc * xc, axis=-1, keepdims=True)
    o_ref[...] = xc * lax.rsqrt(var + LN_EPS) * g_ref[...] + b_ref[...]


def _layer_norm(x2d, g, b, tm=512):
    n, d = x2d.shape
    return pl.pallas_call(
        _ln_kernel,
        grid=(n // tm,),
        in_specs=[pl.BlockSpec((tm, d), lambda i: (i, 0)),
                  _const_spec((1, d)), _const_spec((1, d))],
        out_specs=pl.BlockSpec((tm, d), lambda i: (i, 0)),
        out_shape=jax.ShapeDtypeStruct((n, d), F32),
        compiler_params=_params(("parallel",), 4 * tm * d * 4 * 2),
        name="layer_norm",
    )(x2d, g.reshape(1, d), b.reshape(1, d))


def _inproj_kernel(x_ref, wa_ref, wb_ref, wz_ref, wab_ref, wg_ref,
                   q_ref, k_ref, v_ref, za_ref, xb_ref, zb_ref, ab_ref, ga_ref, gb_ref,
                   kt_ref, vt_ref):
    x = x_ref[0].astype(BF16)
    dot = functools.partial(jnp.dot, preferred_element_type=F32)
    ra = dot(x, wa_ref[...])
    q_ref[0] = ra[:, 0:A_WIDTH].astype(BF16)
    kf = ra[:, A_WIDTH:2 * A_WIDTH]
    vf = ra[:, 2 * A_WIDTH:3 * A_WIDTH]
    k_ref[0] = kf.astype(BF16)
    v_ref[0] = vf.astype(BF16)
    kt_ref[0] = kf
    vt_ref[0] = vf
    za_ref[0] = ra[:, 3 * A_WIDTH:4 * A_WIDTH]
    xb_ref[0] = dot(x, wb_ref[...])
    zb_ref[0] = dot(x, wz_ref[...])
    ab_ref[0] = dot(x, wab_ref[...])
    rg = dot(x, wg_ref[...])
    ga_ref[0] = rg[:, 0:D_MODEL]
    gb_ref[0] = rg[:, D_MODEL:2 * D_MODEL]


def _in_projection(h, w, *, tm, kv_pad, tail):
    bsz, t, d = h.shape
    nt = t // tm
    pad_blocks = kv_pad // tm
    tail_blocks = tail // tm

    def row(width):
        return pl.BlockSpec((1, tm, width), lambda b, i: (b, i, 0))

    kv_spec = pl.BlockSpec((1, tm, A_WIDTH), lambda b, i: (b, i + pad_blocks, 0))
    tail_spec = pl.BlockSpec((1, tm, A_WIDTH),
                             lambda b, i: (b, jnp.maximum(i - (nt - tail_blocks), 0), 0))
    out_shapes = [
        jax.ShapeDtypeStruct((bsz, t, A_WIDTH), BF16),
        jax.ShapeDtypeStruct((bsz, t + kv_pad, A_WIDTH), BF16),
        jax.ShapeDtypeStruct((bsz, t + kv_pad, A_WIDTH), BF16),
        jax.ShapeDtypeStruct((bsz, t, A_WIDTH), F32),
        jax.ShapeDtypeStruct((bsz, t, B_QKV), F32),
        jax.ShapeDtypeStruct((bsz, t, B_WIDTH), F32),
        jax.ShapeDtypeStruct((bsz, t, LANES), F32),
        jax.ShapeDtypeStruct((bsz, t, D_MODEL), F32),
        jax.ShapeDtypeStruct((bsz, t, D_MODEL), F32),
        jax.ShapeDtypeStruct((bsz, tail, A_WIDTH), F32),
        jax.ShapeDtypeStruct((bsz, tail, A_WIDTH), F32),
    ]
    out_specs = [row(A_WIDTH), kv_spec, kv_spec, row(A_WIDTH), row(B_QKV), row(B_WIDTH),
                 row(LANES), row(D_MODEL), row(D_MODEL), tail_spec, tail_spec]
    weights = (w["wa"], w["wb"], w["wz"], w["wab"], w["wg"])
    in_specs = [row(d)] + [_const_spec(x.shape) for x in weights]
    kz = jnp.zeros((bsz, t + kv_pad, A_WIDTH), BF16)
    vz = jnp.zeros((bsz, t + kv_pad, A_WIDTH), BF16)
    n_w = sum(int(np.prod(x.shape)) for x in weights)
    est = 2 * n_w + 2 * tm * (d * 4 + 3 * A_WIDTH * 2 + (2 * A_WIDTH + B_QKV + B_WIDTH + LANES
                                                        + 2 * D_MODEL + 2 * A_WIDTH) * 4)
    est += tm * (4 * A_WIDTH + 2 * D_MODEL) * 4 * 2
    return pl.pallas_call(
        _inproj_kernel_aliased,
        grid=(bsz, nt),
        in_specs=in_specs + [pl.BlockSpec(memory_space=pl.ANY), pl.BlockSpec(memory_space=pl.ANY)],
        out_specs=out_specs,
        out_shape=out_shapes,
        input_output_aliases={6: 1, 7: 2},
        compiler_params=_params(("parallel", "arbitrary"), est + (8 << 20)),
        name="in_projection",
    )(h, *weights, kz, vz)


def _inproj_kernel_aliased(x_ref, wa_ref, wb_ref, wz_ref, wab_ref, wg_ref, kz_ref, vz_ref, *outs):
    del kz_ref, vz_ref
    _inproj_kernel(x_ref, wa_ref, wb_ref, wz_ref, wab_ref, wg_ref, *outs)


def _attention_heads(q, k, v, za, bias_ref, kmask, o_ref):
    lane = lax.broadcasted_iota(jnp.int32, (1, LANES), 1)
    first = lane < A_HEAD_DIM
    scale = A_HEAD_DIM ** -0.5
    for p in range(HEAD_PAIRS):
        sl = slice(p * LANES, (p + 1) * LANES)
        qp, kp, vp = q[:, sl], k[:, sl], v[:, sl]
        outs = []
        for hh in range(2):
            keep = first if hh == 0 else jnp.logical_not(first)
            qm = jnp.where(keep, qp, jnp.zeros_like(qp))
            s = _mm_nt(qm, kp) * scale + bias_ref[2 * p + hh]
            if kmask is not None:
                s = s + kmask
            mx = jnp.max(s, axis=-1, keepdims=True)
            e = jnp.exp(s - mx)
            den = jnp.sum(e, axis=-1, keepdims=True)
            outs.append(_mm(e, vp) / den)
        o = jnp.where(first, outs[0], outs[1])
        o_ref[:, sl] = (o * _silu(za[:, sl])).astype(o_ref.dtype)


def _attn_prompt_kernel(q_ref, k_ref, v_ref, za_ref, bias_ref, o_ref, *, qblk, band):
    start = pl.multiple_of(pl.program_id(1) * qblk, qblk)
    kb = k_ref[0, pl.ds(start, band), :]
    vb = v_ref[0, pl.ds(start, band), :]
    col = lax.broadcasted_iota(jnp.int32, (1, band), 1) + start
    kmask = jnp.where(col >= ATT_WINDOW, 0.0, NEG).astype(F32)
    _attention_heads(q_ref[0], kb, vb, za_ref[0], bias_ref, kmask, o_ref.at[0])


def _rel_bias_table(rel_bias, nq, nk, n_before):
    h = rel_bias.shape[0]
    n_var = nq + REL_MAX
    assert nk - n_var >= 0 and n_before + nq == nk
    period = n_var + nq
    far = rel_bias[:, REL_MAX + CHUNK - 1:]
    n_low = n_var - (REL_MAX + CHUNK)
    g = jnp.concatenate([rel_bias[:, ::-1], jnp.repeat(rel_bias[:, :1], n_low, axis=1),
                         jnp.repeat(far, period - n_var, axis=1)], axis=1)
    skew = jnp.tile(g, (1, nq))[:, :nq * (period - 1)].reshape(h, nq, period - 1)[:, :, :n_var]
    const = jnp.broadcast_to(far[:, :, None], (h, nq, nk - n_var))
    return jnp.concatenate([const, skew], axis=2)


def _prompt_bias_table(rel_bias, qblk, band):
    i = np.arange(qblk)[:, None]
    j = np.arange(band)[None, :]
    qc, kc = i // CHUNK, j // CHUNK
    visible = (kc >= qc) & (kc <= qc + BAND_CHUNKS)
    tab = _rel_bias_table(rel_bias, qblk, band, ATT_WINDOW)
    return jnp.where(jnp.asarray(visible)[None], tab, NEG).astype(F32)


def _attention_prompt(q, kpad, vpad, za, rel_bias, qblk=2 * CHUNK):
    bsz, t, _ = q.shape
    band = ATT_WINDOW + qblk
    tab = _prompt_bias_table(rel_bias, qblk, band)
    tpad = kpad.shape[1]
    row = pl.BlockSpec((1, qblk, A_WIDTH), lambda b, i: (b, i, 0))
    full = pl.BlockSpec((1, tpad, A_WIDTH), lambda b, i: (b, 0, 0), pipeline_mode=pl.Buffered(1))
    est = 2 * tpad * A_WIDTH * 2 + tab.size * 4 + 8 * qblk * A_WIDTH * 4 + 16 * qblk * band * 4
    return pl.pallas_call(
        functools.partial(_attn_prompt_kernel, qblk=qblk, band=band),
        grid=(bsz, t // qblk),
        in_specs=[row, full, full, row, _const_spec(tab.shape)],
        out_specs=row,
        out_shape=jax.ShapeDtypeStruct((bsz, t, A_WIDTH), BF16),
        compiler_params=_params(("parallel", "arbitrary"), est + (8 << 20)),
        name="attention_prompt",
    )(q, kpad, vpad, za, tab)


def _attn_sample_kernel(q_ref, kc_ref, vc_ref, kn_ref, vn_ref, za_ref, bias_ref, o_ref, *, nk):
    n_past, t = kc_ref.shape[1], kn_ref.shape[1]
    zeros = jnp.zeros((nk - n_past - t, A_WIDTH), BF16)
    k = jnp.concatenate([kc_ref[0].astype(BF16), kn_ref[0].astype(BF16), zeros], axis=0)
    v = jnp.concatenate([vc_ref[0].astype(BF16), vn_ref[0].astype(BF16), zeros], axis=0)
    _attention_heads(q_ref[0], k, v, za_ref[0], bias_ref, None, o_ref.at[0])


def _attention_sample(q, k_cache, v_cache, k_new, v_new, za, rel_bias):
    bsz, t, _ = q.shape
    n_past = k_cache.shape[1]
    nk = -(-(n_past + t) // LANES) * LANES
    nq_tab = nk - n_past
    tab = _rel_bias_table(rel_bias, nq_tab, nk, n_past)[:, :t]
    tab = jnp.where(jnp.asarray(np.arange(nk) < n_past + t)[None, None], tab, NEG).astype(F32)

    def blk(rows):
        return pl.BlockSpec((1, rows, A_WIDTH), lambda b: (b, 0, 0))

    return pl.pallas_call(
        functools.partial(_attn_sample_kernel, nk=nk),
        grid=(bsz,),
        in_specs=[blk(t), blk(n_past), blk(n_past), blk(t), blk(t), blk(t), _const_spec(tab.shape)],
        out_specs=blk(t),
        out_shape=jax.ShapeDtypeStruct((bsz, t, A_WIDTH), BF16),
        compiler_params=_params(("parallel",), 32 << 20),
        name="attention_sample",
    )(q, k_cache, v_cache, k_new, v_new, za, tab)


def _gdn_kernel(xb_ref, ab_ref, zb_ref, conv0_ref, s0_ref, convw_ref, arow_ref, dtrow_ref, normw_ref,
                o_ref, s_out_ref, xbuf, cbuf, s_scr, u_scr, w_scr, qk_scr, qe_scr, kd_scr, sd_scr,
                *, tb, blk, n_par):
    t_idx = pl.program_id(1)
    hist = 8
    two = 2 * blk
    n_chunks = tb // blk

    @pl.when(t_idx == 0)
    def _():
        xbuf[0:hist, :] = conv0_ref[0]
        s_scr[...] = s0_ref[0]

    xbuf[hist:hist + tb, :] = xb_ref[0]
    acc = xbuf[hist - 3:hist - 3 + tb, :] * convw_ref[0:1, :]
    for i in range(1, CONV_W):
        acc = acc + xbuf[hist - 3 + i:hist - 3 + i + tb, :] * convw_ref[i:i + 1, :]
    cbuf[...] = _silu(acc)
    tail = xbuf[tb:tb + hist, :]
    xbuf[0:hist, :] = tail

    lane = lax.broadcasted_iota(jnp.int32, (1, LANES), 1)
    first = lane < B_HEAD_DIM
    r = lax.broadcasted_iota(jnp.int32, (two, two), 0)
    c = lax.broadcasted_iota(jnp.int32, (two, two), 1)
    causal = ((r // blk) == (c // blk)) & (r >= c)
    eye = (r == c).astype(F32)
    off_diag = 1.0 - eye

    def level_mask(bs):
        return ((r // bs) == (c // bs)) & ((r % bs) >= bs // 2) & ((c % bs) < bs // 2)

    ri = lax.broadcasted_iota(jnp.int32, (blk, blk), 0)
    ci = lax.broadcasted_iota(jnp.int32, (blk, blk), 1)
    lcum = (ri >= ci).astype(F32)

    def stack(x):
        zero = jnp.zeros_like(x)
        return jnp.concatenate([jnp.where(first, x, zero), jnp.where(first, zero, x)], axis=0)

    def pair_col(x, h0):
        return jnp.concatenate([x[:, h0:h0 + 1], x[:, h0 + 1:h0 + 2]], axis=0)

    def pair_last(x, h0, rows):
        n = x.shape[0]
        return jnp.concatenate([jnp.broadcast_to(x[n - 1:n, h0:h0 + 1], (rows, 1)),
                                jnp.broadcast_to(x[n - 1:n, h0 + 1:h0 + 2], (rows, 1))], axis=0)

    def phase1(step, carry):
        chains = []
        for dc in range(n_par):
            cidx = step * n_par + dc
            r0 = pl.multiple_of(cidx * blk, blk)
            ab = ab_ref[0, pl.ds(r0, blk), :]
            g = arow_ref[...] * jax.nn.softplus(ab + dtrow_ref[...])
            beta = jax.nn.sigmoid(ab)
            gc = _mm_exact01(lcum, g)
            for p in range(HEAD_PAIRS):
                chains.append({"cidx": cidx, "r0": r0, "p": p, "gc": gc, "beta": beta})
        for ch in chains:
            p, r0, gc = ch["p"], ch["r0"], ch["gc"]
            h0 = 2 * p
            gcol = pair_col(gc, h0)
            bcol = pair_col(ch["beta"], B_HEADS + h0)
            glast = pair_last(gc, h0, blk)
            qraw = stack(cbuf[pl.ds(r0, blk), p * LANES:(p + 1) * LANES])
            kraw = stack(cbuf[pl.ds(r0, blk), B_WIDTH + p * LANES:B_WIDTH + (p + 1) * LANES])
            vst = stack(cbuf[pl.ds(r0, blk), 2 * B_WIDTH + p * LANES:2 * B_WIDTH + (p + 1) * LANES])
            qst = qraw * (lax.rsqrt(jnp.sum(qraw * qraw, axis=-1, keepdims=True) + NORM_EPS)
                          * (B_HEAD_DIM ** -0.5))
            kst = kraw * lax.rsqrt(jnp.sum(kraw * kraw, axis=-1, keepdims=True) + NORM_EPS)
            kbeta = kst * bcol
            egc = jnp.exp(gcol)
            gb = jnp.broadcast_to(gcol, (two, two))
            ch["decay"] = jnp.exp(jnp.where(causal, gb - gb.T, NEG))
            ch["lhs"] = jnp.concatenate([kbeta, qst], axis=0).astype(BF16)
            ch["kst"] = kst.astype(BF16)
            ch["rhs"] = jnp.concatenate([vst * bcol, kbeta * egc], axis=1).astype(BF16)
            qe_scr[ch["cidx"], p] = (qst * egc).astype(BF16)
            kd_scr[ch["cidx"], p] = (kst * jnp.exp(glast - gcol)).astype(BF16)
            sd_scr[ch["cidx"], p] = jnp.broadcast_to(jnp.exp(pair_last(gc, h0, B_HEAD_DIM)), (LANES, LANES))
        for ch in chains:
            gram = _mm_nt(ch["lhs"], ch["kst"])
            ch["m"] = gram[0:two] * ch["decay"] * off_diag
            qk_scr[ch["cidx"], ch["p"]] = (gram[two:2 * two] * ch["decay"]).astype(BF16)
            ch["x"] = eye - jnp.where(level_mask(2), ch["m"], 0.0)
        bs = 4
        while bs <= blk:
            lm = level_mask(bs)
            for ch in chains:
                ch["t"] = _mm(jnp.where(lm, ch["m"], 0.0), ch["x"])
            for ch in chains:
                ch["x"] = ch["x"] - _mm(ch["x"], ch["t"])
            bs *= 2
        for ch in chains:
            uw = _mm(ch["x"], ch["rhs"])
            u_scr[ch["cidx"], ch["p"]] = uw[:, 0:LANES]
            w_scr[ch["cidx"], ch["p"]] = uw[:, LANES:2 * LANES].astype(BF16)
        return carry

    lax.fori_loop(0, n_chunks // n_par, phase1, 0)

    def phase2(cidx, carry):
        r0 = pl.multiple_of(cidx * blk, blk)
        pairs = range(HEAD_PAIRS)
        s_old = [s_scr[p] for p in pairs]
        ws_qs = [_mm(jnp.concatenate([w_scr[cidx, p], qe_scr[cidx, p]], axis=0), s_old[p]) for p in pairs]
        v_new = [u_scr[cidx, p] - ws_qs[p][0:two] for p in pairs]
        for p in pairs:
            s_scr[p] = s_old[p] * sd_scr[cidx, p] + _mm_tn(kd_scr[cidx, p], v_new[p])
        for p in pairs:
            o = ws_qs[p][two:2 * two] + _mm(qk_scr[cidx, p], v_new[p])
            ms = jnp.sum(o * o, axis=-1, keepdims=True) * (1.0 / B_HEAD_DIM)
            on = o * lax.rsqrt(ms + NORM_EPS) * normw_ref[...]
            o_pair = on[0:blk] + on[blk:two]
            zb = zb_ref[0, pl.ds(r0, blk), p * LANES:(p + 1) * LANES]
            o_ref[0, pl.ds(r0, blk), p * LANES:(p + 1) * LANES] = (o_pair * _silu(zb)).astype(o_ref.dtype)
        return carry

    lax.fori_loop(0, n_chunks, phase2, 0)

    @pl.when(t_idx == pl.num_programs(1) - 1)
    def _():
        s_out_ref[0] = s_scr[...]


def _pair_state(s):
    bsz = s.shape[0]
    s = s.reshape(bsz, HEAD_PAIRS, 2, B_HEAD_DIM, B_HEAD_DIM)
    z = jnp.zeros_like(s[:, :, 0])
    top = jnp.concatenate([s[:, :, 0], z], axis=-1)
    bot = jnp.concatenate([z, s[:, :, 1]], axis=-1)
    return jnp.concatenate([top, bot], axis=-2)


def _unpair_state(sp):
    d = B_HEAD_DIM
    s = jnp.stack([sp[:, :, 0:d, 0:d], sp[:, :, d:2 * d, d:2 * d]], axis=2)
    return s.reshape(sp.shape[0], B_HEADS, d, d)


def _gated_delta(xb, ab, zb, conv_prev, s_prev, lw, *, tb, blk):
    bsz, t, _ = xb.shape
    hist = 8
    conv0 = jnp.concatenate([jnp.zeros((bsz, hist - (CONV_W - 1), B_QKV), F32), conv_prev.astype(F32)], axis=1)
    s0 = _pair_state(s_prev.astype(F32))

    def row(width):
        return pl.BlockSpec((1, tb, width), lambda b, i: (b, i, 0))

    def per_batch(shape):
        nd = len(shape)
        return pl.BlockSpec((1,) + shape, lambda b, i: (b,) + (0,) * nd)

    n_chunks = tb // blk
    n_par = 2 if n_chunks % 2 == 0 else 1
    two = 2 * blk
    per_chain = (n_chunks, HEAD_PAIRS)
    est = 2 * tb * (B_QKV + LANES + B_WIDTH) * 4 + (2 * tb + hist) * B_QKV * 4 + 6 * HEAD_PAIRS * LANES * LANES * 4
    est += n_chunks * HEAD_PAIRS * LANES * LANES * 16
    o, s_out = pl.pallas_call(
        functools.partial(_gdn_kernel, tb=tb, blk=blk, n_par=n_par),
        grid=(bsz, t // tb),
        in_specs=[row(B_QKV), row(LANES), row(B_WIDTH), per_batch((hist, B_QKV)),
                  per_batch((HEAD_PAIRS, LANES, LANES)),
                  _const_spec((CONV_W, B_QKV)), _const_spec((1, LANES)), _const_spec((1, LANES)),
                  _const_spec((1, LANES))],
        out_specs=[row(B_WIDTH), per_batch((HEAD_PAIRS, LANES, LANES))],
        out_shape=[jax.ShapeDtypeStruct((bsz, t, B_WIDTH), BF16),
                   jax.ShapeDtypeStruct((bsz, HEAD_PAIRS, LANES, LANES), F32)],
        scratch_shapes=[pltpu.VMEM((hist + tb, B_QKV), F32), pltpu.VMEM((tb, B_QKV), F32),
                        pltpu.VMEM((HEAD_PAIRS, LANES, LANES), F32),
                        pltpu.VMEM(per_chain + (two, LANES), F32), pltpu.VMEM(per_chain + (two, LANES), BF16),
                        pltpu.VMEM(per_chain + (two, two), BF16), pltpu.VMEM(per_chain + (two, LANES), BF16),
                        pltpu.VMEM(per_chain + (two, LANES), BF16), pltpu.VMEM(per_chain + (LANES, LANES), F32)],
        compiler_params=_params(("parallel", "arbitrary"), est + (16 << 20)),
        name="gated_delta",
    )(xb, ab, zb, conv0, s0, lw["conv_w"], lw["a_row"], lw["dt_row"], lw["normw_row"])
    return o, _unpair_state(s_out)


def _out_kernel(x_ref, a_ref, b_ref, ga_ref, gb_ref, wa_ref, wb_ref, wo_ref, g_ref, beta_ref, y_ref):
    dot = functools.partial(jnp.dot, preferred_element_type=F32)
    ya = dot(a_ref[...], wa_ref[...])
    yb = dot(b_ref[...], wb_ref[...])
    mix = jax.nn.sigmoid(ga_ref[...]) * ya + jax.nn.sigmoid(gb_ref[...]) * yb
    out = dot(mix.astype(BF16), wo_ref[...])
    z = ALPHA * x_ref[...] + out
    mu = jnp.mean(z, axis=-1, keepdims=True)
    zc = z - mu
    var = jnp.mean(zc * zc, axis=-1, keepdims=True)
    y_ref[...] = zc * lax.rsqrt(var + LN_EPS) * g_ref[...] + beta_ref[...]


def _out_projection(x, a, b, ga, gb, w, tm):
    n, d = x.shape

    def row(width):
        return pl.BlockSpec((tm, width), lambda i: (i, 0))

    weights = (w["w_branch_a"], w["w_branch_b"], w["w_out"], w["ln_g"], w["ln_b"])
    est = 2 * tm * (4 * d * 4 + 2 * A_WIDTH * 2) + 2 * (2 * A_WIDTH * d + d * d) + 6 * tm * d * 4
    return pl.pallas_call(
        _out_kernel,
        grid=(n // tm,),
        in_specs=[row(d), row(A_WIDTH), row(B_WIDTH), row(d), row(d)] + [_const_spec(x.shape) for x in weights],
        out_specs=row(d),
        out_shape=jax.ShapeDtypeStruct((n, d), F32),
        compiler_params=_params(("parallel",), est + (8 << 20)),
        name="out_projection",
    )(x, a, b, ga, gb, *weights)


def _layer_weights(l, w_in, conv_w, a_log, dt_bias, gdn_norm_w, w_branch_a, w_branch_b, w_out, ln_g, ln_b):
    wi = w_in[l]
    c0 = 4 * A_WIDTH
    c1 = c0 + B_QKV
    c2 = c1 + B_WIDTH
    c3 = c2 + 2 * B_HEADS
    wab = jnp.pad(wi[:, c2:c3], ((0, 0), (0, LANES - 2 * B_HEADS)))
    pad_h = (0, LANES - B_HEADS)
    return {
        "wa": wi[:, 0:c0].astype(BF16), "wb": wi[:, c0:c1].astype(BF16), "wz": wi[:, c1:c2].astype(BF16),
        "wab": wab.astype(BF16), "wg": wi[:, c3:].astype(BF16),
        "conv_w": conv_w[l].astype(F32),
        "a_row": jnp.pad(-jnp.exp(a_log[l].astype(F32)), pad_h).reshape(1, LANES),
        "dt_row": jnp.pad(dt_bias[l].astype(F32), pad_h).reshape(1, LANES),
        "normw_row": jnp.tile(gdn_norm_w[l].astype(F32), 2).reshape(1, LANES),
        "w_branch_a": w_branch_a[l].astype(BF16), "w_branch_b": w_branch_b[l].astype(BF16),
        "w_out": w_out[l].astype(BF16),
        "ln_g": ln_g[l].astype(F32).reshape(1, D_MODEL), "ln_b": ln_b[l].astype(F32).reshape(1, D_MODEL),
    }


def _prompt_layer(h, lw, rel_bias, *, tm, tb):
    bsz, t, d = h.shape
    keep = min(ATT_WINDOW, t)
    q, kpad, vpad, za, xb, zb, ab, ga, gb, k_tail, v_tail = _in_projection(
        h, lw, tm=tm, kv_pad=ATT_WINDOW, tail=keep)
    oa = _attention_prompt(q, kpad, vpad, za, rel_bias)
    conv_zero = jnp.zeros((bsz, CONV_W - 1, B_QKV), F32)
    s_zero = jnp.zeros((bsz, B_HEADS, B_HEAD_DIM, B_HEAD_DIM), F32)
    ob, s_new = _gated_delta(xb, ab, zb, conv_zero, s_zero, lw, tb=tb, blk=min(t, GDN_CHUNK))
    y = _out_projection(h.reshape(bsz * t, d), oa.reshape(bsz * t, A_WIDTH), ob.reshape(bsz * t, B_WIDTH),
                        ga.reshape(bsz * t, d), gb.reshape(bsz * t, d), lw, tm)
    new_k = k_tail.reshape(bsz, keep, A_HEADS, A_HEAD_DIM)
    new_v = v_tail.reshape(bsz, keep, A_HEADS, A_HEAD_DIM)
    return y.reshape(bsz, t, d), (new_k, new_v, xb[:, t - (CONV_W - 1):], s_new)


def _sample_layer(h, k_cache, v_cache, conv_prev, s_prev, lw, rel_bias, *, tm):
    bsz, t, d = h.shape
    n = bsz * t
    n_past = k_cache.shape[1]
    q, _, _, za, xb, zb, ab, ga, gb, k_new, v_new = _in_projection(
        h.reshape(1, n, d), lw, tm=tm, kv_pad=0, tail=n)

    def per_seq(x):
        return x.reshape(bsz, t, x.shape[-1])

    oa = _attention_sample(per_seq(q), k_cache.reshape(bsz, n_past, A_WIDTH), v_cache.reshape(bsz, n_past, A_WIDTH),
                           per_seq(k_new), per_seq(v_new), per_seq(za), rel_bias)
    xb_s = per_seq(xb)
    xp_tail = jnp.concatenate([conv_prev.astype(F32), xb_s], axis=1)[:, -(CONV_W - 1):]
    ob, s_new = _gated_delta(xb_s, per_seq(ab), per_seq(zb), conv_prev, s_prev, lw, tb=t, blk=min(t, GDN_CHUNK))
    y = _out_projection(h.reshape(n, d), oa.reshape(n, A_WIDTH), ob.reshape(n, B_WIDTH),
                        ga.reshape(n, d), gb.reshape(n, d), lw, tm)
    new_k = per_seq(k_new).reshape(bsz, t, A_HEADS, A_HEAD_DIM)
    new_v = per_seq(v_new).reshape(bsz, t, A_HEADS, A_HEAD_DIM)
    return y.reshape(bsz, t, d), (new_k, new_v, xp_tail, s_new)


def kernel(x_prompt, x_sample, cache_attn_k, cache_attn_v, state_conv, state_gdn, ln0_g, ln0_b, w_in, rel_bias,
           conv_w, a_log, dt_bias, gdn_norm_w, w_branch_a, w_branch_b, w_out, ln_g, ln_b):
    bp, tp, d = x_prompt.shape
    bs, ts, _ = x_sample.shape
    hp = _layer_norm(x_prompt.reshape(bp * tp, d), ln0_g, ln0_b).reshape(bp, tp, d)
    hs = _layer_norm(x_sample.reshape(bs * ts, d), ln0_g, ln0_b).reshape(bs, ts, d)
    outs_p, outs_s = [], []
    for l in range(DEPTH):
        lw = _layer_weights(l, w_in, conv_w, a_log, dt_bias, gdn_norm_w, w_branch_a, w_branch_b, w_out, ln_g, ln_b)
        hp, st_p = _prompt_layer(hp, lw, rel_bias[l].astype(F32), tm=256, tb=256)
        hs, st_s = _sample_layer(hs, cache_attn_k[l], cache_attn_v[l], state_conv[l], state_gdn[l], lw,
                                 rel_bias[l].astype(F32), tm=256)
        outs_p.append(st_p)
        outs_s.append(st_s)

    def stacked(outs, i):
        return jnp.stack([o[i] for o in outs])

    return (hp, hs,
            stacked(outs_p, 0), stacked(outs_p, 1), stacked(outs_p, 2), stacked(outs_p, 3),
            stacked(outs_s, 0), stacked(outs_s, 1), stacked(outs_s, 2), stacked(outs_s, 3))
```

```python
import functools
import math

import numpy as np
import jax
import jax.numpy as jnp
from jax import lax
from jax.experimental import pallas as pl
from jax.experimental.pallas import tpu as pltpu

D_MODEL = 1024
DEPTH = 2
PAST_LEN = 4096
CHUNK = 64
BAND_CHUNKS = 8
ATT_WINDOW = BAND_CHUNKS * CHUNK
A_HEADS = 8
A_HEAD_DIM = 64
A_WIDTH = A_HEADS * A_HEAD_DIM
REL_MAX = 128
B_HEADS = 8
B_HEAD_DIM = 64
B_WIDTH = B_HEADS * B_HEAD_DIM
B_QKV = 3 * B_WIDTH
CONV_W = 4
GDN_CHUNK = 64
ALPHA = (2 * DEPTH) ** 0.25
LN_EPS = 1e-5
NORM_EPS = 1e-6

LANES = 128
SUBLANES = 8
HEAD_PAIRS = A_HEADS // 2
CONV_HIST = SUBLANES
NEG = -1e30
LOG2E = math.log2(math.e)
V7X_VMEM_LIMIT = 56 * 1024 * 1024

BF16 = jnp.bfloat16
F32 = jnp.float32


def _mm(a, b):
    return jnp.dot(a.astype(BF16), b.astype(BF16), preferred_element_type=F32)


def _mm_nt(a, b):
    return lax.dot_general(a.astype(BF16), b.astype(BF16), (((1,), (1,)), ((), ())),
                           preferred_element_type=F32)


def _mm_tn(a, b):
    return lax.dot_general(a.astype(BF16), b.astype(BF16), (((0,), (0,)), ((), ())),
                           preferred_element_type=F32)


def _split3(x):
    x1 = x.astype(BF16)
    r1 = x - x1.astype(F32)
    x2 = r1.astype(BF16)
    x3 = (r1 - x2.astype(F32)).astype(BF16)
    return x1, x2, x3


def _mm_exact01(w01, x):
    w = w01.astype(BF16)
    return sum(jnp.dot(w, p, preferred_element_type=F32) for p in _split3(x))


def _mm_exact01_right(x, w01):
    w = w01.astype(BF16)
    return sum(jnp.dot(p, w, preferred_element_type=F32) for p in _split3(x))


def _silu(x):
    return x * jax.nn.sigmoid(x)


def _first_head_lanes():
    return lax.broadcasted_iota(jnp.int32, (1, LANES), 1) < B_HEAD_DIM


def _params(sem, est_bytes):
    limit = int(min(V7X_VMEM_LIMIT, max(32 * 1024 * 1024, est_bytes)))
    return pltpu.CompilerParams(dimension_semantics=sem, vmem_limit_bytes=limit)


def _const_spec(shape):
    nd = len(shape)
    return pl.BlockSpec(shape, lambda *_: (0,) * nd, pipeline_mode=pl.Buffered(1))


def _ln_kernel(x_ref, g_ref, b_ref, o_ref):
    x = x_ref[...]
    mu = jnp.mean(x, axis=-1, keepdims=True)
    xc = x - mu
    var = jnp.mean(xc * xc, axis=-1, keepdims=True)
    o_ref[...] = xc * lax.rsqrt(var + LN_EPS) * g_ref[...] + b_ref[...]


def _layer_norm(x2d, g, b, tm=512):
    n, d = x2d.shape
    return pl.pallas_call(
        _ln_kernel,
        grid=(n // tm,),
        in_specs=[pl.BlockSpec((tm, d), lambda i: (i, 0)),
                  _const_spec((1, d)), _const_spec((1, d))],
        out_specs=pl.BlockSpec((tm, d), lambda i: (i, 0)),
        out_shape=jax.ShapeDtypeStruct((n, d), F32),
        compiler_params=_params(("parallel",), 4 * tm * d * 4 * 2),
        name="layer_norm",
    )(x2d, g.reshape(1, d), b.reshape(1, d))


def _conv_silu(xbuf, convw_ref, rows):
    first_tap = CONV_HIST - (CONV_W - 1)
    acc = xbuf[first_tap:first_tap + rows, :] * convw_ref[0:1, :]
    for i in range(1, CONV_W):
        acc = acc + xbuf[first_tap + i:first_tap + i + rows, :] * convw_ref[i:i + 1, :]
    tail = xbuf[rows:rows + CONV_HIST, :]
    xbuf[0:CONV_HIST, :] = tail
    return _silu(acc)


def _l2norm_heads(x, scale):
    first = _first_head_lanes()
    outs = []
    for p in range(HEAD_PAIRS):
        xs = x[:, p * LANES:(p + 1) * LANES]
        x2 = xs * xs
        zero = jnp.zeros_like(x2)
        s0 = jnp.sum(jnp.where(first, x2, zero), axis=-1, keepdims=True)
        s1 = jnp.sum(jnp.where(first, zero, x2), axis=-1, keepdims=True)
        inv = jnp.where(first, lax.rsqrt(s0 + NORM_EPS), lax.rsqrt(s1 + NORM_EPS))
        outs.append(xs * (inv * scale))
    return jnp.concatenate(outs, axis=1)


def _conv_qkv(xbuf, convw_ref, rows):
    c = _conv_silu(xbuf, convw_ref, rows)
    qn = _l2norm_heads(c[:, 0:B_WIDTH], B_HEAD_DIM ** -0.5)
    kn = _l2norm_heads(c[:, B_WIDTH:2 * B_WIDTH], 1.0)
    return jnp.concatenate([qn, kn, c[:, 2 * B_WIDTH:]], axis=1)


def _inproj_kernel(x_ref, wa_ref, wb_ref, wz_ref, wab_ref, wg_ref, convw_ref,
                   q_ref, k_ref, v_ref, za_ref, xb_ref, zb_ref, ab_ref, ga_ref, gb_ref,
                   kt_ref, vt_ref, xtail_ref, xbuf, *, fuse_conv):
    tm = x_ref.shape[1]
    x = x_ref[0].astype(BF16)
    dot = functools.partial(jnp.dot, preferred_element_type=F32)
    xb = dot(x, wb_ref[...])
    if fuse_conv:
        @pl.when(pl.program_id(1) == 0)
        def _():
            xbuf[0:CONV_HIST, :] = jnp.zeros((CONV_HIST, B_QKV), F32)

        xbuf[CONV_HIST:CONV_HIST + tm, :] = xb
        xtail_ref[0] = xb[tm - CONV_HIST:tm, :]
        xb_ref[0] = _conv_qkv(xbuf, convw_ref, tm)
    else:
        xtail_ref[0] = xb[tm - CONV_HIST:tm, :]
        xb_ref[0] = xb
    ra = dot(x, wa_ref[...])
    q_ref[0] = ra[:, 0:A_WIDTH].astype(BF16)
    kf = ra[:, A_WIDTH:2 * A_WIDTH]
    vf = ra[:, 2 * A_WIDTH:3 * A_WIDTH]
    k_ref[0] = kf.astype(BF16)
    v_ref[0] = vf.astype(BF16)
    kt_ref[0] = kf
    vt_ref[0] = vf
    za_ref[0] = ra[:, 3 * A_WIDTH:4 * A_WIDTH]
    zb_ref[0] = dot(x, wz_ref[...])
    ab_ref[0] = dot(x, wab_ref[...])
    rg = dot(x, wg_ref[...])
    ga_ref[0] = rg[:, 0:D_MODEL]
    gb_ref[0] = rg[:, D_MODEL:2 * D_MODEL]


def _in_projection(h, w, *, tm, tail, fuse_conv):
    bsz, t, d = h.shape
    nt = t // tm
    tail_blocks = tail // tm

    def row(width):
        return pl.BlockSpec((1, tm, width), lambda b, i: (b, i, 0))

    tail_spec = pl.BlockSpec((1, tm, A_WIDTH),
                             lambda b, i: (b, jnp.maximum(i - (nt - tail_blocks), 0), 0))
    xtail_spec = pl.BlockSpec((1, CONV_HIST, B_QKV), lambda b, i: (b, 0, 0))
    out_shapes = [
        jax.ShapeDtypeStruct((bsz, t, A_WIDTH), BF16),
        jax.ShapeDtypeStruct((bsz, t, A_WIDTH), BF16),
        jax.ShapeDtypeStruct((bsz, t, A_WIDTH), BF16),
        jax.ShapeDtypeStruct((bsz, t, A_WIDTH), F32),
        jax.ShapeDtypeStruct((bsz, t, B_QKV), F32),
        jax.ShapeDtypeStruct((bsz, t, B_WIDTH), F32),
        jax.ShapeDtypeStruct((bsz, t, LANES), F32),
        jax.ShapeDtypeStruct((bsz, t, D_MODEL), F32),
        jax.ShapeDtypeStruct((bsz, t, D_MODEL), F32),
        jax.ShapeDtypeStruct((bsz, tail, A_WIDTH), F32),
        jax.ShapeDtypeStruct((bsz, tail, A_WIDTH), F32),
        jax.ShapeDtypeStruct((bsz, CONV_HIST, B_QKV), F32),
    ]
    out_specs = [row(A_WIDTH), row(A_WIDTH), row(A_WIDTH), row(A_WIDTH), row(B_QKV), row(B_WIDTH),
                 row(LANES), row(D_MODEL), row(D_MODEL), tail_spec, tail_spec, xtail_spec]
    weights = (w["wa"], w["wb"], w["wz"], w["wab"], w["wg"], w["conv_w"])
    in_specs = [row(d)] + [_const_spec(x.shape) for x in weights]
    n_w = sum(int(np.prod(x.shape)) for x in weights)
    est = 2 * n_w + 2 * tm * (d * 4 + 3 * A_WIDTH * 2 + (2 * A_WIDTH + B_QKV + B_WIDTH + LANES
                                                        + 2 * D_MODEL + 2 * A_WIDTH) * 4)
    est += tm * (4 * A_WIDTH + 2 * D_MODEL + 4 * B_QKV) * 4 * 2
    return pl.pallas_call(
        functools.partial(_inproj_kernel, fuse_conv=fuse_conv),
        grid=(bsz, nt),
        in_specs=in_specs,
        out_specs=out_specs,
        out_shape=out_shapes,
        scratch_shapes=[pltpu.VMEM((CONV_HIST + tm, B_QKV), F32)],
        compiler_params=_params(("parallel", "arbitrary"), est + (8 << 20)),
        name="in_projection",
    )(h, *weights)


def _attention_heads(q, k, v, za, bias_ref, valid, o_ref):
    nq = q.shape[0]
    first = _first_head_lanes()
    c1 = (A_HEAD_DIM ** -0.5) * LOG2E
    for p in range(HEAD_PAIRS):
        sl = slice(p * LANES, (p + 1) * LANES)
        qp, kp, vp = q[:, sl], k[:, sl], v[:, sl]
        zero = jnp.zeros_like(qp)
        q2 = jnp.concatenate([jnp.where(first, qp, zero), jnp.where(first, zero, qp)], axis=0)
        s = _mm_nt(q2, kp) * c1 + bias_ref[p]
        if valid is not None:
            s = jnp.where(valid, s, NEG)
        mx = jnp.max(s, axis=-1, keepdims=True)
        e = jnp.exp2(s - mx)
        den = jnp.sum(e, axis=-1, keepdims=True)
        pv = _mm(e, vp) / den
        o = jnp.where(first, pv[0:nq], pv[nq:2 * nq])
        o_ref[:, sl] = (o * _silu(za[:, sl])).astype(o_ref.dtype)


def _attn_prompt_kernel(q_ref, *refs, qblk, n_kblk):
    k_refs, v_refs = refs[0:n_kblk], refs[n_kblk:2 * n_kblk]
    za_ref, bias_ref, o_ref = refs[2 * n_kblk:]
    m = pl.program_id(1)
    k = jnp.concatenate([r[0] for r in k_refs], axis=0)
    v = jnp.concatenate([r[0] for r in v_refs], axis=0)
    band = n_kblk * qblk
    n_lead = n_kblk - 1

    @pl.when(m >= n_lead)
    def _():
        _attention_heads(q_ref[0], k, v, za_ref[0], bias_ref, None, o_ref.at[0])

    @pl.when(m < n_lead)
    def _():
        col = lax.broadcasted_iota(jnp.int32, (1, band), 1) + (m - n_lead) * qblk
        _attention_heads(q_ref[0], k, v, za_ref[0], bias_ref, col >= 0, o_ref.at[0])


def _rel_bias_table(rel_bias, nq, nk, n_before):
    h = rel_bias.shape[0]
    n_var = nq + REL_MAX
    assert nk - n_var >= 0 and n_before + nq == nk
    period = n_var + nq
    far = rel_bias[:, REL_MAX + CHUNK - 1:]
    n_low = n_var - (REL_MAX + CHUNK)
    g = jnp.concatenate([rel_bias[:, ::-1], jnp.repeat(rel_bias[:, :1], n_low, axis=1),
                         jnp.repeat(far, period - n_var, axis=1)], axis=1)
    skew = jnp.tile(g, (1, nq))[:, :nq * (period - 1)].reshape(h, nq, period - 1)[:, :, :n_var]
    const = jnp.broadcast_to(far[:, :, None], (h, nq, nk - n_var))
    return jnp.concatenate([const, skew], axis=2)


def _pair_tables(tab):
    h, nq, nk = tab.shape
    return (tab * LOG2E).astype(F32).reshape(h // 2, 2 * nq, nk)


def _attention_prompt(q, k, v, za, rel_bias, qblk=2 * CHUNK):
    bsz, t, _ = q.shape
    n_kblk = ATT_WINDOW // qblk + 1
    band = n_kblk * qblk
    i = np.arange(qblk)[:, None]
    j = np.arange(band)[None, :]
    qc, kc = i // CHUNK, j // CHUNK
    visible = (kc >= qc) & (kc <= qc + BAND_CHUNKS)
    tab = _rel_bias_table(rel_bias, qblk, band, ATT_WINDOW)
    tab = _pair_tables(jnp.where(jnp.asarray(visible)[None], tab, NEG))
    row = pl.BlockSpec((1, qblk, A_WIDTH), lambda b, m: (b, m, 0))

    def key_block(jb):
        return pl.BlockSpec((1, qblk, A_WIDTH), lambda b, m: (b, jnp.maximum(m - (n_kblk - 1) + jb, 0), 0))

    kv_specs = [key_block(jb) for jb in range(n_kblk)]
    est = tab.size * 4 + 2 * (2 * n_kblk + 1) * qblk * A_WIDTH * 2 + 4 * qblk * A_WIDTH * 4 + 24 * qblk * band * 4
    return pl.pallas_call(
        functools.partial(_attn_prompt_kernel, qblk=qblk, n_kblk=n_kblk),
        grid=(bsz, t // qblk),
        in_specs=[row] + kv_specs + kv_specs + [row, _const_spec(tab.shape)],
        out_specs=row,
        out_shape=jax.ShapeDtypeStruct((bsz, t, A_WIDTH), BF16),
        compiler_params=_params(("parallel", "arbitrary"), est + (8 << 20)),
        name="attention_prompt",
    )(q, *([k] * n_kblk), *([v] * n_kblk), za, tab)


def _attn_sample_kernel(q_ref, kc_ref, vc_ref, kn_ref, vn_ref, za_ref, bias_ref, o_ref, *, nk):
    n_past, t = kc_ref.shape[1], kn_ref.shape[1]
    zeros = jnp.zeros((nk - n_past - t, A_WIDTH), BF16)
    k = jnp.concatenate([kc_ref[0].astype(BF16), kn_ref[0].astype(BF16), zeros], axis=0)
    v = jnp.concatenate([vc_ref[0].astype(BF16), vn_ref[0].astype(BF16), zeros], axis=0)
    _attention_heads(q_ref[0], k, v, za_ref[0], bias_ref, None, o_ref.at[0])


def _attention_sample(q, k_cache, v_cache, k_new, v_new, za, rel_bias):
    bsz, t, _ = q.shape
    n_past = k_cache.shape[1]
    nk = -(-(n_past + t) // LANES) * LANES
    nq_tab = nk - n_past
    tab = _rel_bias_table(rel_bias, nq_tab, nk, n_past)[:, :t]
    tab = _pair_tables(jnp.where(jnp.asarray(np.arange(nk) < n_past + t)[None, None], tab, NEG))

    def blk(rows):
        return pl.BlockSpec((1, rows, A_WIDTH), lambda b: (b, 0, 0))

    return pl.pallas_call(
        functools.partial(_attn_sample_kernel, nk=nk),
        grid=(bsz,),
        in_specs=[blk(t), blk(n_past), blk(n_past), blk(t), blk(t), blk(t), _const_spec(tab.shape)],
        out_specs=blk(t),
        out_shape=jax.ShapeDtypeStruct((bsz, t, A_WIDTH), BF16),
        compiler_params=_params(("parallel",), 32 << 20),
        name="attention_sample",
    )(q, k_cache, v_cache, k_new, v_new, za, tab)


_KB, _QN, _KN, _VB, _KBE, _QE, _KD = range(7)


def _gdn_kernel(xb_ref, ab_ref, zb_ref, conv0_ref, s0_ref, convw_ref, arow_ref, dtrow_ref, normw_ref,
                o_ref, s_out_ref, xbuf, s_scr, nat_scr, gcx_scr, sdec_scr, u_scr, w_scr, qk_scr,
                *, tb, n_valid, do_conv, n_par):
    t_idx = pl.program_id(1)
    blk = GDN_CHUNK
    two = 2 * blk
    n_chunks = tb // blk

    @pl.when(t_idx == 0)
    def _():
        s_scr[...] = s0_ref[0]
        if do_conv:
            xbuf[0:CONV_HIST, :] = conv0_ref[0]

    def padded(x):
        if n_valid == tb:
            return x
        return jnp.concatenate([x, jnp.zeros((tb - n_valid, x.shape[1]), x.dtype)], axis=0)

    if do_conv:
        xbuf[CONV_HIST:CONV_HIST + tb, :] = padded(xb_ref[0])
        c = _conv_qkv(xbuf, convw_ref, tb)
    else:
        c = xb_ref[0]
    qn, kn, v = c[:, 0:B_WIDTH], c[:, B_WIDTH:2 * B_WIDTH], c[:, 2 * B_WIDTH:]

    ab = padded(ab_ref[0])
    g = arow_ref[...] * jax.nn.softplus(ab + dtrow_ref[...])
    beta = jax.nn.sigmoid(ab)
    if n_valid != tb:
        is_token = lax.broadcasted_iota(jnp.int32, (tb, 1), 0) < n_valid
        g = jnp.where(is_token, g, 0.0)
        beta = jnp.where(is_token, beta, 0.0)
    rt = lax.broadcasted_iota(jnp.int32, (tb, tb), 0)
    ct = lax.broadcasted_iota(jnp.int32, (tb, tb), 1)
    gc = _mm_exact01(((rt // blk) == (ct // blk)) & (rt >= ct), g)
    src = lax.broadcasted_iota(jnp.int32, (LANES, B_WIDTH), 0)
    head = lax.broadcasted_iota(jnp.int32, (LANES, B_WIDTH), 1) // B_HEAD_DIM
    gcx = _mm_exact01_right(gc, src == head)
    betax = _mm_exact01_right(beta, src == head + B_HEADS)
    glx = jnp.concatenate([jnp.broadcast_to(gcx[(i + 1) * blk - 1:(i + 1) * blk, :], (blk, B_WIDTH))
                           for i in range(n_chunks)], axis=0)
    egc = jnp.exp(gcx)
    kbeta = kn * betax
    nat_scr[_KB] = kbeta.astype(BF16)
    nat_scr[_QN] = qn.astype(BF16)
    nat_scr[_KN] = kn.astype(BF16)
    nat_scr[_VB] = (v * betax).astype(BF16)
    nat_scr[_KBE] = (kbeta * egc).astype(BF16)
    nat_scr[_QE] = (qn * egc).astype(BF16)
    nat_scr[_KD] = (kn * jnp.exp(glx - gcx)).astype(BF16)
    gcx_scr[...] = gcx
    for i in range(n_chunks):
        sdec_scr[i] = jnp.broadcast_to(jnp.exp(gcx[(i + 1) * blk - 1:(i + 1) * blk, :]), (SUBLANES, B_WIDTH))

    first = _first_head_lanes()
    r = lax.broadcasted_iota(jnp.int32, (two, two), 0)
    cc = lax.broadcasted_iota(jnp.int32, (two, two), 1)
    causal = ((r // blk) == (cc // blk)) & (r >= cc)
    eye = (r == cc).astype(F32)
    off_diag = 1.0 - eye
    diag = (lax.broadcasted_iota(jnp.int32, (blk, LANES), 0)
            == lax.broadcasted_iota(jnp.int32, (blk, LANES), 1) % B_HEAD_DIM)

    def level_mask(bs):
        return ((r // bs) == (cc // bs)) & ((r % bs) >= bs // 2) & ((cc % bs) < bs // 2)

    def stacked(which, r0, p):
        x = nat_scr[which, pl.ds(r0, blk), p * LANES:(p + 1) * LANES]
        zero = jnp.zeros_like(x)
        return jnp.concatenate([jnp.where(first, x, zero), jnp.where(first, zero, x)], axis=0)

    def phase1(step, carry):
        chains = []
        for dc in range(n_par):
            cidx = step * n_par + dc
            r0 = pl.multiple_of(cidx * blk, blk)
            for p in range(HEAD_PAIRS):
                chains.append({"cidx": cidx, "r0": r0, "p": p})
        for ch in chains:
            r0, p = ch["r0"], ch["p"]
            gt = gcx_scr[pl.ds(r0, blk), p * LANES:(p + 1) * LANES]
            gc_row = jnp.sum(jnp.where(diag, gt, 0.0), axis=0, keepdims=True)
            ch["decay"] = jnp.exp(jnp.where(causal, jnp.concatenate([gt, gt], axis=0) - gc_row, NEG))
            lhs = jnp.concatenate([stacked(_KB, r0, p), stacked(_QN, r0, p)], axis=0)
            gram = _mm_nt(lhs, stacked(_KN, r0, p))
            ch["m"] = gram[0:two] * ch["decay"] * off_diag
            qk_scr[ch["cidx"], p] = (gram[two:2 * two] * ch["decay"]).astype(BF16)
            ch["x"] = eye - jnp.where(level_mask(2), ch["m"], 0.0)
        bs = 4
        while bs <= blk:
            lm = level_mask(bs)
            for ch in chains:
                ch["t"] = _mm(jnp.where(lm, ch["m"], 0.0), ch["x"])
            for ch in chains:
                ch["x"] = ch["x"] - _mm(ch["x"], ch["t"])
            bs *= 2
        for ch in chains:
            r0, p = ch["r0"], ch["p"]
            rhs = jnp.concatenate([stacked(_VB, r0, p), stacked(_KBE, r0, p)], axis=1)
            uw = _mm(ch["x"], rhs)
            u_scr[ch["cidx"], p] = uw[:, 0:LANES]
            w_scr[ch["cidx"], p] = uw[:, LANES:2 * LANES].astype(BF16)
        return carry

    lax.fori_loop(0, n_chunks // n_par, phase1, 0)

    rows_out = min(blk, n_valid)

    def phase2(cidx, carry):
        r0 = pl.multiple_of(cidx * blk, blk)
        pairs = range(HEAD_PAIRS)
        s_old = [s_scr[p] for p in pairs]
        ws_qs = [_mm(jnp.concatenate([w_scr[cidx, p], stacked(_QE, r0, p)], axis=0), s_old[p]) for p in pairs]
        v_new = [u_scr[cidx, p] - ws_qs[p][0:two] for p in pairs]
        for p in pairs:
            dec = sdec_scr[cidx, 0:1, p * LANES:(p + 1) * LANES]
            s_scr[p] = s_old[p] * dec + _mm_tn(stacked(_KD, r0, p), v_new[p])
        for p in pairs:
            o = ws_qs[p][two:2 * two] + _mm(qk_scr[cidx, p], v_new[p])
            ms = jnp.sum(o * o, axis=-1, keepdims=True) * (1.0 / B_HEAD_DIM)
            on = o * lax.rsqrt(ms + NORM_EPS) * normw_ref[...]
            o_pair = (on[0:blk] + on[blk:two])[0:rows_out]
            zb = zb_ref[0, pl.ds(r0, rows_out), p * LANES:(p + 1) * LANES]
            o_ref[0, pl.ds(r0, rows_out), p * LANES:(p + 1) * LANES] = (o_pair * _silu(zb)).astype(o_ref.dtype)
        return carry

    lax.fori_loop(0, n_chunks, phase2, 0)

    @pl.when(t_idx == pl.num_programs(1) - 1)
    def _():
        s_out_ref[0] = s_scr[...]


def _pair_state(s):
    bsz = s.shape[0]
    s = s.reshape(bsz, HEAD_PAIRS, 2, B_HEAD_DIM, B_HEAD_DIM)
    z = jnp.zeros_like(s[:, :, 0])
    top = jnp.concatenate([s[:, :, 0], z], axis=-1)
    bot = jnp.concatenate([z, s[:, :, 1]], axis=-1)
    return jnp.concatenate([top, bot], axis=-2)


def _unpair_state(sp):
    d = B_HEAD_DIM
    s = jnp.stack([sp[:, :, 0:d, 0:d], sp[:, :, d:2 * d, d:2 * d]], axis=2)
    return s.reshape(sp.shape[0], B_HEADS, d, d)


def _gated_delta(xb, ab, zb, conv_prev, s_prev, lw, *, tb, do_conv):
    bsz, t, _ = xb.shape
    n_valid = min(t, tb)
    assert t % n_valid == 0 and tb % GDN_CHUNK == 0 and (n_valid == tb or t == n_valid)
    conv0 = jnp.concatenate([jnp.zeros((bsz, CONV_HIST - (CONV_W - 1), B_QKV), F32), conv_prev.astype(F32)], axis=1)
    s0 = _pair_state(s_prev.astype(F32))

    def row(width):
        return pl.BlockSpec((1, n_valid, width), lambda b, i: (b, i, 0))

    def per_batch(shape):
        nd = len(shape)
        return pl.BlockSpec((1,) + shape, lambda b, i: (b,) + (0,) * nd)

    n_chunks = tb // GDN_CHUNK
    n_par = max(p for p in (1, 2, 4) if n_chunks % p == 0)
    per_chain = (n_chunks, HEAD_PAIRS, LANES, LANES)
    est = 2 * n_valid * (B_QKV + LANES + B_WIDTH) * 4 + (tb + CONV_HIST) * B_QKV * 4 + 6 * HEAD_PAIRS * LANES * LANES * 4
    est += tb * B_WIDTH * (7 * 2 + 4 + 16 * 4) + n_chunks * HEAD_PAIRS * LANES * LANES * 8 + tb * tb * 8
    o, s_out = pl.pallas_call(
        functools.partial(_gdn_kernel, tb=tb, n_valid=n_valid, do_conv=do_conv, n_par=n_par),
        grid=(bsz, t // n_valid),
        in_specs=[row(B_QKV), row(LANES), row(B_WIDTH), per_batch((CONV_HIST, B_QKV)),
                  per_batch((HEAD_PAIRS, LANES, LANES)),
                  _const_spec((CONV_W, B_QKV)), _const_spec((1, LANES)), _const_spec((1, LANES)),
                  _const_spec((1, LANES))],
        out_specs=[row(B_WIDTH), per_batch((HEAD_PAIRS, LANES, LANES))],
        out_shape=[jax.ShapeDtypeStruct((bsz, t, B_WIDTH), BF16),
                   jax.ShapeDtypeStruct((bsz, HEAD_PAIRS, LANES, LANES), F32)],
        scratch_shapes=[pltpu.VMEM((CONV_HIST + tb, B_QKV), F32),
                        pltpu.VMEM((HEAD_PAIRS, LANES, LANES), F32),
                        pltpu.VMEM((7, tb, B_WIDTH), BF16),
                        pltpu.VMEM((tb, B_WIDTH), F32),
                        pltpu.VMEM((n_chunks, SUBLANES, B_WIDTH), F32),
                        pltpu.VMEM(per_chain, F32), pltpu.VMEM(per_chain, BF16), pltpu.VMEM(per_chain, BF16)],
        compiler_params=_params(("parallel", "arbitrary"), est + (16 << 20)),
        name="gated_delta",
    )(xb, ab, zb, conv0, s0, lw["conv_w"], lw["a_row"], lw["dt_row"], lw["normw_row"])
    return o, _unpair_state(s_out)


def _out_kernel(x_ref, a_ref, b_ref, ga_ref, gb_ref, wa_ref, wb_ref, wo_ref, g_ref, beta_ref, y_ref):
    dot = functools.partial(jnp.dot, preferred_element_type=F32)
    ya = dot(a_ref[...], wa_ref[...])
    yb = dot(b_ref[...], wb_ref[...])
    mix = jax.nn.sigmoid(ga_ref[...]) * ya + jax.nn.sigmoid(gb_ref[...]) * yb
    out = dot(mix.astype(BF16), wo_ref[...])
    z = ALPHA * x_ref[...] + out
    mu = jnp.mean(z, axis=-1, keepdims=True)
    zc = z - mu
    var = jnp.mean(zc * zc, axis=-1, keepdims=True)
    y_ref[...] = zc * lax.rsqrt(var + LN_EPS) * g_ref[...] + beta_ref[...]


def _out_projection(x, a, b, ga, gb, w, tm):
    n, d = x.shape

    def row(width):
        return pl.BlockSpec((tm, width), lambda i: (i, 0))

    weights = (w["w_branch_a"], w["w_branch_b"], w["w_out"], w["ln_g"], w["ln_b"])
    est = 2 * tm * (4 * d * 4 + 2 * A_WIDTH * 2) + 2 * (2 * A_WIDTH * d + d * d) + 6 * tm * d * 4
    return pl.pallas_call(
        _out_kernel,
        grid=(n // tm,),
        in_specs=[row(d), row(A_WIDTH), row(B_WIDTH), row(d), row(d)] + [_const_spec(x.shape) for x in weights],
        out_specs=row(d),
        out_shape=jax.ShapeDtypeStruct((n, d), F32),
        compiler_params=_params(("parallel",), est + (8 << 20)),
        name="out_projection",
    )(x, a, b, ga, gb, *weights)


def _layer_weights(l, w_in, conv_w, a_log, dt_bias, gdn_norm_w, w_branch_a, w_branch_b, w_out, ln_g, ln_b):
    wi = w_in[l]
    c0 = 4 * A_WIDTH
    c1 = c0 + B_QKV
    c2 = c1 + B_WIDTH
    c3 = c2 + 2 * B_HEADS
    wab = jnp.pad(wi[:, c2:c3], ((0, 0), (0, LANES - 2 * B_HEADS)))
    pad_h = (0, LANES - B_HEADS)
    return {
        "wa": wi[:, 0:c0].astype(BF16), "wb": wi[:, c0:c1].astype(BF16), "wz": wi[:, c1:c2].astype(BF16),
        "wab": wab.astype(BF16), "wg": wi[:, c3:].astype(BF16),
        "conv_w": conv_w[l].astype(F32),
        "a_row": jnp.pad(-jnp.exp(a_log[l].astype(F32)), pad_h).reshape(1, LANES),
        "dt_row": jnp.pad(dt_bias[l].astype(F32), pad_h).reshape(1, LANES),
        "normw_row": jnp.tile(gdn_norm_w[l].astype(F32), 2).reshape(1, LANES),
        "w_branch_a": w_branch_a[l].astype(BF16), "w_branch_b": w_branch_b[l].astype(BF16),
        "w_out": w_out[l].astype(BF16),
        "ln_g": ln_g[l].astype(F32).reshape(1, D_MODEL), "ln_b": ln_b[l].astype(F32).reshape(1, D_MODEL),
    }


def _prompt_layer(h, lw, rel_bias, *, tm, tb):
    bsz, t, d = h.shape
    keep = min(ATT_WINDOW, t)
    q, k, v, za, xb, zb, ab, ga, gb, k_tail, v_tail, x_tail = _in_projection(
        h, lw, tm=tm, tail=keep, fuse_conv=True)
    oa = _attention_prompt(q, k, v, za, rel_bias)
    conv_zero = jnp.zeros((bsz, CONV_W - 1, B_QKV), F32)
    s_zero = jnp.zeros((bsz, B_HEADS, B_HEAD_DIM, B_HEAD_DIM), F32)
    ob, s_new = _gated_delta(xb, ab, zb, conv_zero, s_zero, lw, tb=tb, do_conv=False)
    y = _out_projection(h.reshape(bsz * t, d), oa.reshape(bsz * t, A_WIDTH), ob.reshape(bsz * t, B_WIDTH),
                        ga.reshape(bsz * t, d), gb.reshape(bsz * t, d), lw, tm)
    new_k = k_tail.reshape(bsz, keep, A_HEADS, A_HEAD_DIM)
    new_v = v_tail.reshape(bsz, keep, A_HEADS, A_HEAD_DIM)
    return y.reshape(bsz, t, d), (new_k, new_v, x_tail[:, CONV_HIST - (CONV_W - 1):], s_new)


def _sample_layer(h, k_cache, v_cache, conv_prev, s_prev, lw, rel_bias, *, tm):
    bsz, t, d = h.shape
    n = bsz * t
    n_past = k_cache.shape[1]
    q, _, _, za, xb, zb, ab, ga, gb, k_new, v_new, _ = _in_projection(
        h.reshape(1, n, d), lw, tm=tm, tail=n, fuse_conv=False)

    def per_seq(x):
        return x.reshape(bsz, t, x.shape[-1])

    oa = _attention_sample(per_seq(q), k_cache.reshape(bsz, n_past, A_WIDTH), v_cache.reshape(bsz, n_past, A_WIDTH),
                           per_seq(k_new), per_seq(v_new), per_seq(za), rel_bias)
    xb_s = per_seq(xb)
    xp_tail = jnp.concatenate([conv_prev.astype(F32), xb_s], axis=1)[:, -(CONV_W - 1):]
    ob, s_new = _gated_delta(xb_s, per_seq(ab), per_seq(zb), conv_prev, s_prev, lw, tb=GDN_CHUNK, do_conv=True)
    y = _out_projection(h.reshape(n, d), oa.reshape(n, A_WIDTH), ob.reshape(n, B_WIDTH),
                        ga.reshape(n, d), gb.reshape(n, d), lw, tm)
    new_k = per_seq(k_new).reshape(bsz, t, A_HEADS, A_HEAD_DIM)
    new_v = per_seq(v_new).reshape(bsz, t, A_HEADS, A_HEAD_DIM)
    return y.reshape(bsz, t, d), (new_k, new_v, xp_tail, s_new)


def kernel(x_prompt, x_sample, cache_attn_k, cache_attn_v, state_conv, state_gdn, ln0_g, ln0_b, w_in, rel_bias,
           conv_w, a_log, dt_bias, gdn_norm_w, w_branch_a, w_branch_b, w_out, ln_g, ln_b):
    bp, tp, d = x_prompt.shape
    bs, ts, _ = x_sample.shape
    hp = _layer_norm(x_prompt.reshape(bp * tp, d), ln0_g, ln0_b).reshape(bp, tp, d)
    hs = _layer_norm(x_sample.reshape(bs * ts, d), ln0_g, ln0_b).reshape(bs, ts, d)
    outs_p, outs_s = [], []
    for l in range(DEPTH):
        lw = _layer_weights(l, w_in, conv_w, a_log, dt_bias, gdn_norm_w, w_branch_a, w_branch_b, w_out, ln_g, ln_b)
        hp, st_p = _prompt_layer(hp, lw, rel_bias[l].astype(F32), tm=256, tb=256)
        hs, st_s = _sample_layer(hs, cache_attn_k[l], cache_attn_v[l], state_conv[l], state_gdn[l], lw,
                                 rel_bias[l].astype(F32), tm=256)
        outs_p.append(st_p)
        outs_s.append(st_s)

    def stacked(outs, i):
        return jnp.stack([o[i] for o in outs])

    return (hp, hs,
            stacked(outs_p, 0), stacked(outs_p, 1), stacked(outs_p, 2), stacked(outs_p, 3),
            stacked(outs_s, 0), stacked(outs_s, 1), stacked(outs_s, 2), stacked(outs_s, 3))
```

```python
import functools
import math

import numpy as np
import jax
import jax.numpy as jnp
from jax import lax
from jax.experimental import pallas as pl
from jax.experimental.pallas import tpu as pltpu

D_MODEL = 1024
DEPTH = 2
PAST_LEN = 4096
CHUNK = 64
BAND_CHUNKS = 8
ATT_WINDOW = BAND_CHUNKS * CHUNK
A_HEADS = 8
A_HEAD_DIM = 64
A_WIDTH = A_HEADS * A_HEAD_DIM
REL_MAX = 128
B_HEADS = 8
B_HEAD_DIM = 64
B_WIDTH = B_HEADS * B_HEAD_DIM
B_QKV = 3 * B_WIDTH
CONV_W = 4
GDN_CHUNK = 64
ALPHA = (2 * DEPTH) ** 0.25
LN_EPS = 1e-5
NORM_EPS = 1e-6

LANES = 128
SUBLANES = 8
HEAD_PAIRS = A_HEADS // 2
CONV_HIST = SUBLANES
NEG = -1e30
LOG2E = math.log2(math.e)
V7X_VMEM_LIMIT = 56 * 1024 * 1024

BF16 = jnp.bfloat16
F32 = jnp.float32


def _mm(a, b):
    return jnp.dot(a.astype(BF16), b.astype(BF16), preferred_element_type=F32)


def _mm_nt(a, b):
    return lax.dot_general(a.astype(BF16), b.astype(BF16), (((1,), (1,)), ((), ())),
                           preferred_element_type=F32)


def _mm_tn(a, b):
    return lax.dot_general(a.astype(BF16), b.astype(BF16), (((0,), (0,)), ((), ())),
                           preferred_element_type=F32)


def _split3(x):
    x1 = x.astype(BF16)
    r1 = x - x1.astype(F32)
    x2 = r1.astype(BF16)
    x3 = (r1 - x2.astype(F32)).astype(BF16)
    return x1, x2, x3


def _mm_exact01(w01, x):
    w = w01.astype(BF16)
    return sum(jnp.dot(w, p, preferred_element_type=F32) for p in _split3(x))


def _mm_exact01_right(x, w01):
    w = w01.astype(BF16)
    return sum(jnp.dot(p, w, preferred_element_type=F32) for p in _split3(x))


def _silu(x):
    return x * jax.nn.sigmoid(x)


def _first_head_lanes():
    return lax.broadcasted_iota(jnp.int32, (1, LANES), 1) < B_HEAD_DIM


def _params(sem, est_bytes):
    limit = int(min(V7X_VMEM_LIMIT, max(32 * 1024 * 1024, est_bytes)))
    return pltpu.CompilerParams(dimension_semantics=sem, vmem_limit_bytes=limit)


def _const_spec(shape):
    nd = len(shape)
    return pl.BlockSpec(shape, lambda *_: (0,) * nd, pipeline_mode=pl.Buffered(1))


def _ln_kernel(x_ref, g_ref, b_ref, o_ref):
    x = x_ref[...]
    mu = jnp.mean(x, axis=-1, keepdims=True)
    xc = x - mu
    var = jnp.mean(xc * xc, axis=-1, keepdims=True)
    o_ref[...] = xc * lax.rsqrt(var + LN_EPS) * g_ref[...] + b_ref[...]


def _layer_norm(x2d, g, b, tm=512):
    n, d = x2d.shape
    return pl.pallas_call(
        _ln_kernel,
        grid=(n // tm,),
        in_specs=[pl.BlockSpec((tm, d), lambda i: (i, 0)),
                  _const_spec((1, d)), _const_spec((1, d))],
        out_specs=pl.BlockSpec((tm, d), lambda i: (i, 0)),
        out_shape=jax.ShapeDtypeStruct((n, d), F32),
        compiler_params=_params(("parallel",), 4 * tm * d * 4 * 2),
        name="layer_norm",
    )(x2d, g.reshape(1, d), b.reshape(1, d))


CONV_COLS = 2 * LANES


def _l2norm_pair(xs, scale):
    first = _first_head_lanes()
    x2 = xs * xs
    zero = jnp.zeros_like(x2)
    s0 = jnp.sum(jnp.where(first, x2, zero), axis=-1, keepdims=True)
    s1 = jnp.sum(jnp.where(first, zero, x2), axis=-1, keepdims=True)
    inv = jnp.where(first, lax.rsqrt(s0 + NORM_EPS), lax.rsqrt(s1 + NORM_EPS))
    return xs * (inv * scale)


def _conv_qkv_cols(xbuf, convw_ref, rows, c0):
    cols = slice(c0, c0 + CONV_COLS)
    first_tap = CONV_HIST - (CONV_W - 1)
    acc = xbuf[first_tap:first_tap + rows, cols] * convw_ref[0:1, cols]
    for i in range(1, CONV_W):
        acc = acc + xbuf[first_tap + i:first_tap + i + rows, cols] * convw_ref[i:i + 1, cols]
    tail = xbuf[rows:rows + CONV_HIST, cols]
    xbuf[0:CONV_HIST, cols] = tail
    c = _silu(acc)
    if c0 >= 2 * B_WIDTH:
        return c
    scale = B_HEAD_DIM ** -0.5 if c0 < B_WIDTH else 1.0
    return jnp.concatenate([_l2norm_pair(c[:, j * LANES:(j + 1) * LANES], scale)
                            for j in range(CONV_COLS // LANES)], axis=1)


def _inproj_kernel(x_ref, wa_ref, wb_ref, wz_ref, wab_ref, wg_ref, convw_ref,
                   q_ref, k_ref, v_ref, za_ref, xb_ref, zb_ref, ab_ref, ga_ref, gb_ref,
                   kt_ref, vt_ref, xtail_ref, xbuf, *, fuse_conv):
    tm = x_ref.shape[1]
    dot = functools.partial(jnp.dot, preferred_element_type=F32)
    if fuse_conv:
        @pl.when(pl.program_id(1) == 0)
        def _():
            xbuf[0:CONV_HIST, :] = jnp.zeros((CONV_HIST, B_QKV), F32)

    x = x_ref[0].astype(BF16)

    def attn_cols(j):
        r = dot(x, wa_ref[:, j * A_WIDTH:(j + 1) * A_WIDTH])
        if j == 0:
            q_ref[0] = r.astype(BF16)
        elif j == 3:
            za_ref[0] = r.astype(BF16)
        else:
            (k_ref, v_ref)[j - 1][0] = r.astype(BF16)
            (kt_ref, vt_ref)[j - 1][0] = r

    def gate_cols(j):
        (ga_ref, gb_ref)[j][0] = dot(x, wg_ref[:, j * D_MODEL:(j + 1) * D_MODEL]).astype(BF16)

    def zb_cols():
        zb_ref[0] = dot(x, wz_ref[...]).astype(BF16)
        ab_ref[0] = dot(x, wab_ref[...])

    others = [functools.partial(attn_cols, 0), functools.partial(attn_cols, 1), functools.partial(attn_cols, 2),
              functools.partial(attn_cols, 3), zb_cols, functools.partial(gate_cols, 0),
              functools.partial(gate_cols, 1)]
    for step in range(B_QKV // CONV_COLS):
        c0 = step * CONV_COLS
        xb = dot(x, wb_ref[:, c0:c0 + CONV_COLS])
        xtail_ref[0, :, c0:c0 + CONV_COLS] = xb[tm - CONV_HIST:tm, :]
        if fuse_conv:
            xbuf[CONV_HIST:CONV_HIST + tm, c0:c0 + CONV_COLS] = xb
        else:
            xb_ref[0, :, c0:c0 + CONV_COLS] = xb
        if others:
            others.pop(0)()
        if fuse_conv:
            xb_ref[0, :, c0:c0 + CONV_COLS] = _conv_qkv_cols(xbuf, convw_ref, tm, c0)
    for rest in others:
        rest()


def _in_projection(h, w, *, tm, tail, fuse_conv):
    bsz, t, d = h.shape
    nt = t // tm
    tail_blocks = tail // tm

    def row(width):
        return pl.BlockSpec((1, tm, width), lambda b, i: (b, i, 0))

    tail_spec = pl.BlockSpec((1, tm, A_WIDTH),
                             lambda b, i: (b, jnp.maximum(i - (nt - tail_blocks), 0), 0))
    xtail_spec = pl.BlockSpec((1, CONV_HIST, B_QKV), lambda b, i: (b, 0, 0))
    out_shapes = [
        jax.ShapeDtypeStruct((bsz, t, A_WIDTH), BF16),
        jax.ShapeDtypeStruct((bsz, t, A_WIDTH), BF16),
        jax.ShapeDtypeStruct((bsz, t, A_WIDTH), BF16),
        jax.ShapeDtypeStruct((bsz, t, A_WIDTH), BF16),
        jax.ShapeDtypeStruct((bsz, t, B_QKV), F32),
        jax.ShapeDtypeStruct((bsz, t, B_WIDTH), BF16),
        jax.ShapeDtypeStruct((bsz, t, LANES), F32),
        jax.ShapeDtypeStruct((bsz, t, D_MODEL), BF16),
        jax.ShapeDtypeStruct((bsz, t, D_MODEL), BF16),
        jax.ShapeDtypeStruct((bsz, tail, A_WIDTH), F32),
        jax.ShapeDtypeStruct((bsz, tail, A_WIDTH), F32),
        jax.ShapeDtypeStruct((bsz, CONV_HIST, B_QKV), F32),
    ]
    out_specs = [row(A_WIDTH), row(A_WIDTH), row(A_WIDTH), row(A_WIDTH), row(B_QKV), row(B_WIDTH),
                 row(LANES), row(D_MODEL), row(D_MODEL), tail_spec, tail_spec, xtail_spec]
    weights = (w["wa"], w["wb"], w["wz"], w["wab"], w["wg"], w["conv_w"])
    in_specs = [row(d)] + [_const_spec(x.shape) for x in weights]
    n_w = sum(int(np.prod(x.shape)) for x in weights)
    est = 2 * n_w + 2 * tm * (d * 4 + 3 * A_WIDTH * 2 + (2 * A_WIDTH + B_QKV + B_WIDTH + LANES
                                                        + 2 * D_MODEL + 2 * A_WIDTH) * 4)
    est += tm * (4 * A_WIDTH + 2 * D_MODEL + 4 * B_QKV) * 4 * 2
    return pl.pallas_call(
        functools.partial(_inproj_kernel, fuse_conv=fuse_conv),
        grid=(bsz, nt),
        in_specs=in_specs,
        out_specs=out_specs,
        out_shape=out_shapes,
        scratch_shapes=[pltpu.VMEM((CONV_HIST + tm, B_QKV), F32)],
        compiler_params=_params(("parallel", "arbitrary"), est + (8 << 20)),
        name="in_projection",
    )(h, *weights)


def _attention_heads(q, k, v, za, bias_ref, valid, o_ref):
    nq = q.shape[0]
    first = _first_head_lanes()
    c1 = (A_HEAD_DIM ** -0.5) * LOG2E
    def scores(p):
        qp, kp = q[:, p * LANES:(p + 1) * LANES], k[:, p * LANES:(p + 1) * LANES]
        zero = jnp.zeros_like(qp)
        q2 = jnp.concatenate([jnp.where(first, qp, zero), jnp.where(first, zero, qp)], axis=0)
        return _mm_nt(q2, kp)

    def softmax(p, s):
        s = s * c1 + bias_ref[p]
        if valid is not None:
            s = jnp.where(valid, s, NEG)
        e = jnp.exp2(s - jnp.max(s, axis=-1, keepdims=True))
        return e.astype(BF16), jnp.sum(e, axis=-1, keepdims=True)

    def output(p, e_den):
        sl = slice(p * LANES, (p + 1) * LANES)
        pv = _mm(e_den[0], v[:, sl]) / e_den[1]
        o = jnp.where(first, pv[0:nq], pv[nq:2 * nq])
        o_ref[:, sl] = (o * _silu(za[:, sl].astype(F32))).astype(o_ref.dtype)

    s_val, e_val = {}, {}
    for step in range(HEAD_PAIRS + 2):
        if step < HEAD_PAIRS:
            s_val[step] = scores(step)
        if 0 <= step - 1 < HEAD_PAIRS:
            e_val[step - 1] = softmax(step - 1, s_val.pop(step - 1))
        if 0 <= step - 2 < HEAD_PAIRS:
            output(step - 2, e_val.pop(step - 2))


def _attn_prompt_kernel(q_ref, *refs, qblk, n_kblk):
    k_refs, v_refs = refs[0:n_kblk], refs[n_kblk:2 * n_kblk]
    za_ref, bias_ref, o_ref = refs[2 * n_kblk:]
    m = pl.program_id(1)
    k = jnp.concatenate([r[0] for r in k_refs], axis=0)
    v = jnp.concatenate([r[0] for r in v_refs], axis=0)
    band = n_kblk * qblk
    n_lead = n_kblk - 1

    @pl.when(m >= n_lead)
    def _():
        _attention_heads(q_ref[0], k, v, za_ref[0], bias_ref, None, o_ref.at[0])

    @pl.when(m < n_lead)
    def _():
        col = lax.broadcasted_iota(jnp.int32, (1, band), 1) + (m - n_lead) * qblk
        _attention_heads(q_ref[0], k, v, za_ref[0], bias_ref, col >= 0, o_ref.at[0])


def _rel_bias_table(rel_bias, nq, nk, n_before):
    h = rel_bias.shape[0]
    n_var = nq + REL_MAX
    assert nk - n_var >= 0 and n_before + nq == nk
    period = n_var + nq
    far = rel_bias[:, REL_MAX + CHUNK - 1:]
    n_low = n_var - (REL_MAX + CHUNK)
    g = jnp.concatenate([rel_bias[:, ::-1], jnp.repeat(rel_bias[:, :1], n_low, axis=1),
                         jnp.repeat(far, period - n_var, axis=1)], axis=1)
    skew = jnp.tile(g, (1, nq))[:, :nq * (period - 1)].reshape(h, nq, period - 1)[:, :, :n_var]
    const = jnp.broadcast_to(far[:, :, None], (h, nq, nk - n_var))
    return jnp.concatenate([const, skew], axis=2)


def _pair_tables(tab):
    h, nq, nk = tab.shape
    return (tab * LOG2E).astype(F32).reshape(h // 2, 2 * nq, nk)


def _attention_prompt(q, k, v, za, rel_bias, qblk=2 * CHUNK):
    bsz, t, _ = q.shape
    n_kblk = ATT_WINDOW // qblk + 1
    band = n_kblk * qblk
    i = np.arange(qblk)[:, None]
    j = np.arange(band)[None, :]
    qc, kc = i // CHUNK, j // CHUNK
    visible = (kc >= qc) & (kc <= qc + BAND_CHUNKS)
    tab = _rel_bias_table(rel_bias, qblk, band, ATT_WINDOW)
    tab = _pair_tables(jnp.where(jnp.asarray(visible)[None], tab, NEG))
    row = pl.BlockSpec((1, qblk, A_WIDTH), lambda b, m: (b, m, 0))

    def key_block(jb):
        return pl.BlockSpec((1, qblk, A_WIDTH), lambda b, m: (b, jnp.maximum(m - (n_kblk - 1) + jb, 0), 0))

    kv_specs = [key_block(jb) for jb in range(n_kblk)]
    est = tab.size * 4 + 2 * (2 * n_kblk + 1) * qblk * A_WIDTH * 2 + 4 * qblk * A_WIDTH * 4 + 24 * qblk * band * 4
    return pl.pallas_call(
        functools.partial(_attn_prompt_kernel, qblk=qblk, n_kblk=n_kblk),
        grid=(bsz, t // qblk),
        in_specs=[row] + kv_specs + kv_specs + [row, _const_spec(tab.shape)],
        out_specs=row,
        out_shape=jax.ShapeDtypeStruct((bsz, t, A_WIDTH), BF16),
        compiler_params=_params(("parallel", "arbitrary"), est + (8 << 20)),
        name="attention_prompt",
    )(q, *([k] * n_kblk), *([v] * n_kblk), za, tab)


def _attn_sample_kernel(q_ref, kc_ref, vc_ref, kn_ref, vn_ref, za_ref, bias_ref, o_ref, *, nk):
    n_past, t = kc_ref.shape[1], kn_ref.shape[1]
    zeros = jnp.zeros((nk - n_past - t, A_WIDTH), BF16)
    k = jnp.concatenate([kc_ref[0].astype(BF16), kn_ref[0].astype(BF16), zeros], axis=0)
    v = jnp.concatenate([vc_ref[0].astype(BF16), vn_ref[0].astype(BF16), zeros], axis=0)
    _attention_heads(q_ref[0], k, v, za_ref[0], bias_ref, None, o_ref.at[0])


def _attention_sample(q, k_cache, v_cache, k_new, v_new, za, rel_bias):
    bsz, t, _ = q.shape
    n_past = k_cache.shape[1]
    nk = -(-(n_past + t) // LANES) * LANES
    nq_tab = nk - n_past
    tab = _rel_bias_table(rel_bias, nq_tab, nk, n_past)[:, :t]
    tab = _pair_tables(jnp.where(jnp.asarray(np.arange(nk) < n_past + t)[None, None], tab, NEG))

    def blk(rows):
        return pl.BlockSpec((1, rows, A_WIDTH), lambda b: (b, 0, 0))

    return pl.pallas_call(
        functools.partial(_attn_sample_kernel, nk=nk),
        grid=(bsz,),
        in_specs=[blk(t), blk(n_past), blk(n_past), blk(t), blk(t), blk(t), _const_spec(tab.shape)],
        out_specs=blk(t),
        out_shape=jax.ShapeDtypeStruct((bsz, t, A_WIDTH), BF16),
        compiler_params=_params(("parallel",), 32 << 20),
        name="attention_sample",
    )(q, k_cache, v_cache, k_new, v_new, za, tab)


_KB, _QN, _KN, _VB, _KBE, _QE, _KD = range(7)


def _gdn_kernel(xb_ref, ab_ref, zb_ref, conv0_ref, s0_ref, convw_ref, arow_ref, dtrow_ref, normw_ref,
                o_ref, s_out_ref, xbuf, s_scr, nat_scr, gcx_scr, sdec_scr, aq_scr, bo_scr,
                *, tb, n_valid, do_conv, n_par):
    t_idx = pl.program_id(1)
    blk = GDN_CHUNK
    two = 2 * blk
    n_chunks = tb // blk

    @pl.when(t_idx == 0)
    def _():
        s_scr[...] = s0_ref[0]
        if do_conv:
            xbuf[0:CONV_HIST, :] = conv0_ref[0]

    def padded(x):
        if n_valid == tb:
            return x
        return jnp.concatenate([x, jnp.zeros((tb - n_valid, x.shape[1]), x.dtype)], axis=0)

    if do_conv:
        xbuf[CONV_HIST:CONV_HIST + tb, :] = padded(xb_ref[0])
        c = jnp.concatenate([_conv_qkv_cols(xbuf, convw_ref, tb, c0) for c0 in range(0, B_QKV, CONV_COLS)], axis=1)
    else:
        c = xb_ref[0]
    qn, kn, v = c[:, 0:B_WIDTH], c[:, B_WIDTH:2 * B_WIDTH], c[:, 2 * B_WIDTH:]

    ab = padded(ab_ref[0])
    g = arow_ref[...] * jax.nn.softplus(ab + dtrow_ref[...])
    beta = jax.nn.sigmoid(ab)
    if n_valid != tb:
        is_token = lax.broadcasted_iota(jnp.int32, (tb, 1), 0) < n_valid
        g = jnp.where(is_token, g, 0.0)
        beta = jnp.where(is_token, beta, 0.0)
    rt = lax.broadcasted_iota(jnp.int32, (tb, tb), 0)
    ct = lax.broadcasted_iota(jnp.int32, (tb, tb), 1)
    gc = _mm_exact01(((rt // blk) == (ct // blk)) & (rt >= ct), g)
    src = lax.broadcasted_iota(jnp.int32, (LANES, B_WIDTH), 0)
    head = lax.broadcasted_iota(jnp.int32, (LANES, B_WIDTH), 1) // B_HEAD_DIM
    gcx = _mm_exact01_right(gc, src == head)
    betax = _mm_exact01_right(beta, src == head + B_HEADS)
    glx = jnp.concatenate([jnp.broadcast_to(gcx[(i + 1) * blk - 1:(i + 1) * blk, :], (blk, B_WIDTH))
                           for i in range(n_chunks)], axis=0)
    egc = jnp.exp(gcx)
    kbeta = kn * betax
    nat_scr[_KB] = kbeta.astype(BF16)
    nat_scr[_QN] = qn.astype(BF16)
    nat_scr[_KN] = kn.astype(BF16)
    nat_scr[_VB] = (v * betax).astype(BF16)
    nat_scr[_KBE] = (kbeta * egc).astype(BF16)
    nat_scr[_QE] = (qn * egc).astype(BF16)
    nat_scr[_KD] = (kn * jnp.exp(glx - gcx)).astype(BF16)
    gcx_scr[...] = gcx
    for i in range(n_chunks):
        sdec_scr[i] = jnp.broadcast_to(jnp.exp(gcx[(i + 1) * blk - 1:(i + 1) * blk, :]), (SUBLANES, B_WIDTH))

    first = _first_head_lanes()
    r = lax.broadcasted_iota(jnp.int32, (two, two), 0)
    cc = lax.broadcasted_iota(jnp.int32, (two, two), 1)
    causal = ((r // blk) == (cc // blk)) & (r >= cc)
    eye = (r == cc).astype(F32)
    off_diag = 1.0 - eye
    diag = (lax.broadcasted_iota(jnp.int32, (blk, LANES), 0)
            == lax.broadcasted_iota(jnp.int32, (blk, LANES), 1) % B_HEAD_DIM)

    def level_mask(bs):
        return ((r // bs) == (cc // bs)) & ((r % bs) >= bs // 2) & ((cc % bs) < bs // 2)

    def stacked(which, r0, p):
        x = nat_scr[which, pl.ds(r0, blk), p * LANES:(p + 1) * LANES]
        zero = jnp.zeros_like(x)
        return jnp.concatenate([jnp.where(first, x, zero), jnp.where(first, zero, x)], axis=0)

    def phase1(step, carry):
        chains = []
        for dc in range(n_par):
            cidx = step * n_par + dc
            r0 = pl.multiple_of(cidx * blk, blk)
            for p in range(HEAD_PAIRS):
                chains.append({"cidx": cidx, "r0": r0, "p": p})
        for ch in chains:
            r0, p = ch["r0"], ch["p"]
            gt = gcx_scr[pl.ds(r0, blk), p * LANES:(p + 1) * LANES]
            gc_row = jnp.sum(jnp.where(diag, gt, 0.0), axis=0, keepdims=True)
            ch["decay"] = jnp.exp(jnp.where(causal, jnp.concatenate([gt, gt], axis=0) - gc_row, NEG))
            lhs = jnp.concatenate([stacked(_KB, r0, p), stacked(_QN, r0, p)], axis=0)
            gram = _mm_nt(lhs, stacked(_KN, r0, p))
            ch["m"] = gram[0:two] * ch["decay"] * off_diag
            ch["qk"] = (gram[two:2 * two] * ch["decay"]).astype(BF16)
            ch["x"] = eye - jnp.where(level_mask(2), ch["m"], 0.0)
        bs = 4
        while bs <= blk:
            lm = level_mask(bs)
            for ch in chains:
                ch["t"] = _mm(jnp.where(lm, ch["m"], 0.0), ch["x"])
            for ch in chains:
                ch["x"] = ch["x"] - _mm(ch["x"], ch["t"])
            bs *= 2
        for ch in chains:
            r0, p = ch["r0"], ch["p"]
            rhs = jnp.concatenate([stacked(_VB, r0, p), stacked(_KBE, r0, p)], axis=1)
            ch["uw"] = _mm(ch["x"], rhs).astype(BF16)
        for ch in chains:
            r0, p = ch["r0"], ch["p"]
            kd_uw = _mm_tn(stacked(_KD, r0, p), ch["uw"])
            qk_uw = _mm(ch["qk"], ch["uw"])
            qeff = stacked(_QE, r0, p).astype(F32) - qk_uw[:, LANES:2 * LANES]
            aq_scr[ch["cidx"], p] = jnp.concatenate([-kd_uw[:, LANES:2 * LANES], qeff], axis=0).astype(BF16)
            bo_scr[ch["cidx"], p] = jnp.concatenate([kd_uw[:, 0:LANES], qk_uw[:, 0:LANES]], axis=0)
        return carry

    lax.fori_loop(0, n_chunks // n_par, phase1, 0)

    rows_out = min(blk, n_valid)

    def phase2(cidx, carry):
        r0 = pl.multiple_of(cidx * blk, blk)
        for p in range(HEAD_PAIRS):
            s_old = s_scr[p]
            tot = bo_scr[cidx, p] + _mm(aq_scr[cidx, p], s_old)
            dec = sdec_scr[cidx, 0:1, p * LANES:(p + 1) * LANES]
            s_scr[p] = s_old * dec + tot[0:two]
            o = tot[two:2 * two]
            ms = jnp.sum(o * o, axis=-1, keepdims=True) * (1.0 / B_HEAD_DIM)
            on = o * lax.rsqrt(ms + NORM_EPS) * normw_ref[...]
            o_pair = (on[0:blk] + on[blk:two])[0:rows_out]
            zb = zb_ref[0, pl.ds(r0, rows_out), p * LANES:(p + 1) * LANES].astype(F32)
            o_ref[0, pl.ds(r0, rows_out), p * LANES:(p + 1) * LANES] = (o_pair * _silu(zb)).astype(o_ref.dtype)
        return carry

    lax.fori_loop(0, n_chunks, phase2, 0)

    @pl.when(t_idx == pl.num_programs(1) - 1)
    def _():
        s_out_ref[0] = s_scr[...]


def _pair_state(s):
    bsz = s.shape[0]
    s = s.reshape(bsz, HEAD_PAIRS, 2, B_HEAD_DIM, B_HEAD_DIM)
    z = jnp.zeros_like(s[:, :, 0])
    top = jnp.concatenate([s[:, :, 0], z], axis=-1)
    bot = jnp.concatenate([z, s[:, :, 1]], axis=-1)
    return jnp.concatenate([top, bot], axis=-2)


def _unpair_state(sp):
    d = B_HEAD_DIM
    s = jnp.stack([sp[:, :, 0:d, 0:d], sp[:, :, d:2 * d, d:2 * d]], axis=2)
    return s.reshape(sp.shape[0], B_HEADS, d, d)


def _gated_delta(xb, ab, zb, conv_prev, s_prev, lw, *, tb, do_conv):
    bsz, t, _ = xb.shape
    n_valid = min(t, tb)
    assert t % n_valid == 0 and tb % GDN_CHUNK == 0 and (n_valid == tb or t == n_valid)
    conv0 = jnp.concatenate([jnp.zeros((bsz, CONV_HIST - (CONV_W - 1), B_QKV), F32), conv_prev.astype(F32)], axis=1)
    s0 = _pair_state(s_prev.astype(F32))

    def row(width):
        return pl.BlockSpec((1, n_valid, width), lambda b, i: (b, i, 0))

    def per_batch(shape):
        nd = len(shape)
        return pl.BlockSpec((1,) + shape, lambda b, i: (b,) + (0,) * nd)

    n_chunks = tb // GDN_CHUNK
    n_par = max(p for p in (1, 2, 4) if n_chunks % p == 0)
    per_chain = (n_chunks, HEAD_PAIRS, 2 * LANES, LANES)
    est = 2 * n_valid * (B_QKV + LANES + B_WIDTH) * 4 + (tb + CONV_HIST) * B_QKV * 4 + 6 * HEAD_PAIRS * LANES * LANES * 4
    est += tb * B_WIDTH * (7 * 2 + 4 + 16 * 4) + n_chunks * HEAD_PAIRS * LANES * LANES * 8 + tb * tb * 8
    o, s_out = pl.pallas_call(
        functools.partial(_gdn_kernel, tb=tb, n_valid=n_valid, do_conv=do_conv, n_par=n_par),
        grid=(bsz, t // n_valid),
        in_specs=[row(B_QKV), row(LANES), row(B_WIDTH), per_batch((CONV_HIST, B_QKV)),
                  per_batch((HEAD_PAIRS, LANES, LANES)),
                  _const_spec((CONV_W, B_QKV)), _const_spec((1, LANES)), _const_spec((1, LANES)),
                  _const_spec((1, LANES))],
        out_specs=[row(B_WIDTH), per_batch((HEAD_PAIRS, LANES, LANES))],
        out_shape=[jax.ShapeDtypeStruct((bsz, t, B_WIDTH), BF16),
                   jax.ShapeDtypeStruct((bsz, HEAD_PAIRS, LANES, LANES), F32)],
        scratch_shapes=[pltpu.VMEM((CONV_HIST + tb, B_QKV), F32),
                        pltpu.VMEM((HEAD_PAIRS, LANES, LANES), F32),
                        pltpu.VMEM((7, tb, B_WIDTH), BF16),
                        pltpu.VMEM((tb, B_WIDTH), F32),
                        pltpu.VMEM((n_chunks, SUBLANES, B_WIDTH), F32),
                        pltpu.VMEM(per_chain, BF16), pltpu.VMEM(per_chain, F32)],
        compiler_params=_params(("parallel", "arbitrary"), est + (16 << 20)),
        name="gated_delta",
    )(xb, ab, zb, conv0, s0, lw["conv_w"], lw["a_row"], lw["dt_row"], lw["normw_row"])
    return o, _unpair_state(s_out)


def _out_kernel(x_ref, a_ref, b_ref, ga_ref, gb_ref, wa_ref, wb_ref, wo_ref, g_ref, beta_ref, y_ref):
    dot = functools.partial(jnp.dot, preferred_element_type=F32)
    ya = dot(a_ref[...], wa_ref[...])
    yb = dot(b_ref[...], wb_ref[...])
    mix = jax.nn.sigmoid(ga_ref[...].astype(F32)) * ya + jax.nn.sigmoid(gb_ref[...].astype(F32)) * yb
    out = dot(mix.astype(BF16), wo_ref[...])
    z = ALPHA * x_ref[...] + out
    mu = jnp.mean(z, axis=-1, keepdims=True)
    zc = z - mu
    var = jnp.mean(zc * zc, axis=-1, keepdims=True)
    y_ref[...] = zc * lax.rsqrt(var + LN_EPS) * g_ref[...] + beta_ref[...]


def _out_projection(x, a, b, ga, gb, w, tm):
    n, d = x.shape

    def row(width):
        return pl.BlockSpec((tm, width), lambda i: (i, 0))

    weights = (w["w_branch_a"], w["w_branch_b"], w["w_out"], w["ln_g"], w["ln_b"])
    est = 2 * tm * (4 * d * 4 + 2 * A_WIDTH * 2) + 2 * (2 * A_WIDTH * d + d * d) + 6 * tm * d * 4
    return pl.pallas_call(
        _out_kernel,
        grid=(n // tm,),
        in_specs=[row(d), row(A_WIDTH), row(B_WIDTH), row(d), row(d)] + [_const_spec(x.shape) for x in weights],
        out_specs=row(d),
        out_shape=jax.ShapeDtypeStruct((n, d), F32),
        compiler_params=_params(("parallel",), est + (8 << 20)),
        name="out_projection",
    )(x, a, b, ga, gb, *weights)


def _layer_weights(l, w_in, conv_w, a_log, dt_bias, gdn_norm_w, w_branch_a, w_branch_b, w_out, ln_g, ln_b):
    wi = w_in[l]
    c0 = 4 * A_WIDTH
    c1 = c0 + B_QKV
    c2 = c1 + B_WIDTH
    c3 = c2 + 2 * B_HEADS
    wab = jnp.pad(wi[:, c2:c3], ((0, 0), (0, LANES - 2 * B_HEADS)))
    pad_h = (0, LANES - B_HEADS)
    return {
        "wa": wi[:, 0:c0].astype(BF16), "wb": wi[:, c0:c1].astype(BF16), "wz": wi[:, c1:c2].astype(BF16),
        "wab": wab.astype(BF16), "wg": wi[:, c3:].astype(BF16),
        "conv_w": conv_w[l].astype(F32),
        "a_row": jnp.pad(-jnp.exp(a_log[l].astype(F32)), pad_h).reshape(1, LANES),
        "dt_row": jnp.pad(dt_bias[l].astype(F32), pad_h).reshape(1, LANES),
        "normw_row": jnp.tile(gdn_norm_w[l].astype(F32), 2).reshape(1, LANES),
        "w_branch_a": w_branch_a[l].astype(BF16), "w_branch_b": w_branch_b[l].astype(BF16),
        "w_out": w_out[l].astype(BF16),
        "ln_g": ln_g[l].astype(F32).reshape(1, D_MODEL), "ln_b": ln_b[l].astype(F32).reshape(1, D_MODEL),
    }


def _prompt_layer(h, lw, rel_bias, *, tm, tb):
    bsz, t, d = h.shape
    keep = min(ATT_WINDOW, t)
    q, k, v, za, xb, zb, ab, ga, gb, k_tail, v_tail, x_tail = _in_projection(
        h, lw, tm=tm, tail=keep, fuse_conv=True)
    oa = _attention_prompt(q, k, v, za, rel_bias)
    conv_zero = jnp.zeros((bsz, CONV_W - 1, B_QKV), F32)
    s_zero = jnp.zeros((bsz, B_HEADS, B_HEAD_DIM, B_HEAD_DIM), F32)
    ob, s_new = _gated_delta(xb, ab, zb, conv_zero, s_zero, lw, tb=tb, do_conv=False)
    y = _out_projection(h.reshape(bsz * t, d), oa.reshape(bsz * t, A_WIDTH), ob.reshape(bsz * t, B_WIDTH),
                        ga.reshape(bsz * t, d), gb.reshape(bsz * t, d), lw, tm)
    new_k = k_tail.reshape(bsz, keep, A_HEADS, A_HEAD_DIM)
    new_v = v_tail.reshape(bsz, keep, A_HEADS, A_HEAD_DIM)
    return y.reshape(bsz, t, d), (new_k, new_v, x_tail[:, CONV_HIST - (CONV_W - 1):], s_new)


def _sample_layer(h, k_cache, v_cache, conv_prev, s_prev, lw, rel_bias, *, tm):
    bsz, t, d = h.shape
    n = bsz * t
    n_past = k_cache.shape[1]
    q, _, _, za, xb, zb, ab, ga, gb, k_new, v_new, _ = _in_projection(
        h.reshape(1, n, d), lw, tm=tm, tail=n, fuse_conv=False)

    def per_seq(x):
        return x.reshape(bsz, t, x.shape[-1])

    oa = _attention_sample(per_seq(q), k_cache.reshape(bsz, n_past, A_WIDTH), v_cache.reshape(bsz, n_past, A_WIDTH),
                           per_seq(k_new), per_seq(v_new), per_seq(za), rel_bias)
    xb_s = per_seq(xb)
    xp_tail = jnp.concatenate([conv_prev.astype(F32), xb_s], axis=1)[:, -(CONV_W - 1):]
    ob, s_new = _gated_delta(xb_s, per_seq(ab), per_seq(zb), conv_prev, s_prev, lw, tb=GDN_CHUNK, do_conv=True)
    y = _out_projection(h.reshape(n, d), oa.reshape(n, A_WIDTH), ob.reshape(n, B_WIDTH),
                        ga.reshape(n, d), gb.reshape(n, d), lw, tm)
    new_k = per_seq(k_new).reshape(bsz, t, A_HEADS, A_HEAD_DIM)
    new_v = per_seq(v_new).reshape(bsz, t, A_HEADS, A_HEAD_DIM)
    return y.reshape(bsz, t, d), (new_k, new_v, xp_tail, s_new)


def kernel(x_prompt, x_sample, cache_attn_k, cache_attn_v, state_conv, state_gdn, ln0_g, ln0_b, w_in, rel_bias,
           conv_w, a_log, dt_bias, gdn_norm_w, w_branch_a, w_branch_b, w_out, ln_g, ln_b):
    bp, tp, d = x_prompt.shape
    bs, ts, _ = x_sample.shape
    hp = _layer_norm(x_prompt.reshape(bp * tp, d), ln0_g, ln0_b).reshape(bp, tp, d)
    hs = _layer_norm(x_sample.reshape(bs * ts, d), ln0_g, ln0_b).reshape(bs, ts, d)
    outs_p, outs_s = [], []
    for l in range(DEPTH):
        lw = _layer_weights(l, w_in, conv_w, a_log, dt_bias, gdn_norm_w, w_branch_a, w_branch_b, w_out, ln_g, ln_b)
        hp, st_p = _prompt_layer(hp, lw, rel_bias[l].astype(F32), tm=256, tb=256)
        hs, st_s = _sample_layer(hs, cache_attn_k[l], cache_attn_v[l], state_conv[l], state_gdn[l], lw,
                                 rel_bias[l].astype(F32), tm=256)
        outs_p.append(st_p)
        outs_s.append(st_s)

    def stacked(outs, i):
        return jnp.stack([o[i] for o in outs])

    return (hp, hs,
            stacked(outs_p, 0), stacked(outs_p, 1), stacked(outs_p, 2), stacked(outs_p, 3),
            stacked(outs_s, 0), stacked(outs_s, 1), stacked(outs_s, 2), stacked(outs_s, 3))
```

```python
import functools
import math

import numpy as np
import jax
import jax.numpy as jnp
from jax import lax
from jax.experimental import pallas as pl
from jax.experimental.pallas import tpu as pltpu

D_MODEL = 1024
DEPTH = 2
PAST_LEN = 4096
CHUNK = 64
BAND_CHUNKS = 8
ATT_WINDOW = BAND_CHUNKS * CHUNK
A_HEADS = 8
A_HEAD_DIM = 64
A_WIDTH = A_HEADS * A_HEAD_DIM
REL_MAX = 128
B_HEADS = 8
B_HEAD_DIM = 64
B_WIDTH = B_HEADS * B_HEAD_DIM
B_QKV = 3 * B_WIDTH
CONV_W = 4
GDN_CHUNK = 64
ALPHA = (2 * DEPTH) ** 0.25
LN_EPS = 1e-5
NORM_EPS = 1e-6

LANES = 128
SUBLANES = 8
HEAD_PAIRS = A_HEADS // 2
CONV_HIST = SUBLANES
NEG = -1e30
LOG2E = math.log2(math.e)
V7X_VMEM_LIMIT = 56 * 1024 * 1024

BF16 = jnp.bfloat16
F32 = jnp.float32


def _mm(a, b):
    return jnp.dot(a.astype(BF16), b.astype(BF16), preferred_element_type=F32)


def _mm_nt(a, b):
    return lax.dot_general(a.astype(BF16), b.astype(BF16), (((1,), (1,)), ((), ())),
                           preferred_element_type=F32)


def _mm_tn(a, b):
    return lax.dot_general(a.astype(BF16), b.astype(BF16), (((0,), (0,)), ((), ())),
                           preferred_element_type=F32)


def _split3(x):
    x1 = x.astype(BF16)
    r1 = x - x1.astype(F32)
    x2 = r1.astype(BF16)
    x3 = (r1 - x2.astype(F32)).astype(BF16)
    return x1, x2, x3


def _mm_exact01(w01, x):
    w = w01.astype(BF16)
    return sum(jnp.dot(w, p, preferred_element_type=F32) for p in _split3(x))


def _mm_exact01_right(x, w01):
    w = w01.astype(BF16)
    return sum(jnp.dot(p, w, preferred_element_type=F32) for p in _split3(x))


def _silu(x):
    return x * jax.nn.sigmoid(x)


def _first_head_lanes():
    return lax.broadcasted_iota(jnp.int32, (1, LANES), 1) < B_HEAD_DIM


def _params(sem, est_bytes):
    limit = int(min(V7X_VMEM_LIMIT, max(32 * 1024 * 1024, est_bytes)))
    return pltpu.CompilerParams(dimension_semantics=sem, vmem_limit_bytes=limit)


def _const_spec(shape):
    nd = len(shape)
    return pl.BlockSpec(shape, lambda *_: (0,) * nd, pipeline_mode=pl.Buffered(1))


def _ln_rows(x, g, b):
    mu = jnp.mean(x, axis=-1, keepdims=True)
    xc = x - mu
    var = jnp.mean(xc * xc, axis=-1, keepdims=True)
    return xc * lax.rsqrt(var + LN_EPS) * g + b


CONV_COLS = 2 * LANES


def _l2norm_pair(xs, scale):
    first = _first_head_lanes()
    x2 = xs * xs
    zero = jnp.zeros_like(x2)
    s0 = jnp.sum(jnp.where(first, x2, zero), axis=-1, keepdims=True)
    s1 = jnp.sum(jnp.where(first, zero, x2), axis=-1, keepdims=True)
    inv = jnp.where(first, lax.rsqrt(s0 + NORM_EPS), lax.rsqrt(s1 + NORM_EPS))
    return xs * (inv * scale)


def _conv_qkv_cols(xbuf, convw_ref, rows, c0):
    cols = slice(c0, c0 + CONV_COLS)
    first_tap = CONV_HIST - (CONV_W - 1)
    acc = xbuf[first_tap:first_tap + rows, cols] * convw_ref[0:1, cols]
    for i in range(1, CONV_W):
        acc = acc + xbuf[first_tap + i:first_tap + i + rows, cols] * convw_ref[i:i + 1, cols]
    tail = xbuf[rows:rows + CONV_HIST, cols]
    xbuf[0:CONV_HIST, cols] = tail
    c = _silu(acc)
    if c0 >= 2 * B_WIDTH:
        return c
    scale = B_HEAD_DIM ** -0.5 if c0 < B_WIDTH else 1.0
    return jnp.concatenate([_l2norm_pair(c[:, j * LANES:(j + 1) * LANES], scale)
                            for j in range(CONV_COLS // LANES)], axis=1)


def _inproj_kernel(x_ref, wa_ref, wb_ref, wz_ref, wab_ref, wg_ref, convw_ref, lng_ref, lnb_ref,
                   q_ref, k_ref, v_ref, za_ref, xb_ref, zb_ref, ab_ref, ga_ref, gb_ref,
                   kt_ref, vt_ref, xtail_ref, xbuf, *, fuse_conv, pre_ln):
    tm = x_ref.shape[1]
    dot = functools.partial(jnp.dot, preferred_element_type=F32)
    if fuse_conv:
        @pl.when(pl.program_id(1) == 0)
        def _():
            xbuf[0:CONV_HIST, :] = jnp.zeros((CONV_HIST, B_QKV), F32)

    x = x_ref[0]
    if pre_ln:
        x = _ln_rows(x, lng_ref[...], lnb_ref[...])
    x = x.astype(BF16)

    def attn_cols(j):
        r = dot(x, wa_ref[:, j * A_WIDTH:(j + 1) * A_WIDTH])
        if j == 0:
            q_ref[0] = r.astype(BF16)
        elif j == 3:
            za_ref[0] = r.astype(BF16)
        else:
            (k_ref, v_ref)[j - 1][0] = r.astype(BF16)
            (kt_ref, vt_ref)[j - 1][0] = r

    def gate_cols(j):
        (ga_ref, gb_ref)[j][0] = dot(x, wg_ref[:, j * D_MODEL:(j + 1) * D_MODEL]).astype(BF16)

    def zb_cols():
        zb_ref[0] = dot(x, wz_ref[...]).astype(BF16)
        ab_ref[0] = dot(x, wab_ref[...])

    others = [functools.partial(attn_cols, 0), functools.partial(attn_cols, 1), functools.partial(attn_cols, 2),
              functools.partial(attn_cols, 3), zb_cols, functools.partial(gate_cols, 0),
              functools.partial(gate_cols, 1)]
    for step in range(B_QKV // CONV_COLS):
        c0 = step * CONV_COLS
        xb = dot(x, wb_ref[:, c0:c0 + CONV_COLS])
        xtail_ref[0, :, c0:c0 + CONV_COLS] = xb[tm - CONV_HIST:tm, :]
        if fuse_conv:
            xbuf[CONV_HIST:CONV_HIST + tm, c0:c0 + CONV_COLS] = xb
        else:
            xb_ref[0, :, c0:c0 + CONV_COLS] = xb
        if others:
            others.pop(0)()
        if fuse_conv:
            xb_ref[0, :, c0:c0 + CONV_COLS] = _conv_qkv_cols(xbuf, convw_ref, tm, c0)
    for rest in others:
        rest()


def _in_projection(h, w, *, tm, tail, fuse_conv, pre_ln):
    bsz, t, d = h.shape
    nt = t // tm
    tail_blocks = tail // tm

    def row(width):
        return pl.BlockSpec((1, tm, width), lambda b, i: (b, i, 0))

    tail_spec = pl.BlockSpec((1, tm, A_WIDTH),
                             lambda b, i: (b, jnp.maximum(i - (nt - tail_blocks), 0), 0))
    xtail_spec = pl.BlockSpec((1, CONV_HIST, B_QKV), lambda b, i: (b, 0, 0))
    out_shapes = [
        jax.ShapeDtypeStruct((bsz, t, A_WIDTH), BF16),
        jax.ShapeDtypeStruct((bsz, t, A_WIDTH), BF16),
        jax.ShapeDtypeStruct((bsz, t, A_WIDTH), BF16),
        jax.ShapeDtypeStruct((bsz, t, A_WIDTH), BF16),
        jax.ShapeDtypeStruct((bsz, t, B_QKV), F32),
        jax.ShapeDtypeStruct((bsz, t, B_WIDTH), BF16),
        jax.ShapeDtypeStruct((bsz, t, LANES), F32),
        jax.ShapeDtypeStruct((bsz, t, D_MODEL), BF16),
        jax.ShapeDtypeStruct((bsz, t, D_MODEL), BF16),
        jax.ShapeDtypeStruct((bsz, tail, A_WIDTH), F32),
        jax.ShapeDtypeStruct((bsz, tail, A_WIDTH), F32),
        jax.ShapeDtypeStruct((bsz, CONV_HIST, B_QKV), F32),
    ]
    out_specs = [row(A_WIDTH), row(A_WIDTH), row(A_WIDTH), row(A_WIDTH), row(B_QKV), row(B_WIDTH),
                 row(LANES), row(D_MODEL), row(D_MODEL), tail_spec, tail_spec, xtail_spec]
    weights = (w["wa"], w["wb"], w["wz"], w["wab"], w["wg"], w["conv_w"], w["ln0_g"], w["ln0_b"])
    in_specs = [row(d)] + [_const_spec(x.shape) for x in weights]
    n_w = sum(int(np.prod(x.shape)) for x in weights)
    est = 2 * n_w + 2 * tm * (d * 4 + 3 * A_WIDTH * 2 + (2 * A_WIDTH + B_QKV + B_WIDTH + LANES
                                                        + 2 * D_MODEL + 2 * A_WIDTH) * 4)
    est += tm * (4 * A_WIDTH + 2 * D_MODEL + 4 * B_QKV) * 4 * 2
    return pl.pallas_call(
        functools.partial(_inproj_kernel, fuse_conv=fuse_conv, pre_ln=pre_ln),
        grid=(bsz, nt),
        in_specs=in_specs,
        out_specs=out_specs,
        out_shape=out_shapes,
        scratch_shapes=[pltpu.VMEM((CONV_HIST + tm, B_QKV), F32)],
        compiler_params=_params(("parallel", "arbitrary"), est + (8 << 20)),
        name="in_projection",
    )(h, *weights)


def _attention_heads(jobs, bias_ref):
    first = _first_head_lanes()
    c1 = (A_HEAD_DIM ** -0.5) * LOG2E

    def scores(job, p):
        q, k = job[0], job[1]
        qp, kp = q[:, p * LANES:(p + 1) * LANES], k[:, p * LANES:(p + 1) * LANES]
        zero = jnp.zeros_like(qp)
        q2 = jnp.concatenate([jnp.where(first, qp, zero), jnp.where(first, zero, qp)], axis=0)
        return _mm_nt(q2, kp)

    def softmax(job, p, s):
        s = s * c1 + bias_ref[p]
        if job[4] is not None:
            s = jnp.where(job[4], s, NEG)
        e = jnp.exp2(s - jnp.max(s, axis=-1, keepdims=True))
        return e.astype(BF16), jnp.sum(e, axis=-1, keepdims=True)

    def output(job, p, e_den):
        _, _, v, za, _, o_ref = job
        nq = za.shape[0]
        sl = slice(p * LANES, (p + 1) * LANES)
        pv = _mm(e_den[0], v[:, sl]) / e_den[1]
        o = jnp.where(first, pv[0:nq], pv[nq:2 * nq])
        o_ref[:, sl] = (o * _silu(za[:, sl].astype(F32))).astype(o_ref.dtype)

    chains = [(job, p) for job in jobs for p in range(HEAD_PAIRS)]
    s_val, e_val = {}, {}
    for step in range(len(chains) + 2):
        if step < len(chains):
            s_val[step] = scores(*chains[step])
        if 0 <= step - 1 < len(chains):
            e_val[step - 1] = softmax(*chains[step - 1], s_val.pop(step - 1))
        if 0 <= step - 2 < len(chains):
            output(*chains[step - 2], e_val.pop(step - 2))


def _attn_prompt_kernel(q_ref, *refs, qblk, n_sub, n_kblk):
    k_refs, v_refs = refs[0:n_kblk], refs[n_kblk:2 * n_kblk]
    za_ref, bias_ref, o_ref = refs[2 * n_kblk:]
    m = pl.program_id(1)
    n_lead = n_kblk - n_sub
    band = (n_lead + 1) * qblk

    def jobs(masked):
        out = []
        for u in range(n_sub):
            k = jnp.concatenate([r[0] for r in k_refs[u:u + n_lead + 1]], axis=0)
            v = jnp.concatenate([r[0] for r in v_refs[u:u + n_lead + 1]], axis=0)
            valid = None
            if masked:
                col = lax.broadcasted_iota(jnp.int32, (1, band), 1) + (m * n_sub + u - n_lead) * qblk
                valid = col >= 0
            rows = slice(u * qblk, (u + 1) * qblk)
            out.append((q_ref[0, rows, :], k, v, za_ref[0, rows, :], valid, o_ref.at[0, rows, :]))
        return out

    @pl.when(m * n_sub >= n_lead)
    def _():
        _attention_heads(jobs(False), bias_ref)

    @pl.when(m * n_sub < n_lead)
    def _():
        _attention_heads(jobs(True), bias_ref)


def _rel_bias_table(rel_bias, nq, nk, n_before):
    h = rel_bias.shape[0]
    n_var = nq + REL_MAX
    assert nk - n_var >= 0 and n_before + nq == nk
    period = n_var + nq
    far = rel_bias[:, REL_MAX + CHUNK - 1:]
    n_low = n_var - (REL_MAX + CHUNK)
    g = jnp.concatenate([rel_bias[:, ::-1], jnp.repeat(rel_bias[:, :1], n_low, axis=1),
                         jnp.repeat(far, period - n_var, axis=1)], axis=1)
    skew = jnp.tile(g, (1, nq))[:, :nq * (period - 1)].reshape(h, nq, period - 1)[:, :, :n_var]
    const = jnp.broadcast_to(far[:, :, None], (h, nq, nk - n_var))
    return jnp.concatenate([const, skew], axis=2)


def _pair_tables(tab):
    h, nq, nk = tab.shape
    return (tab * LOG2E).astype(F32).reshape(h // 2, 2 * nq, nk)


def _attention_prompt(q, k, v, za, rel_bias, qblk=2 * CHUNK, n_sub=2):
    bsz, t, _ = q.shape
    n_lead = ATT_WINDOW // qblk
    n_kblk = n_lead + n_sub
    band = (n_lead + 1) * qblk
    i = np.arange(qblk)[:, None]
    j = np.arange(band)[None, :]
    qc, kc = i // CHUNK, j // CHUNK
    visible = (kc >= qc) & (kc <= qc + BAND_CHUNKS)
    tab = _rel_bias_table(rel_bias, qblk, band, ATT_WINDOW)
    tab = _pair_tables(jnp.where(jnp.asarray(visible)[None], tab, NEG))
    row = pl.BlockSpec((1, n_sub * qblk, A_WIDTH), lambda b, m: (b, m, 0))

    def key_block(jb):
        return pl.BlockSpec((1, qblk, A_WIDTH), lambda b, m: (b, jnp.maximum(m * n_sub - n_lead + jb, 0), 0))

    kv_specs = [key_block(jb) for jb in range(n_kblk)]
    est = tab.size * 4 + 2 * (2 * n_kblk + 2 * n_sub) * qblk * A_WIDTH * 2 + n_sub * 32 * qblk * band * 4
    return pl.pallas_call(
        functools.partial(_attn_prompt_kernel, qblk=qblk, n_sub=n_sub, n_kblk=n_kblk),
        grid=(bsz, t // (n_sub * qblk)),
        in_specs=[row] + kv_specs + kv_specs + [row, _const_spec(tab.shape)],
        out_specs=row,
        out_shape=jax.ShapeDtypeStruct((bsz, t, A_WIDTH), BF16),
        compiler_params=_params(("parallel", "arbitrary"), est + (8 << 20)),
        name="attention_prompt",
    )(q, *([k] * n_kblk), *([v] * n_kblk), za, tab)


def _attn_sample_kernel(q_ref, kc_ref, vc_ref, kn_ref, vn_ref, za_ref, bias_ref, o_ref, *, nk):
    n_past, t = kc_ref.shape[1], kn_ref.shape[1]
    zeros = jnp.zeros((nk - n_past - t, A_WIDTH), BF16)
    k = jnp.concatenate([kc_ref[0].astype(BF16), kn_ref[0].astype(BF16), zeros], axis=0)
    v = jnp.concatenate([vc_ref[0].astype(BF16), vn_ref[0].astype(BF16), zeros], axis=0)
    _attention_heads([(q_ref[0], k, v, za_ref[0], None, o_ref.at[0])], bias_ref)


def _attention_sample(q, k_cache, v_cache, k_new, v_new, za, rel_bias):
    bsz, t, _ = q.shape
    n_past = k_cache.shape[1]
    nk = -(-(n_past + t) // LANES) * LANES
    nq_tab = nk - n_past
    tab = _rel_bias_table(rel_bias, nq_tab, nk, n_past)[:, :t]
    tab = _pair_tables(jnp.where(jnp.asarray(np.arange(nk) < n_past + t)[None, None], tab, NEG))

    def blk(rows):
        return pl.BlockSpec((1, rows, A_WIDTH), lambda b: (b, 0, 0))

    return pl.pallas_call(
        functools.partial(_attn_sample_kernel, nk=nk),
        grid=(bsz,),
        in_specs=[blk(t), blk(n_past), blk(n_past), blk(t), blk(t), blk(t), _const_spec(tab.shape)],
        out_specs=blk(t),
        out_shape=jax.ShapeDtypeStruct((bsz, t, A_WIDTH), BF16),
        compiler_params=_params(("parallel",), 32 << 20),
        name="attention_sample",
    )(q, k_cache, v_cache, k_new, v_new, za, tab)


_KB, _QN, _KN, _VB, _KBE, _QE, _KD = range(7)


def _gdn_kernel(xb_ref, ab_ref, zb_ref, conv0_ref, s0_ref, convw_ref, arow_ref, dtrow_ref, normw_ref,
                o_ref, s_out_ref, xbuf, s_scr, nat_scr, gcx_scr, sdec_scr, aq_scr, bo_scr,
                *, tb, n_valid, do_conv, n_par):
    t_idx = pl.program_id(1)
    blk = GDN_CHUNK
    two = 2 * blk
    n_chunks = tb // blk

    @pl.when(t_idx == 0)
    def _():
        s_scr[...] = s0_ref[0]
        if do_conv:
            xbuf[0:CONV_HIST, :] = conv0_ref[0]

    def padded(x):
        if n_valid == tb:
            return x
        return jnp.concatenate([x, jnp.zeros((tb - n_valid, x.shape[1]), x.dtype)], axis=0)

    if do_conv:
        xbuf[CONV_HIST:CONV_HIST + tb, :] = padded(xb_ref[0])
        c = jnp.concatenate([_conv_qkv_cols(xbuf, convw_ref, tb, c0) for c0 in range(0, B_QKV, CONV_COLS)], axis=1)
    else:
        c = xb_ref[0]
    qn, kn, v = c[:, 0:B_WIDTH], c[:, B_WIDTH:2 * B_WIDTH], c[:, 2 * B_WIDTH:]

    ab = padded(ab_ref[0])
    g = arow_ref[...] * jax.nn.softplus(ab + dtrow_ref[...])
    beta = jax.nn.sigmoid(ab)
    if n_valid != tb:
        is_token = lax.broadcasted_iota(jnp.int32, (tb, 1), 0) < n_valid
        g = jnp.where(is_token, g, 0.0)
        beta = jnp.where(is_token, beta, 0.0)
    rt = lax.broadcasted_iota(jnp.int32, (tb, tb), 0)
    ct = lax.broadcasted_iota(jnp.int32, (tb, tb), 1)
    gc = _mm_exact01(((rt // blk) == (ct // blk)) & (rt >= ct), g)
    src = lax.broadcasted_iota(jnp.int32, (LANES, B_WIDTH), 0)
    head = lax.broadcasted_iota(jnp.int32, (LANES, B_WIDTH), 1) // B_HEAD_DIM
    gcx = _mm_exact01_right(gc, src == head)
    betax = _mm_exact01_right(beta, src == head + B_HEADS)
    glx = jnp.concatenate([jnp.broadcast_to(gcx[(i + 1) * blk - 1:(i + 1) * blk, :], (blk, B_WIDTH))
                           for i in range(n_chunks)], axis=0)
    egc = jnp.exp(gcx)
    kbeta = kn * betax
    nat_scr[_KB] = kbeta.astype(BF16)
    nat_scr[_QN] = qn.astype(BF16)
    nat_scr[_KN] = kn.astype(BF16)
    nat_scr[_VB] = (v * betax).astype(BF16)
    nat_scr[_KBE] = (kbeta * egc).astype(BF16)
    nat_scr[_QE] = (qn * egc).astype(BF16)
    nat_scr[_KD] = (kn * jnp.exp(glx - gcx)).astype(BF16)
    gcx_scr[...] = gcx
    for i in range(n_chunks):
        sdec_scr[i] = jnp.broadcast_to(jnp.exp(gcx[(i + 1) * blk - 1:(i + 1) * blk, :]), (SUBLANES, B_WIDTH))

    first = _first_head_lanes()
    r = lax.broadcasted_iota(jnp.int32, (two, two), 0)
    cc = lax.broadcasted_iota(jnp.int32, (two, two), 1)
    causal = ((r // blk) == (cc // blk)) & (r >= cc)
    eye = (r == cc).astype(F32)
    off_diag = 1.0 - eye
    diag = (lax.broadcasted_iota(jnp.int32, (blk, LANES), 0)
            == lax.broadcasted_iota(jnp.int32, (blk, LANES), 1) % B_HEAD_DIM)

    def level_mask(bs):
        return ((r // bs) == (cc // bs)) & ((r % bs) >= bs // 2) & ((cc % bs) < bs // 2)

    def stacked(which, r0, p):
        x = nat_scr[which, pl.ds(r0, blk), p * LANES:(p + 1) * LANES]
        zero = jnp.zeros_like(x)
        return jnp.concatenate([jnp.where(first, x, zero), jnp.where(first, zero, x)], axis=0)

    def phase1(step, carry):
        chains = []
        for dc in range(n_par):
            cidx = step * n_par + dc
            r0 = pl.multiple_of(cidx * blk, blk)
            for p in range(HEAD_PAIRS):
                chains.append({"cidx": cidx, "r0": r0, "p": p})
        for ch in chains:
            r0, p = ch["r0"], ch["p"]
            gt = gcx_scr[pl.ds(r0, blk), p * LANES:(p + 1) * LANES]
            gc_row = jnp.sum(jnp.where(diag, gt, 0.0), axis=0, keepdims=True)
            ch["decay"] = jnp.exp(jnp.where(causal, jnp.concatenate([gt, gt], axis=0) - gc_row, NEG))
            lhs = jnp.concatenate([stacked(_KB, r0, p), stacked(_QN, r0, p)], axis=0)
            gram = _mm_nt(lhs, stacked(_KN, r0, p))
            ch["m"] = gram[0:two] * ch["decay"] * off_diag
            ch["qk"] = (gram[two:2 * two] * ch["decay"]).astype(BF16)
            ch["x"] = eye - jnp.where(level_mask(2), ch["m"], 0.0)
        bs = 4
        while bs <= blk:
            lm = level_mask(bs)
            for ch in chains:
                ch["t"] = _mm(jnp.where(lm, ch["m"], 0.0), ch["x"])
            for ch in chains:
                ch["x"] = ch["x"] - _mm(ch["x"], ch["t"])
            bs *= 2
        for ch in chains:
            r0, p = ch["r0"], ch["p"]
            rhs = jnp.concatenate([stacked(_VB, r0, p), stacked(_KBE, r0, p)], axis=1)
            ch["uw"] = _mm(ch["x"], rhs).astype(BF16)
        for ch in chains:
            r0, p = ch["r0"], ch["p"]
            kd_uw = _mm_tn(stacked(_KD, r0, p), ch["uw"])
            qk_uw = _mm(ch["qk"], ch["uw"])
            qeff = stacked(_QE, r0, p).astype(F32) - qk_uw[:, LANES:2 * LANES]
            aq_scr[ch["cidx"], p] = jnp.concatenate([-kd_uw[:, LANES:2 * LANES], qeff], axis=0).astype(BF16)
            bo_scr[ch["cidx"], p] = jnp.concatenate([kd_uw[:, 0:LANES], qk_uw[:, 0:LANES]], axis=0)
        return carry

    lax.fori_loop(0, n_chunks // n_par, phase1, 0)

    rows_out = min(blk, n_valid)

    def phase2(cidx, carry):
        r0 = pl.multiple_of(cidx * blk, blk)
        for p in range(HEAD_PAIRS):
            s_old = s_scr[p]
            tot = bo_scr[cidx, p] + _mm(aq_scr[cidx, p], s_old)
            dec = sdec_scr[cidx, 0:1, p * LANES:(p + 1) * LANES]
            s_scr[p] = s_old * dec + tot[0:two]
            o = tot[two:2 * two]
            ms = jnp.sum(o * o, axis=-1, keepdims=True) * (1.0 / B_HEAD_DIM)
            on = o * lax.rsqrt(ms + NORM_EPS) * normw_ref[...]
            o_pair = (on[0:blk] + on[blk:two])[0:rows_out]
            zb = zb_ref[0, pl.ds(r0, rows_out), p * LANES:(p + 1) * LANES].astype(F32)
            o_ref[0, pl.ds(r0, rows_out), p * LANES:(p + 1) * LANES] = (o_pair * _silu(zb)).astype(o_ref.dtype)
        return carry

    lax.fori_loop(0, n_chunks, phase2, 0)

    @pl.when(t_idx == pl.num_programs(1) - 1)
    def _():
        s_out_ref[0] = s_scr[...]


def _pair_state(s):
    bsz = s.shape[0]
    s = s.reshape(bsz, HEAD_PAIRS, 2, B_HEAD_DIM, B_HEAD_DIM)
    z = jnp.zeros_like(s[:, :, 0])
    top = jnp.concatenate([s[:, :, 0], z], axis=-1)
    bot = jnp.concatenate([z, s[:, :, 1]], axis=-1)
    return jnp.concatenate([top, bot], axis=-2)


def _unpair_state(sp):
    d = B_HEAD_DIM
    s = jnp.stack([sp[:, :, 0:d, 0:d], sp[:, :, d:2 * d, d:2 * d]], axis=2)
    return s.reshape(sp.shape[0], B_HEADS, d, d)


def _gated_delta(xb, ab, zb, conv_prev, s_prev, lw, *, tb, do_conv):
    bsz, t, _ = xb.shape
    n_valid = min(t, tb)
    assert t % n_valid == 0 and tb % GDN_CHUNK == 0 and (n_valid == tb or t == n_valid)
    conv0 = jnp.concatenate([jnp.zeros((bsz, CONV_HIST - (CONV_W - 1), B_QKV), F32), conv_prev.astype(F32)], axis=1)
    s0 = _pair_state(s_prev.astype(F32))

    def row(width):
        return pl.BlockSpec((1, n_valid, width), lambda b, i: (b, i, 0))

    def per_batch(shape):
        nd = len(shape)
        return pl.BlockSpec((1,) + shape, lambda b, i: (b,) + (0,) * nd)

    n_chunks = tb // GDN_CHUNK
    n_par = max(p for p in (1, 2, 4) if n_chunks % p == 0)
    per_chain = (n_chunks, HEAD_PAIRS, 2 * LANES, LANES)
    est = 2 * n_valid * (B_QKV + LANES + B_WIDTH) * 4 + (tb + CONV_HIST) * B_QKV * 4 + 6 * HEAD_PAIRS * LANES * LANES * 4
    est += tb * B_WIDTH * (7 * 2 + 4 + 16 * 4) + n_chunks * HEAD_PAIRS * LANES * LANES * 8 + tb * tb * 8
    o, s_out = pl.pallas_call(
        functools.partial(_gdn_kernel, tb=tb, n_valid=n_valid, do_conv=do_conv, n_par=n_par),
        grid=(bsz, t // n_valid),
        in_specs=[row(B_QKV), row(LANES), row(B_WIDTH), per_batch((CONV_HIST, B_QKV)),
                  per_batch((HEAD_PAIRS, LANES, LANES)),
                  _const_spec((CONV_W, B_QKV)), _const_spec((1, LANES)), _const_spec((1, LANES)),
                  _const_spec((1, LANES))],
        out_specs=[row(B_WIDTH), per_batch((HEAD_PAIRS, LANES, LANES))],
        out_shape=[jax.ShapeDtypeStruct((bsz, t, B_WIDTH), BF16),
                   jax.ShapeDtypeStruct((bsz, HEAD_PAIRS, LANES, LANES), F32)],
        scratch_shapes=[pltpu.VMEM((CONV_HIST + tb, B_QKV), F32),
                        pltpu.VMEM((HEAD_PAIRS, LANES, LANES), F32),
                        pltpu.VMEM((7, tb, B_WIDTH), BF16),
                        pltpu.VMEM((tb, B_WIDTH), F32),
                        pltpu.VMEM((n_chunks, SUBLANES, B_WIDTH), F32),
                        pltpu.VMEM(per_chain, BF16), pltpu.VMEM(per_chain, F32)],
        compiler_params=_params(("parallel", "arbitrary"), est + (16 << 20)),
        name="gated_delta",
    )(xb, ab, zb, conv0, s0, lw["conv_w"], lw["a_row"], lw["dt_row"], lw["normw_row"])
    return o, _unpair_state(s_out)


def _out_kernel(x_ref, a_ref, b_ref, ga_ref, gb_ref, wa_ref, wb_ref, wo_ref, g_ref, beta_ref, lng_ref, lnb_ref,
                y_ref, *, pre_ln):
    dot = functools.partial(jnp.dot, preferred_element_type=F32)
    x = x_ref[...]
    if pre_ln:
        x = _ln_rows(x, lng_ref[...], lnb_ref[...])
    ya = dot(a_ref[...], wa_ref[...])
    yb = dot(b_ref[...], wb_ref[...])
    mix = jax.nn.sigmoid(ga_ref[...].astype(F32)) * ya + jax.nn.sigmoid(gb_ref[...].astype(F32)) * yb
    out = dot(mix.astype(BF16), wo_ref[...])
    y_ref[...] = _ln_rows(ALPHA * x + out, g_ref[...], beta_ref[...])


def _out_projection(x, a, b, ga, gb, w, tm, pre_ln):
    n, d = x.shape

    def row(width):
        return pl.BlockSpec((tm, width), lambda i: (i, 0))

    weights = (w["w_branch_a"], w["w_branch_b"], w["w_out"], w["ln_g"], w["ln_b"], w["ln0_g"], w["ln0_b"])
    est = 2 * tm * (4 * d * 4 + 2 * A_WIDTH * 2) + 2 * (2 * A_WIDTH * d + d * d) + 6 * tm * d * 4
    return pl.pallas_call(
        functools.partial(_out_kernel, pre_ln=pre_ln),
        grid=(n // tm,),
        in_specs=[row(d), row(A_WIDTH), row(B_WIDTH), row(d), row(d)] + [_const_spec(x.shape) for x in weights],
        out_specs=row(d),
        out_shape=jax.ShapeDtypeStruct((n, d), F32),
        compiler_params=_params(("parallel",), est + (8 << 20)),
        name="out_projection",
    )(x, a, b, ga, gb, *weights)


def _layer_weights(l, w_in, conv_w, a_log, dt_bias, gdn_norm_w, w_branch_a, w_branch_b, w_out, ln_g, ln_b,
                   ln0_g, ln0_b):
    wi = w_in[l]
    c0 = 4 * A_WIDTH
    c1 = c0 + B_QKV
    c2 = c1 + B_WIDTH
    c3 = c2 + 2 * B_HEADS
    wab = jnp.pad(wi[:, c2:c3], ((0, 0), (0, LANES - 2 * B_HEADS)))
    pad_h = (0, LANES - B_HEADS)
    return {
        "wa": wi[:, 0:c0].astype(BF16), "wb": wi[:, c0:c1].astype(BF16), "wz": wi[:, c1:c2].astype(BF16),
        "wab": wab.astype(BF16), "wg": wi[:, c3:].astype(BF16),
        "conv_w": conv_w[l].astype(F32),
        "a_row": jnp.pad(-jnp.exp(a_log[l].astype(F32)), pad_h).reshape(1, LANES),
        "dt_row": jnp.pad(dt_bias[l].astype(F32), pad_h).reshape(1, LANES),
        "normw_row": jnp.tile(gdn_norm_w[l].astype(F32), 2).reshape(1, LANES),
        "w_branch_a": w_branch_a[l].astype(BF16), "w_branch_b": w_branch_b[l].astype(BF16),
        "w_out": w_out[l].astype(BF16),
        "ln_g": ln_g[l].astype(F32).reshape(1, D_MODEL), "ln_b": ln_b[l].astype(F32).reshape(1, D_MODEL),
        "ln0_g": ln0_g.astype(F32).reshape(1, D_MODEL), "ln0_b": ln0_b.astype(F32).reshape(1, D_MODEL),
    }


def _prompt_layer(h, lw, rel_bias, *, tm, tb, pre_ln):
    bsz, t, d = h.shape
    keep = min(ATT_WINDOW, t)
    q, k, v, za, xb, zb, ab, ga, gb, k_tail, v_tail, x_tail = _in_projection(
        h, lw, tm=tm, tail=keep, fuse_conv=True, pre_ln=pre_ln)
    oa = _attention_prompt(q, k, v, za, rel_bias)
    conv_zero = jnp.zeros((bsz, CONV_W - 1, B_QKV), F32)
    s_zero = jnp.zeros((bsz, B_HEADS, B_HEAD_DIM, B_HEAD_DIM), F32)
    ob, s_new = _gated_delta(xb, ab, zb, conv_zero, s_zero, lw, tb=tb, do_conv=False)
    y = _out_projection(h.reshape(bsz * t, d), oa.reshape(bsz * t, A_WIDTH), ob.reshape(bsz * t, B_WIDTH),
                        ga.reshape(bsz * t, d), gb.reshape(bsz * t, d), lw, tm, pre_ln)
    new_k = k_tail.reshape(bsz, keep, A_HEADS, A_HEAD_DIM)
    new_v = v_tail.reshape(bsz, keep, A_HEADS, A_HEAD_DIM)
    return y.reshape(bsz, t, d), (new_k, new_v, x_tail[:, CONV_HIST - (CONV_W - 1):], s_new)


def _sample_layer(h, k_cache, v_cache, conv_prev, s_prev, lw, rel_bias, *, tm, pre_ln):
    bsz, t, d = h.shape
    n = bsz * t
    n_past = k_cache.shape[1]
    q, _, _, za, xb, zb, ab, ga, gb, k_new, v_new, _ = _in_projection(
        h.reshape(1, n, d), lw, tm=tm, tail=n, fuse_conv=False, pre_ln=pre_ln)

    def per_seq(x):
        return x.reshape(bsz, t, x.shape[-1])

    oa = _attention_sample(per_seq(q), k_cache.reshape(bsz, n_past, A_WIDTH), v_cache.reshape(bsz, n_past, A_WIDTH),
                           per_seq(k_new), per_seq(v_new), per_seq(za), rel_bias)
    xb_s = per_seq(xb)
    xp_tail = jnp.concatenate([conv_prev.astype(F32), xb_s], axis=1)[:, -(CONV_W - 1):]
    ob, s_new = _gated_delta(xb_s, per_seq(ab), per_seq(zb), conv_prev, s_prev, lw, tb=GDN_CHUNK, do_conv=True)
    y = _out_projection(h.reshape(n, d), oa.reshape(n, A_WIDTH), ob.reshape(n, B_WIDTH),
                        ga.reshape(n, d), gb.reshape(n, d), lw, tm, pre_ln)
    new_k = per_seq(k_new).reshape(bsz, t, A_HEADS, A_HEAD_DIM)
    new_v = per_seq(v_new).reshape(bsz, t, A_HEADS, A_HEAD_DIM)
    return y.reshape(bsz, t, d), (new_k, new_v, xp_tail, s_new)


def kernel(x_prompt, x_sample, cache_attn_k, cache_attn_v, state_conv, state_gdn, ln0_g, ln0_b, w_in, rel_bias,
           conv_w, a_log, dt_bias, gdn_norm_w, w_branch_a, w_branch_b, w_out, ln_g, ln_b):
    bp, tp, d = x_prompt.shape
    bs, ts, _ = x_sample.shape
    hp, hs = x_prompt, x_sample
    outs_p, outs_s = [], []
    for l in range(DEPTH):
        lw = _layer_weights(l, w_in, conv_w, a_log, dt_bias, gdn_norm_w, w_branch_a, w_branch_b, w_out, ln_g, ln_b,
                            ln0_g, ln0_b)
        hp, st_p = _prompt_layer(hp, lw, rel_bias[l].astype(F32), tm=256, tb=256, pre_ln=(l == 0))
        hs, st_s = _sample_layer(hs, cache_attn_k[l], cache_attn_v[l], state_conv[l], state_gdn[l], lw,
                                 rel_bias[l].astype(F32), tm=256, pre_ln=(l == 0))
        outs_p.append(st_p)
        outs_s.append(st_s)

    def stacked(outs, i):
        return jnp.stack([o[i] for o in outs])

    return (hp, hs,
            stacked(outs_p, 0), stacked(outs_p, 1), stacked(outs_p, 2), stacked(outs_p, 3),
            stacked(outs_s, 0), stacked(outs_s, 1), stacked(outs_s, 2), stacked(outs_s, 3))
```

```python
import functools
import math

import numpy as np
import jax
import jax.numpy as jnp
from jax import lax
from jax.experimental import pallas as pl
from jax.experimental.pallas import tpu as pltpu

D_MODEL = 1024
DEPTH = 2
PAST_LEN = 4096
CHUNK = 64
BAND_CHUNKS = 8
ATT_WINDOW = BAND_CHUNKS * CHUNK
A_HEADS = 8
A_HEAD_DIM = 64
A_WIDTH = A_HEADS * A_HEAD_DIM
REL_MAX = 128
B_HEADS = 8
B_HEAD_DIM = 64
B_WIDTH = B_HEADS * B_HEAD_DIM
B_QKV = 3 * B_WIDTH
CONV_W = 4
GDN_CHUNK = 64
ALPHA = (2 * DEPTH) ** 0.25
LN_EPS = 1e-5
NORM_EPS = 1e-6

LANES = 128
SUBLANES = 8
HEAD_PAIRS = A_HEADS // 2
CONV_HIST = SUBLANES
NEG = -1e30
LOG2E = math.log2(math.e)
V7X_VMEM_LIMIT = 56 * 1024 * 1024

BF16 = jnp.bfloat16
F32 = jnp.float32


def _mm(a, b):
    return jnp.dot(a.astype(BF16), b.astype(BF16), preferred_element_type=F32)


def _mm_nt(a, b):
    return lax.dot_general(a.astype(BF16), b.astype(BF16), (((1,), (1,)), ((), ())),
                           preferred_element_type=F32)


def _mm_tn(a, b):
    return lax.dot_general(a.astype(BF16), b.astype(BF16), (((0,), (0,)), ((), ())),
                           preferred_element_type=F32)


def _split3(x):
    x1 = x.astype(BF16)
    r1 = x - x1.astype(F32)
    x2 = r1.astype(BF16)
    x3 = (r1 - x2.astype(F32)).astype(BF16)
    return x1, x2, x3


def _mm_exact01(w01, x):
    w = w01.astype(BF16)
    return sum(jnp.dot(w, p, preferred_element_type=F32) for p in _split3(x))


def _mm_exact01_right(x, w01, pieces=3):
    w = w01.astype(BF16)
    return sum(jnp.dot(p, w, preferred_element_type=F32) for p in _split3(x)[:pieces])


def _silu(x):
    return x * jax.nn.sigmoid(x)


def _first_head_lanes():
    return lax.broadcasted_iota(jnp.int32, (1, LANES), 1) < B_HEAD_DIM


def _params(sem, est_bytes):
    limit = int(min(V7X_VMEM_LIMIT, max(32 * 1024 * 1024, est_bytes)))
    return pltpu.CompilerParams(dimension_semantics=sem, vmem_limit_bytes=limit)


def _const_spec(shape):
    nd = len(shape)
    return pl.BlockSpec(shape, lambda *_: (0,) * nd, pipeline_mode=pl.Buffered(1))


def _ln_rows(x, g, b):
    mu = jnp.mean(x, axis=-1, keepdims=True)
    xc = x - mu
    var = jnp.mean(xc * xc, axis=-1, keepdims=True)
    return xc * lax.rsqrt(var + LN_EPS) * g + b


CONV_COLS = 2 * LANES


def _l2norm_pair(xs, scale):
    first = _first_head_lanes()
    x2 = xs * xs
    zero = jnp.zeros_like(x2)
    s0 = jnp.sum(jnp.where(first, x2, zero), axis=-1, keepdims=True)
    s1 = jnp.sum(jnp.where(first, zero, x2), axis=-1, keepdims=True)
    inv = jnp.where(first, lax.rsqrt(s0 + NORM_EPS), lax.rsqrt(s1 + NORM_EPS))
    return xs * (inv * scale)


def _conv_qkv_cols(xbuf, convw_ref, rows, c0):
    cols = slice(c0, c0 + CONV_COLS)
    first_tap = CONV_HIST - (CONV_W - 1)
    acc = xbuf[first_tap:first_tap + rows, cols] * convw_ref[0:1, cols]
    for i in range(1, CONV_W):
        acc = acc + xbuf[first_tap + i:first_tap + i + rows, cols] * convw_ref[i:i + 1, cols]
    tail = xbuf[rows:rows + CONV_HIST, cols]
    xbuf[0:CONV_HIST, cols] = tail
    c = _silu(acc)
    if c0 >= 2 * B_WIDTH:
        return c
    scale = B_HEAD_DIM ** -0.5 if c0 < B_WIDTH else 1.0
    return jnp.concatenate([_l2norm_pair(c[:, j * LANES:(j + 1) * LANES], scale)
                            for j in range(CONV_COLS // LANES)], axis=1)


def _inproj_kernel(x_ref, wa_ref, wb_ref, wz_ref, wab_ref, wg_ref, convw_ref, lng_ref, lnb_ref,
                   q_ref, k_ref, v_ref, za_ref, xb_ref, zb_ref, ab_ref, ga_ref, gb_ref,
                   kt_ref, vt_ref, xtail_ref, xbuf, *, fuse_conv, pre_ln):
    tm = x_ref.shape[1]
    dot = functools.partial(jnp.dot, preferred_element_type=F32)
    if fuse_conv:
        @pl.when(pl.program_id(1) == 0)
        def _():
            xbuf[0:CONV_HIST, :] = jnp.zeros((CONV_HIST, B_QKV), F32)

    x = x_ref[0]
    if pre_ln:
        x = _ln_rows(x, lng_ref[...], lnb_ref[...])
    x = x.astype(BF16)

    def attn_cols(j):
        r = dot(x, wa_ref[:, j * A_WIDTH:(j + 1) * A_WIDTH])
        if j == 0:
            q_ref[0] = r.astype(BF16)
        elif j == 3:
            za_ref[0] = r.astype(BF16)
        else:
            (k_ref, v_ref)[j - 1][0] = r.astype(BF16)
            (kt_ref, vt_ref)[j - 1][0] = r

    def gate_cols(j):
        (ga_ref, gb_ref)[j][0] = dot(x, wg_ref[:, j * D_MODEL:(j + 1) * D_MODEL]).astype(BF16)

    def zb_cols():
        zb_ref[0] = dot(x, wz_ref[...]).astype(BF16)
        ab_ref[0] = dot(x, wab_ref[...])

    others = [functools.partial(attn_cols, 0), functools.partial(attn_cols, 1), functools.partial(attn_cols, 2),
              functools.partial(attn_cols, 3), zb_cols, functools.partial(gate_cols, 0),
              functools.partial(gate_cols, 1)]
    for step in range(B_QKV // CONV_COLS):
        c0 = step * CONV_COLS
        xb = dot(x, wb_ref[:, c0:c0 + CONV_COLS])
        xtail_ref[0, :, c0:c0 + CONV_COLS] = xb[tm - CONV_HIST:tm, :]
        if fuse_conv:
            xbuf[CONV_HIST:CONV_HIST + tm, c0:c0 + CONV_COLS] = xb
        else:
            xb_ref[0, :, c0:c0 + CONV_COLS] = xb
        if others:
            others.pop(0)()
        if fuse_conv:
            xb_ref[0, :, c0:c0 + CONV_COLS] = _conv_qkv_cols(xbuf, convw_ref, tm, c0)
    for rest in others:
        rest()


def _in_projection(h, w, *, tm, tail, fuse_conv, pre_ln):
    bsz, t, d = h.shape
    nt = t // tm
    tail_blocks = tail // tm

    def row(width):
        return pl.BlockSpec((1, tm, width), lambda b, i: (b, i, 0))

    tail_spec = pl.BlockSpec((1, tm, A_WIDTH),
                             lambda b, i: (b, jnp.maximum(i - (nt - tail_blocks), 0), 0))
    xtail_spec = pl.BlockSpec((1, CONV_HIST, B_QKV), lambda b, i: (b, 0, 0))
    out_shapes = [
        jax.ShapeDtypeStruct((bsz, t, A_WIDTH), BF16),
        jax.ShapeDtypeStruct((bsz, t, A_WIDTH), BF16),
        jax.ShapeDtypeStruct((bsz, t, A_WIDTH), BF16),
        jax.ShapeDtypeStruct((bsz, t, A_WIDTH), BF16),
        jax.ShapeDtypeStruct((bsz, t, B_QKV), F32),
        jax.ShapeDtypeStruct((bsz, t, B_WIDTH), BF16),
        jax.ShapeDtypeStruct((bsz, t, LANES), F32),
        jax.ShapeDtypeStruct((bsz, t, D_MODEL), BF16),
        jax.ShapeDtypeStruct((bsz, t, D_MODEL), BF16),
        jax.ShapeDtypeStruct((bsz, tail, A_WIDTH), F32),
        jax.ShapeDtypeStruct((bsz, tail, A_WIDTH), F32),
        jax.ShapeDtypeStruct((bsz, CONV_HIST, B_QKV), F32),
    ]
    out_specs = [row(A_WIDTH), row(A_WIDTH), row(A_WIDTH), row(A_WIDTH), row(B_QKV), row(B_WIDTH),
                 row(LANES), row(D_MODEL), row(D_MODEL), tail_spec, tail_spec, xtail_spec]
    weights = (w["wa"], w["wb"], w["wz"], w["wab"], w["wg"], w["conv_w"], w["ln0_g"], w["ln0_b"])
    in_specs = [row(d)] + [_const_spec(x.shape) for x in weights]
    n_w = sum(int(np.prod(x.shape)) for x in weights)
    est = 2 * n_w + 2 * tm * (d * 4 + 3 * A_WIDTH * 2 + (2 * A_WIDTH + B_QKV + B_WIDTH + LANES
                                                        + 2 * D_MODEL + 2 * A_WIDTH) * 4)
    est += tm * (4 * A_WIDTH + 2 * D_MODEL + 4 * B_QKV) * 4 * 2
    return pl.pallas_call(
        functools.partial(_inproj_kernel, fuse_conv=fuse_conv, pre_ln=pre_ln),
        grid=(bsz, nt),
        in_specs=in_specs,
        out_specs=out_specs,
        out_shape=out_shapes,
        scratch_shapes=[pltpu.VMEM((CONV_HIST + tm, B_QKV), F32)],
        compiler_params=_params(("parallel", "arbitrary"), est + (8 << 20)),
        name="in_projection",
    )(h, *weights)


def _attention_heads(jobs, bias_ref):
    first = _first_head_lanes()
    c1 = (A_HEAD_DIM ** -0.5) * LOG2E

    def scores(job, p):
        q, k = job[0], job[1]
        qp, kp = q[:, p * LANES:(p + 1) * LANES], k[:, p * LANES:(p + 1) * LANES]
        zero = jnp.zeros_like(qp)
        q2 = jnp.concatenate([jnp.where(first, qp, zero), jnp.where(first, zero, qp)], axis=0)
        return _mm_nt(q2, kp)

    def softmax(job, p, s):
        s = s * c1 + bias_ref[p]
        if job[4] is not None:
            s = jnp.where(job[4], s, NEG)
        e = jnp.exp2(s - jnp.max(s, axis=-1, keepdims=True))
        return e.astype(BF16), jnp.sum(e, axis=-1, keepdims=True)

    def output(job, p, e_den):
        _, _, v, za, _, o_ref = job
        nq = za.shape[0]
        sl = slice(p * LANES, (p + 1) * LANES)
        pv = _mm(e_den[0], v[:, sl]) / e_den[1]
        o = jnp.where(first, pv[0:nq], pv[nq:2 * nq])
        o_ref[:, sl] = (o * _silu(za[:, sl].astype(F32))).astype(o_ref.dtype)

    chains = [(job, p) for job in jobs for p in range(HEAD_PAIRS)]
    s_val, e_val = {}, {}
    for step in range(len(chains) + 2):
        if step < len(chains):
            s_val[step] = scores(*chains[step])
        if 0 <= step - 1 < len(chains):
            e_val[step - 1] = softmax(*chains[step - 1], s_val.pop(step - 1))
        if 0 <= step - 2 < len(chains):
            output(*chains[step - 2], e_val.pop(step - 2))


def _attn_prompt_kernel(q_ref, *refs, qblk, n_sub, n_kblk):
    k_refs, v_refs = refs[0:n_kblk], refs[n_kblk:2 * n_kblk]
    za_ref, bias_ref, o_ref = refs[2 * n_kblk:]
    m = pl.program_id(1)
    n_lead = n_kblk - n_sub
    band = (n_lead + 1) * qblk

    def jobs(masked):
        out = []
        for u in range(n_sub):
            k = jnp.concatenate([r[0] for r in k_refs[u:u + n_lead + 1]], axis=0)
            v = jnp.concatenate([r[0] for r in v_refs[u:u + n_lead + 1]], axis=0)
            valid = None
            if masked:
                col = lax.broadcasted_iota(jnp.int32, (1, band), 1) + (m * n_sub + u - n_lead) * qblk
                valid = col >= 0
            rows = slice(u * qblk, (u + 1) * qblk)
            out.append((q_ref[0, rows, :], k, v, za_ref[0, rows, :], valid, o_ref.at[0, rows, :]))
        return out

    @pl.when(m * n_sub >= n_lead)
    def _():
        _attention_heads(jobs(False), bias_ref)

    @pl.when(m * n_sub < n_lead)
    def _():
        _attention_heads(jobs(True), bias_ref)


def _rel_bias_table(rel_bias, nq, nk, n_before):
    h = rel_bias.shape[0]
    n_var = nq + REL_MAX
    assert nk - n_var >= 0 and n_before + nq == nk
    period = n_var + nq
    far = rel_bias[:, REL_MAX + CHUNK - 1:]
    n_low = n_var - (REL_MAX + CHUNK)
    g = jnp.concatenate([rel_bias[:, ::-1], jnp.repeat(rel_bias[:, :1], n_low, axis=1),
                         jnp.repeat(far, period - n_var, axis=1)], axis=1)
    skew = jnp.tile(g, (1, nq))[:, :nq * (period - 1)].reshape(h, nq, period - 1)[:, :, :n_var]
    const = jnp.broadcast_to(far[:, :, None], (h, nq, nk - n_var))
    return jnp.concatenate([const, skew], axis=2)


def _pair_tables(tab):
    h, nq, nk = tab.shape
    return (tab * LOG2E).astype(F32).reshape(h // 2, 2 * nq, nk)


def _attention_prompt(q, k, v, za, rel_bias, qblk=2 * CHUNK, n_sub=2):
    bsz, t, _ = q.shape
    n_lead = ATT_WINDOW // qblk
    n_kblk = n_lead + n_sub
    band = (n_lead + 1) * qblk
    i = np.arange(qblk)[:, None]
    j = np.arange(band)[None, :]
    qc, kc = i // CHUNK, j // CHUNK
    visible = (kc >= qc) & (kc <= qc + BAND_CHUNKS)
    tab = _rel_bias_table(rel_bias, qblk, band, ATT_WINDOW)
    tab = _pair_tables(jnp.where(jnp.asarray(visible)[None], tab, NEG))
    row = pl.BlockSpec((1, n_sub * qblk, A_WIDTH), lambda b, m: (b, m, 0))

    def key_block(jb):
        return pl.BlockSpec((1, qblk, A_WIDTH), lambda b, m: (b, jnp.maximum(m * n_sub - n_lead + jb, 0), 0))

    kv_specs = [key_block(jb) for jb in range(n_kblk)]
    est = tab.size * 4 + 2 * (2 * n_kblk + 2 * n_sub) * qblk * A_WIDTH * 2 + n_sub * 32 * qblk * band * 4
    return pl.pallas_call(
        functools.partial(_attn_prompt_kernel, qblk=qblk, n_sub=n_sub, n_kblk=n_kblk),
        grid=(bsz, t // (n_sub * qblk)),
        in_specs=[row] + kv_specs + kv_specs + [row, _const_spec(tab.shape)],
        out_specs=row,
        out_shape=jax.ShapeDtypeStruct((bsz, t, A_WIDTH), BF16),
        compiler_params=_params(("parallel", "arbitrary"), est + (8 << 20)),
        name="attention_prompt",
    )(q, *([k] * n_kblk), *([v] * n_kblk), za, tab)


def _attn_sample_kernel(q_ref, kc_ref, vc_ref, kn_ref, vn_ref, za_ref, bias_ref, o_ref, *, nk):
    n_past, t = kc_ref.shape[1], kn_ref.shape[1]
    zeros = jnp.zeros((nk - n_past - t, A_WIDTH), BF16)
    k = jnp.concatenate([kc_ref[0].astype(BF16), kn_ref[0].astype(BF16), zeros], axis=0)
    v = jnp.concatenate([vc_ref[0].astype(BF16), vn_ref[0].astype(BF16), zeros], axis=0)
    _attention_heads([(q_ref[0], k, v, za_ref[0], None, o_ref.at[0])], bias_ref)


def _attention_sample(q, k_cache, v_cache, k_new, v_new, za, rel_bias):
    bsz, t, _ = q.shape
    n_past = k_cache.shape[1]
    nk = -(-(n_past + t) // LANES) * LANES
    nq_tab = nk - n_past
    tab = _rel_bias_table(rel_bias, nq_tab, nk, n_past)[:, :t]
    tab = _pair_tables(jnp.where(jnp.asarray(np.arange(nk) < n_past + t)[None, None], tab, NEG))

    def blk(rows):
        return pl.BlockSpec((1, rows, A_WIDTH), lambda b: (b, 0, 0))

    return pl.pallas_call(
        functools.partial(_attn_sample_kernel, nk=nk),
        grid=(bsz,),
        in_specs=[blk(t), blk(n_past), blk(n_past), blk(t), blk(t), blk(t), _const_spec(tab.shape)],
        out_specs=blk(t),
        out_shape=jax.ShapeDtypeStruct((bsz, t, A_WIDTH), BF16),
        compiler_params=_params(("parallel",), 32 << 20),
        name="attention_sample",
    )(q, k_cache, v_cache, k_new, v_new, za, tab)


_KB, _QN, _KN, _VB, _KBE, _QE, _KD = range(7)


def _gdn_kernel(xb_ref, ab_ref, zb_ref, conv0_ref, s0_ref, convw_ref, arow_ref, dtrow_ref, normw_ref,
                o_ref, s_out_ref, xbuf, s_scr, nat_scr, gcx_scr, sdec_scr, aq_scr, bo_scr,
                *, tb, n_valid, do_conv, n_par, pipelined):
    t_idx = pl.program_id(1)
    blk = GDN_CHUNK
    two = 2 * blk
    n_chunks = tb // blk

    assert not (pipelined and do_conv)
    rows_out = min(blk, n_valid)

    @pl.when(t_idx == 0)
    def _():
        s_scr[...] = s0_ref[0]
        if do_conv:
            xbuf[0:CONV_HIST, :] = conv0_ref[0]
        if pipelined:
            aq_scr[...] = jnp.zeros(aq_scr.shape, aq_scr.dtype)
            bo_scr[...] = jnp.zeros(bo_scr.shape, bo_scr.dtype)
            sdec_scr[...] = jnp.ones(sdec_scr.shape, sdec_scr.dtype)

    def phase2(cidx):
        r0 = cidx * blk
        for p in range(HEAD_PAIRS):
            s_old = s_scr[p]
            tot = bo_scr[cidx, p] + _mm(aq_scr[cidx, p], s_old)
            dec = sdec_scr[cidx, 0:1, p * LANES:(p + 1) * LANES]
            s_scr[p] = s_old * dec + tot[0:two]
            o = tot[two:2 * two]
            ms = jnp.sum(o * o, axis=-1, keepdims=True) * (1.0 / B_HEAD_DIM)
            on = o * lax.rsqrt(ms + NORM_EPS) * normw_ref[...]
            o_pair = (on[0:blk] + on[blk:two])[0:rows_out]
            zb = zb_ref[0, r0:r0 + rows_out, p * LANES:(p + 1) * LANES].astype(F32)
            o_ref[0, r0:r0 + rows_out, p * LANES:(p + 1) * LANES] = (o_pair * _silu(zb)).astype(o_ref.dtype)

    pending = list(range(n_chunks)) if pipelined else []

    def padded(x):
        if n_valid == tb:
            return x
        return jnp.concatenate([x, jnp.zeros((tb - n_valid, x.shape[1]), x.dtype)], axis=0)

    if do_conv:
        xbuf[CONV_HIST:CONV_HIST + tb, :] = padded(xb_ref[0])
        c = jnp.concatenate([_conv_qkv_cols(xbuf, convw_ref, tb, c0) for c0 in range(0, B_QKV, CONV_COLS)], axis=1)
    else:
        c = xb_ref[0]
    qn, kn, v = c[:, 0:B_WIDTH], c[:, B_WIDTH:2 * B_WIDTH], c[:, 2 * B_WIDTH:]

    ab = padded(ab_ref[0])
    g = arow_ref[...] * jax.nn.softplus(ab + dtrow_ref[...])
    beta = jax.nn.sigmoid(ab)
    if n_valid != tb:
        is_token = lax.broadcasted_iota(jnp.int32, (tb, 1), 0) < n_valid
        g = jnp.where(is_token, g, 0.0)
        beta = jnp.where(is_token, beta, 0.0)
    rt = lax.broadcasted_iota(jnp.int32, (tb, tb), 0)
    ct = lax.broadcasted_iota(jnp.int32, (tb, tb), 1)
    gc = _mm_exact01(((rt // blk) == (ct // blk)) & (rt >= ct), g)
    src = lax.broadcasted_iota(jnp.int32, (LANES, B_WIDTH), 0)
    head = lax.broadcasted_iota(jnp.int32, (LANES, B_WIDTH), 1) // B_HEAD_DIM
    gcx = _mm_exact01_right(gc, src == head)
    betax = _mm_exact01_right(beta, src == head + B_HEADS, pieces=2)
    glx = jnp.concatenate([jnp.broadcast_to(gcx[(i + 1) * blk - 1:(i + 1) * blk, :], (blk, B_WIDTH))
                           for i in range(n_chunks)], axis=0)
    egc = jnp.exp(gcx)
    kbeta = kn * betax
    nat_scr[_KB] = kbeta.astype(BF16)
    nat_scr[_QN] = qn.astype(BF16)
    nat_scr[_KN] = kn.astype(BF16)
    nat_scr[_VB] = (v * betax).astype(BF16)
    nat_scr[_KBE] = (kbeta * egc).astype(BF16)
    nat_scr[_QE] = (qn * egc).astype(BF16)
    nat_scr[_KD] = (kn * jnp.exp(glx - gcx)).astype(BF16)
    gcx_scr[...] = gcx
    sdec_rows = [jnp.exp(gcx[(i + 1) * blk - 1:(i + 1) * blk, :]) for i in range(n_chunks)]

    first = _first_head_lanes()
    r = lax.broadcasted_iota(jnp.int32, (two, two), 0)
    cc = lax.broadcasted_iota(jnp.int32, (two, two), 1)
    causal = ((r // blk) == (cc // blk)) & (r >= cc)
    eye = (r == cc).astype(F32)
    off_diag = 1.0 - eye
    diag = (lax.broadcasted_iota(jnp.int32, (blk, LANES), 0)
            == lax.broadcasted_iota(jnp.int32, (blk, LANES), 1) % B_HEAD_DIM)

    def level_mask(bs):
        return ((r // bs) == (cc // bs)) & ((r % bs) >= bs // 2) & ((cc % bs) < bs // 2)

    def stacked(which, r0, p):
        x = nat_scr[which, pl.ds(r0, blk), p * LANES:(p + 1) * LANES]
        zero = jnp.zeros_like(x)
        return jnp.concatenate([jnp.where(first, x, zero), jnp.where(first, zero, x)], axis=0)

    def phase1(step, carry):
        chains = []
        for dc in range(n_par):
            cidx = step * n_par + dc
            for p in range(HEAD_PAIRS):
                chains.append({"cidx": cidx, "r0": cidx * blk, "p": p})
        for ch in chains:
            r0, p = ch["r0"], ch["p"]
            gt = gcx_scr[pl.ds(r0, blk), p * LANES:(p + 1) * LANES]
            gc_row = jnp.sum(jnp.where(diag, gt, 0.0), axis=0, keepdims=True)
            ch["decay"] = jnp.exp(jnp.where(causal, jnp.concatenate([gt, gt], axis=0) - gc_row, NEG))
            lhs = jnp.concatenate([stacked(_KB, r0, p), stacked(_QN, r0, p)], axis=0)
            gram = _mm_nt(lhs, stacked(_KN, r0, p))
            ch["m"] = gram[0:two] * ch["decay"] * off_diag
            ch["qk"] = (gram[two:2 * two] * ch["decay"]).astype(BF16)
            ch["x"] = eye - jnp.where(level_mask(2), ch["m"], 0.0)
        bs = 4
        while bs <= blk:
            lm = level_mask(bs)
            for ch in chains:
                ch["t"] = _mm(jnp.where(lm, ch["m"], 0.0), ch["x"])
            if pending:
                phase2(pending.pop(0))
            for ch in chains:
                ch["x"] = ch["x"] - _mm(ch["x"], ch["t"])
            bs *= 2
        while pending:
            phase2(pending.pop(0))
        for ch in chains:
            r0, p = ch["r0"], ch["p"]
            rhs = jnp.concatenate([stacked(_VB, r0, p), stacked(_KBE, r0, p)], axis=1)
            ch["uw"] = _mm(ch["x"], rhs).astype(BF16)
        for ch in chains:
            r0, p = ch["r0"], ch["p"]
            kd_uw = _mm_tn(stacked(_KD, r0, p), ch["uw"])
            qk_uw = _mm(ch["qk"], ch["uw"])
            qeff = stacked(_QE, r0, p).astype(F32) - qk_uw[:, LANES:2 * LANES]
            aq_scr[ch["cidx"], p] = jnp.concatenate([-kd_uw[:, LANES:2 * LANES], qeff], axis=0).astype(BF16)
            bo_scr[ch["cidx"], p] = jnp.concatenate([kd_uw[:, 0:LANES], qk_uw[:, 0:LANES]], axis=0)
        return carry

    for step in range(n_chunks // n_par):
        phase1(step, 0)
    for i in range(n_chunks):
        sdec_scr[i] = jnp.broadcast_to(sdec_rows[i], (SUBLANES, B_WIDTH))
    if not pipelined:
        for i in range(n_chunks):
            phase2(i)

    @pl.when(t_idx == pl.num_programs(1) - 1)
    def _():
        s_out_ref[0] = s_scr[...]


def _pair_state(s):
    bsz = s.shape[0]
    s = s.reshape(bsz, HEAD_PAIRS, 2, B_HEAD_DIM, B_HEAD_DIM)
    z = jnp.zeros_like(s[:, :, 0])
    top = jnp.concatenate([s[:, :, 0], z], axis=-1)
    bot = jnp.concatenate([z, s[:, :, 1]], axis=-1)
    return jnp.concatenate([top, bot], axis=-2)


def _unpair_state(sp):
    d = B_HEAD_DIM
    s = jnp.stack([sp[:, :, 0:d, 0:d], sp[:, :, d:2 * d, d:2 * d]], axis=2)
    return s.reshape(sp.shape[0], B_HEADS, d, d)


def _gated_delta(xb, ab, zb, conv_prev, s_prev, lw, *, tb, do_conv, pipelined):
    bsz, t, _ = xb.shape
    n_valid = min(t, tb)
    assert t % n_valid == 0 and tb % GDN_CHUNK == 0 and (n_valid == tb or t == n_valid)
    conv0 = jnp.concatenate([jnp.zeros((bsz, CONV_HIST - (CONV_W - 1), B_QKV), F32), conv_prev.astype(F32)], axis=1)
    s0 = _pair_state(s_prev.astype(F32))

    nt = t // n_valid

    def row(width, lag):
        if not pipelined:
            return pl.BlockSpec((1, n_valid, width), lambda b, i: (b, i, 0))
        if lag:
            return pl.BlockSpec((1, n_valid, width), lambda b, i: (b, jnp.maximum(i - 1, 0), 0))
        return pl.BlockSpec((1, n_valid, width), lambda b, i: (b, jnp.minimum(i, nt - 1), 0))

    def per_batch(shape):
        nd = len(shape)
        return pl.BlockSpec((1,) + shape, lambda b, i: (b,) + (0,) * nd)

    n_chunks = tb // GDN_CHUNK
    n_par = max(p for p in (1, 2, 4) if n_chunks % p == 0)
    per_chain = (n_chunks, HEAD_PAIRS, 2 * LANES, LANES)
    est = 2 * n_valid * (B_QKV + LANES + B_WIDTH) * 4 + (tb + CONV_HIST) * B_QKV * 4 + 6 * HEAD_PAIRS * LANES * LANES * 4
    est += tb * B_WIDTH * (7 * 2 + 4 + 16 * 4) + n_chunks * HEAD_PAIRS * LANES * LANES * 8 + tb * tb * 8
    o, s_out = pl.pallas_call(
        functools.partial(_gdn_kernel, tb=tb, n_valid=n_valid, do_conv=do_conv, n_par=n_par, pipelined=pipelined),
        grid=(bsz, nt + 1 if pipelined else nt),
        in_specs=[row(B_QKV, False), row(LANES, False), row(B_WIDTH, True), per_batch((CONV_HIST, B_QKV)),
                  per_batch((HEAD_PAIRS, LANES, LANES)),
                  _const_spec((CONV_W, B_QKV)), _const_spec((1, LANES)), _const_spec((1, LANES)),
                  _const_spec((1, LANES))],
        out_specs=[row(B_WIDTH, True), per_batch((HEAD_PAIRS, LANES, LANES))],
        out_shape=[jax.ShapeDtypeStruct((bsz, t, B_WIDTH), BF16),
                   jax.ShapeDtypeStruct((bsz, HEAD_PAIRS, LANES, LANES), F32)],
        scratch_shapes=[pltpu.VMEM((CONV_HIST + tb, B_QKV), F32),
                        pltpu.VMEM((HEAD_PAIRS, LANES, LANES), F32),
                        pltpu.VMEM((7, tb, B_WIDTH), BF16),
                        pltpu.VMEM((tb, B_WIDTH), F32),
                        pltpu.VMEM((n_chunks, SUBLANES, B_WIDTH), F32),
                        pltpu.VMEM(per_chain, BF16), pltpu.VMEM(per_chain, F32)],
        compiler_params=_params(("parallel", "arbitrary"), est + (16 << 20)),
        name="gated_delta",
    )(xb, ab, zb, conv0, s0, lw["conv_w"], lw["a_row"], lw["dt_row"], lw["normw_row"])
    return o, _unpair_state(s_out)


def _out_kernel(x_ref, a_ref, b_ref, ga_ref, gb_ref, wa_ref, wb_ref, wo_ref, g_ref, beta_ref, lng_ref, lnb_ref,
                y_ref, *, pre_ln, n_split):
    dot = functools.partial(jnp.dot, preferred_element_type=F32)
    rows = x_ref.shape[0] // n_split

    def project(u):
        sl = slice(u * rows, (u + 1) * rows)
        ya = dot(a_ref[sl, :], wa_ref[...])
        yb = dot(b_ref[sl, :], wb_ref[...])
        mix = jax.nn.sigmoid(ga_ref[sl, :].astype(F32)) * ya + jax.nn.sigmoid(gb_ref[sl, :].astype(F32)) * yb
        return dot(mix.astype(BF16), wo_ref[...])

    def finish(u, out):
        sl = slice(u * rows, (u + 1) * rows)
        x = x_ref[sl, :]
        if pre_ln:
            x = _ln_rows(x, lng_ref[...], lnb_ref[...])
        y_ref[sl, :] = _ln_rows(ALPHA * x + out, g_ref[...], beta_ref[...])

    outs = {}
    for step in range(n_split + 1):
        if step < n_split:
            outs[step] = project(step)
        if step >= 1:
            finish(step - 1, outs.pop(step - 1))


def _out_projection(x, a, b, ga, gb, w, tm, pre_ln):
    n, d = x.shape

    def row(width):
        return pl.BlockSpec((tm, width), lambda i: (i, 0))

    weights = (w["w_branch_a"], w["w_branch_b"], w["w_out"], w["ln_g"], w["ln_b"], w["ln0_g"], w["ln0_b"])
    est = 2 * tm * (4 * d * 4 + 2 * A_WIDTH * 2) + 2 * (2 * A_WIDTH * d + d * d) + 6 * tm * d * 4
    return pl.pallas_call(
        functools.partial(_out_kernel, pre_ln=pre_ln, n_split=max(1, tm // LANES)),
        grid=(n // tm,),
        in_specs=[row(d), row(A_WIDTH), row(B_WIDTH), row(d), row(d)] + [_const_spec(x.shape) for x in weights],
        out_specs=row(d),
        out_shape=jax.ShapeDtypeStruct((n, d), F32),
        compiler_params=_params(("parallel",), est + (8 << 20)),
        name="out_projection",
    )(x, a, b, ga, gb, *weights)


def _layer_weights(l, w_in, conv_w, a_log, dt_bias, gdn_norm_w, w_branch_a, w_branch_b, w_out, ln_g, ln_b,
                   ln0_g, ln0_b):
    wi = w_in[l]
    c0 = 4 * A_WIDTH
    c1 = c0 + B_QKV
    c2 = c1 + B_WIDTH
    c3 = c2 + 2 * B_HEADS
    wab = jnp.pad(wi[:, c2:c3], ((0, 0), (0, LANES - 2 * B_HEADS)))
    pad_h = (0, LANES - B_HEADS)
    return {
        "wa": wi[:, 0:c0].astype(BF16), "wb": wi[:, c0:c1].astype(BF16), "wz": wi[:, c1:c2].astype(BF16),
        "wab": wab.astype(BF16), "wg": wi[:, c3:].astype(BF16),
        "conv_w": conv_w[l].astype(F32),
        "a_row": jnp.pad(-jnp.exp(a_log[l].astype(F32)), pad_h).reshape(1, LANES),
        "dt_row": jnp.pad(dt_bias[l].astype(F32), pad_h).reshape(1, LANES),
        "normw_row": jnp.tile(gdn_norm_w[l].astype(F32), 2).reshape(1, LANES),
        "w_branch_a": w_branch_a[l].astype(BF16), "w_branch_b": w_branch_b[l].astype(BF16),
        "w_out": w_out[l].astype(BF16),
        "ln_g": ln_g[l].astype(F32).reshape(1, D_MODEL), "ln_b": ln_b[l].astype(F32).reshape(1, D_MODEL),
        "ln0_g": ln0_g.astype(F32).reshape(1, D_MODEL), "ln0_b": ln0_b.astype(F32).reshape(1, D_MODEL),
    }


def _prompt_layer(h, lw, rel_bias, *, tm, tb, pre_ln):
    bsz, t, d = h.shape
    keep = min(ATT_WINDOW, t)
    q, k, v, za, xb, zb, ab, ga, gb, k_tail, v_tail, x_tail = _in_projection(
        h, lw, tm=tm, tail=keep, fuse_conv=True, pre_ln=pre_ln)
    oa = _attention_prompt(q, k, v, za, rel_bias)
    conv_zero = jnp.zeros((bsz, CONV_W - 1, B_QKV), F32)
    s_zero = jnp.zeros((bsz, B_HEADS, B_HEAD_DIM, B_HEAD_DIM), F32)
    ob, s_new = _gated_delta(xb, ab, zb, conv_zero, s_zero, lw, tb=tb, do_conv=False, pipelined=True)
    y = _out_projection(h.reshape(bsz * t, d), oa.reshape(bsz * t, A_WIDTH), ob.reshape(bsz * t, B_WIDTH),
                        ga.reshape(bsz * t, d), gb.reshape(bsz * t, d), lw, tm, pre_ln)
    new_k = k_tail.reshape(bsz, keep, A_HEADS, A_HEAD_DIM)
    new_v = v_tail.reshape(bsz, keep, A_HEADS, A_HEAD_DIM)
    return y.reshape(bsz, t, d), (new_k, new_v, x_tail[:, CONV_HIST - (CONV_W - 1):], s_new)


def _sample_layer(h, k_cache, v_cache, conv_prev, s_prev, lw, rel_bias, *, tm, pre_ln):
    bsz, t, d = h.shape
    n = bsz * t
    n_past = k_cache.shape[1]
    q, _, _, za, xb, zb, ab, ga, gb, k_new, v_new, _ = _in_projection(
        h.reshape(1, n, d), lw, tm=tm, tail=n, fuse_conv=False, pre_ln=pre_ln)

    def per_seq(x):
        return x.reshape(bsz, t, x.shape[-1])

    oa = _attention_sample(per_seq(q), k_cache.reshape(bsz, n_past, A_WIDTH), v_cache.reshape(bsz, n_past, A_WIDTH),
                           per_seq(k_new), per_seq(v_new), per_seq(za), rel_bias)
    xb_s = per_seq(xb)
    xp_tail = jnp.concatenate([conv_prev.astype(F32), xb_s], axis=1)[:, -(CONV_W - 1):]
    ob, s_new = _gated_delta(xb_s, per_seq(ab), per_seq(zb), conv_prev, s_prev, lw, tb=GDN_CHUNK, do_conv=True, pipelined=False)
    y = _out_projection(h.reshape(n, d), oa.reshape(n, A_WIDTH), ob.reshape(n, B_WIDTH),
                        ga.reshape(n, d), gb.reshape(n, d), lw, tm, pre_ln)
    new_k = per_seq(k_new).reshape(bsz, t, A_HEADS, A_HEAD_DIM)
    new_v = per_seq(v_new).reshape(bsz, t, A_HEADS, A_HEAD_DIM)
    return y.reshape(bsz, t, d), (new_k, new_v, xp_tail, s_new)


def kernel(x_prompt, x_sample, cache_attn_k, cache_attn_v, state_conv, state_gdn, ln0_g, ln0_b, w_in, rel_bias,
           conv_w, a_log, dt_bias, gdn_norm_w, w_branch_a, w_branch_b, w_out, ln_g, ln_b):
    bp, tp, d = x_prompt.shape
    bs, ts, _ = x_sample.shape
    hp, hs = x_prompt, x_sample
    outs_p, outs_s = [], []
    for l in range(DEPTH):
        lw = _layer_weights(l, w_in, conv_w, a_log, dt_bias, gdn_norm_w, w_branch_a, w_branch_b, w_out, ln_g, ln_b,
                            ln0_g, ln0_b)
        hp, st_p = _prompt_layer(hp, lw, rel_bias[l].astype(F32), tm=512, tb=256, pre_ln=(l == 0))
        hs, st_s = _sample_layer(hs, cache_attn_k[l], cache_attn_v[l], state_conv[l], state_gdn[l], lw,
                                 rel_bias[l].astype(F32), tm=256, pre_ln=(l == 0))
        outs_p.append(st_p)
        outs_s.append(st_s)

    def stacked(outs, i):
        return jnp.stack([o[i] for o in outs])

    return (hp, hs,
            stacked(outs_p, 0), stacked(outs_p, 1), stacked(outs_p, 2), stacked(outs_p, 3),
            stacked(outs_s, 0), stacked(outs_s, 1), stacked(outs_s, 2), stacked(outs_s, 3))
```

```python
import functools
import math

import numpy as np
import jax
import jax.numpy as jnp
from jax import lax
from jax.experimental import pallas as pl
from jax.experimental.pallas import tpu as pltpu

D_MODEL = 1024
DEPTH = 2
PAST_LEN = 4096
CHUNK = 64
BAND_CHUNKS = 8
ATT_WINDOW = BAND_CHUNKS * CHUNK
A_HEADS = 8
A_HEAD_DIM = 64
A_WIDTH = A_HEADS * A_HEAD_DIM
REL_MAX = 128
B_HEADS = 8
B_HEAD_DIM = 64
B_WIDTH = B_HEADS * B_HEAD_DIM
B_QKV = 3 * B_WIDTH
CONV_W = 4
GDN_CHUNK = 64
ALPHA = (2 * DEPTH) ** 0.25
LN_EPS = 1e-5
NORM_EPS = 1e-6

LANES = 128
SUBLANES = 8
HEAD_PAIRS = A_HEADS // 2
CONV_HIST = SUBLANES
NEG = -1e30
LOG2E = math.log2(math.e)
Q_SCALE = (A_HEAD_DIM ** -0.5) * LOG2E
V7X_VMEM_LIMIT = 56 * 1024 * 1024

BF16 = jnp.bfloat16
F32 = jnp.float32


def _mm(a, b):
    return jnp.dot(a.astype(BF16), b.astype(BF16), preferred_element_type=F32)


def _mm_nt(a, b):
    return lax.dot_general(a.astype(BF16), b.astype(BF16), (((1,), (1,)), ((), ())),
                           preferred_element_type=F32)


def _mm_tn(a, b):
    return lax.dot_general(a.astype(BF16), b.astype(BF16), (((0,), (0,)), ((), ())),
                           preferred_element_type=F32)


def _split3(x):
    x1 = x.astype(BF16)
    r1 = x - x1.astype(F32)
    x2 = r1.astype(BF16)
    x3 = (r1 - x2.astype(F32)).astype(BF16)
    return x1, x2, x3


def _mm_exact01(w01, x):
    w = w01.astype(BF16)
    return sum(jnp.dot(w, p, preferred_element_type=F32) for p in _split3(x))


def _mm_exact01_right(x, w01, pieces=3):
    w = w01.astype(BF16)
    return sum(jnp.dot(p, w, preferred_element_type=F32) for p in _split3(x)[:pieces])


def _sigmoid(x):
    return 0.5 + 0.5 * jnp.tanh(0.5 * x)


def _silu(x):
    h = 0.5 * x
    return h + h * jnp.tanh(h)


def _first_head_lanes():
    return lax.broadcasted_iota(jnp.int32, (1, LANES), 1) < B_HEAD_DIM


def _params(sem, est_bytes):
    limit = int(min(V7X_VMEM_LIMIT, max(32 * 1024 * 1024, est_bytes)))
    return pltpu.CompilerParams(dimension_semantics=sem, vmem_limit_bytes=limit)


def _const_spec(shape):
    nd = len(shape)
    return pl.BlockSpec(shape, lambda *_: (0,) * nd, pipeline_mode=pl.Buffered(1))


def _ln_rows(x, g, b):
    mu = jnp.mean(x, axis=-1, keepdims=True)
    xc = x - mu
    var = jnp.mean(xc * xc, axis=-1, keepdims=True)
    return xc * lax.rsqrt(var + LN_EPS) * g + b


CONV_COLS = 2 * LANES


def _l2norm_pair(xs, scale):
    first = _first_head_lanes()
    x2 = xs * xs
    zero = jnp.zeros_like(x2)
    s0 = jnp.sum(jnp.where(first, x2, zero), axis=-1, keepdims=True)
    s1 = jnp.sum(jnp.where(first, zero, x2), axis=-1, keepdims=True)
    inv = jnp.where(first, lax.rsqrt(s0 + NORM_EPS), lax.rsqrt(s1 + NORM_EPS))
    return xs * (inv * scale)


def _conv_qkv_cols(xbuf, convw_ref, rows, c0, width=CONV_COLS):
    cols = slice(c0, c0 + width)
    first_tap = CONV_HIST - (CONV_W - 1)
    acc = xbuf[first_tap:first_tap + rows, cols] * convw_ref[0:1, cols]
    for i in range(1, CONV_W):
        acc = acc + xbuf[first_tap + i:first_tap + i + rows, cols] * convw_ref[i:i + 1, cols]
    tail = xbuf[rows:rows + CONV_HIST, cols]
    xbuf[0:CONV_HIST, cols] = tail
    c = _silu(acc)
    if c0 >= 2 * B_WIDTH:
        return c
    scale = B_HEAD_DIM ** -0.5 if c0 < B_WIDTH else 1.0
    return jnp.concatenate([_l2norm_pair(c[:, j * LANES:(j + 1) * LANES], scale)
                            for j in range(width // LANES)], axis=1)


def _inproj_kernel(x_ref, wa_ref, wb_ref, wz_ref, wab_ref, wg_ref, convw_ref, lng_ref, lnb_ref,
                   q_ref, k_ref, v_ref, za_ref, xb_ref, zb_ref, ab_ref, ga_ref, gb_ref,
                   kt_ref, vt_ref, xtail_ref, xbuf, *, fuse_conv, pre_ln):
    tm = x_ref.shape[1]
    dot = functools.partial(jnp.dot, preferred_element_type=F32)
    if fuse_conv:
        @pl.when(pl.program_id(1) == 0)
        def _():
            xbuf[0:CONV_HIST, :] = jnp.zeros((CONV_HIST, B_QKV), F32)

    x = x_ref[0]
    if pre_ln:
        x = _ln_rows(x, lng_ref[...], lnb_ref[...])
    x = x.astype(BF16)

    def attn_cols(j, half):
        cols = slice(half * CONV_COLS, (half + 1) * CONV_COLS)
        c0 = j * A_WIDTH + half * CONV_COLS
        r = dot(x, wa_ref[:, c0:c0 + CONV_COLS])
        if j == 0:
            q_ref[0, :, cols] = (r * Q_SCALE).astype(BF16)
        elif j == 3:
            za_ref[0, :, cols] = r.astype(BF16)
        else:
            (k_ref, v_ref)[j - 1][0, :, cols] = r.astype(BF16)
            (kt_ref, vt_ref)[j - 1][0, :, cols] = r

    def gate_cols(j, part):
        cols = slice(part * CONV_COLS, (part + 1) * CONV_COLS)
        c0 = j * D_MODEL + part * CONV_COLS
        (ga_ref, gb_ref)[j][0, :, cols] = dot(x, wg_ref[:, c0:c0 + CONV_COLS]).astype(BF16)

    def zb_cols(half):
        cols = slice(half * CONV_COLS, (half + 1) * CONV_COLS)
        zb_ref[0, :, cols] = dot(x, wz_ref[:, cols]).astype(BF16)

    others = ([functools.partial(attn_cols, j, half) for j in range(4) for half in range(2)]
              + [functools.partial(zb_cols, half) for half in range(2)]
              + [functools.partial(gate_cols, j, part) for j in range(2) for part in range(D_MODEL // CONV_COLS)])
    for step in range(B_QKV // CONV_COLS):
        c0 = step * CONV_COLS
        xb = dot(x, wb_ref[:, c0:c0 + CONV_COLS])
        xtail_ref[0, :, c0:c0 + CONV_COLS] = xb[tm - CONV_HIST:tm, :]
        if fuse_conv:
            xbuf[CONV_HIST:CONV_HIST + tm, c0:c0 + CONV_COLS] = xb
        else:
            xb_ref[0, :, c0:c0 + CONV_COLS] = xb
        for piece in range(CONV_COLS // LANES):
            if others:
                others.pop(0)()
            if fuse_conv:
                cp = c0 + piece * LANES
                xb_ref[0, :, cp:cp + LANES] = _conv_qkv_cols(xbuf, convw_ref, tm, cp, LANES)
    for rest in others:
        rest()
    ab_ref[0] = dot(x, wab_ref[...])


def _in_projection(h, w, *, tm, tail, fuse_conv, pre_ln):
    bsz, t, d = h.shape
    nt = t // tm
    tail_blocks = tail // tm

    def row(width):
        return pl.BlockSpec((1, tm, width), lambda b, i: (b, i, 0))

    tail_spec = pl.BlockSpec((1, tm, A_WIDTH),
                             lambda b, i: (b, jnp.maximum(i - (nt - tail_blocks), 0), 0))
    xtail_spec = pl.BlockSpec((1, CONV_HIST, B_QKV), lambda b, i: (b, 0, 0))
    out_shapes = [
        jax.ShapeDtypeStruct((bsz, t, A_WIDTH), BF16),
        jax.ShapeDtypeStruct((bsz, t, A_WIDTH), BF16),
        jax.ShapeDtypeStruct((bsz, t, A_WIDTH), BF16),
        jax.ShapeDtypeStruct((bsz, t, A_WIDTH), BF16),
        jax.ShapeDtypeStruct((bsz, t, B_QKV), F32),
        jax.ShapeDtypeStruct((bsz, t, B_WIDTH), BF16),
        jax.ShapeDtypeStruct((bsz, t, LANES), F32),
        jax.ShapeDtypeStruct((bsz, t, D_MODEL), BF16),
        jax.ShapeDtypeStruct((bsz, t, D_MODEL), BF16),
        jax.ShapeDtypeStruct((bsz, tail, A_WIDTH), F32),
        jax.ShapeDtypeStruct((bsz, tail, A_WIDTH), F32),
        jax.ShapeDtypeStruct((bsz, CONV_HIST, B_QKV), F32),
    ]
    out_specs = [row(A_WIDTH), row(A_WIDTH), row(A_WIDTH), row(A_WIDTH), row(B_QKV), row(B_WIDTH),
                 row(LANES), row(D_MODEL), row(D_MODEL), tail_spec, tail_spec, xtail_spec]
    weights = (w["wa"], w["wb"], w["wz"], w["wab"], w["wg"], w["conv_w"], w["ln0_g"], w["ln0_b"])
    in_specs = [row(d)] + [_const_spec(x.shape) for x in weights]
    n_w = sum(int(np.prod(x.shape)) for x in weights)
    est = 2 * n_w + 2 * tm * (d * 4 + 3 * A_WIDTH * 2 + (2 * A_WIDTH + B_QKV + B_WIDTH + LANES
                                                        + 2 * D_MODEL + 2 * A_WIDTH) * 4)
    est += tm * (4 * A_WIDTH + 2 * D_MODEL + 4 * B_QKV) * 4 * 2
    return pl.pallas_call(
        functools.partial(_inproj_kernel, fuse_conv=fuse_conv, pre_ln=pre_ln),
        grid=(bsz, nt),
        in_specs=in_specs,
        out_specs=out_specs,
        out_shape=out_shapes,
        scratch_shapes=[pltpu.VMEM((CONV_HIST + tm, B_QKV), F32)],
        compiler_params=_params(("parallel", "arbitrary"), est + (8 << 20)),
        name="in_projection",
    )(h, *weights)


def _attention_heads(jobs, bias_ref):
    first = _first_head_lanes()

    def scores(job, p):
        q, k = job[0], job[1]
        qp, kp = q[:, p * LANES:(p + 1) * LANES], k[:, p * LANES:(p + 1) * LANES]
        zero = jnp.zeros_like(qp)
        q2 = jnp.concatenate([jnp.where(first, qp, zero), jnp.where(first, zero, qp)], axis=0)
        return _mm_nt(q2, kp)

    def softmax(job, p, s):
        s = s + bias_ref[p]
        if job[4] is not None:
            s = jnp.where(job[4], s, NEG)
        e = jnp.exp2(s - jnp.max(s, axis=-1, keepdims=True))
        return e.astype(BF16), jnp.sum(e, axis=-1, keepdims=True)

    def output(job, p, e_den):
        _, _, v, za, _, o_ref = job
        nq = za.shape[0]
        sl = slice(p * LANES, (p + 1) * LANES)
        pv = _mm(e_den[0], v[:, sl]) / e_den[1]
        o = jnp.where(first, pv[0:nq], pv[nq:2 * nq])
        o_ref[:, sl] = (o * _silu(za[:, sl].astype(F32))).astype(o_ref.dtype)

    chains = [(job, p) for job in jobs for p in range(HEAD_PAIRS)]
    s_val, e_val = {}, {}
    for step in range(len(chains) + 2):
        if step < len(chains):
            s_val[step] = scores(*chains[step])
        if 0 <= step - 1 < len(chains):
            e_val[step - 1] = softmax(*chains[step - 1], s_val.pop(step - 1))
        if 0 <= step - 2 < len(chains):
            output(*chains[step - 2], e_val.pop(step - 2))


def _attn_prompt_kernel(q_ref, *refs, qblk, n_sub, n_kblk):
    k_refs, v_refs = refs[0:n_kblk], refs[n_kblk:2 * n_kblk]
    za_ref, bias_ref, o_ref = refs[2 * n_kblk:]
    m = pl.program_id(1)
    n_lead = n_kblk - n_sub
    band = (n_lead + 1) * qblk

    def jobs(masked):
        out = []
        for u in range(n_sub):
            k = jnp.concatenate([r[0] for r in k_refs[u:u + n_lead + 1]], axis=0)
            v = jnp.concatenate([r[0] for r in v_refs[u:u + n_lead + 1]], axis=0)
            valid = None
            if masked:
                col = lax.broadcasted_iota(jnp.int32, (1, band), 1) + (m * n_sub + u - n_lead) * qblk
                valid = col >= 0
            rows = slice(u * qblk, (u + 1) * qblk)
            out.append((q_ref[0, rows, :], k, v, za_ref[0, rows, :], valid, o_ref.at[0, rows, :]))
        return out

    @pl.when(m * n_sub >= n_lead)
    def _():
        _attention_heads(jobs(False), bias_ref)

    @pl.when(m * n_sub < n_lead)
    def _():
        _attention_heads(jobs(True), bias_ref)


def _rel_bias_table(rel_bias, nq, nk, n_before):
    h = rel_bias.shape[0]
    n_var = nq + REL_MAX
    assert nk - n_var >= 0 and n_before + nq == nk
    period = n_var + nq
    far = rel_bias[:, REL_MAX + CHUNK - 1:]
    n_low = n_var - (REL_MAX + CHUNK)
    g = jnp.concatenate([rel_bias[:, ::-1], jnp.repeat(rel_bias[:, :1], n_low, axis=1),
                         jnp.repeat(far, period - n_var, axis=1)], axis=1)
    skew = jnp.tile(g, (1, nq))[:, :nq * (period - 1)].reshape(h, nq, period - 1)[:, :, :n_var]
    const = jnp.broadcast_to(far[:, :, None], (h, nq, nk - n_var))
    return jnp.concatenate([const, skew], axis=2)


def _pair_tables(tab):
    h, nq, nk = tab.shape
    return (tab * LOG2E).astype(F32).reshape(h // 2, 2 * nq, nk)


def _attention_prompt(q, k, v, za, rel_bias, qblk=2 * CHUNK, n_sub=2):
    bsz, t, _ = q.shape
    n_lead = ATT_WINDOW // qblk
    n_kblk = n_lead + n_sub
    band = (n_lead + 1) * qblk
    i = np.arange(qblk)[:, None]
    j = np.arange(band)[None, :]
    qc, kc = i // CHUNK, j // CHUNK
    visible = (kc >= qc) & (kc <= qc + BAND_CHUNKS)
    tab = _rel_bias_table(rel_bias, qblk, band, ATT_WINDOW)
    tab = _pair_tables(jnp.where(jnp.asarray(visible)[None], tab, NEG))
    row = pl.BlockSpec((1, n_sub * qblk, A_WIDTH), lambda b, m: (b, m, 0))

    def key_block(jb):
        return pl.BlockSpec((1, qblk, A_WIDTH), lambda b, m: (b, jnp.maximum(m * n_sub - n_lead + jb, 0), 0))

    kv_specs = [key_block(jb) for jb in range(n_kblk)]
    est = tab.size * 4 + 2 * (2 * n_kblk + 2 * n_sub) * qblk * A_WIDTH * 2 + n_sub * 32 * qblk * band * 4
    return pl.pallas_call(
        functools.partial(_attn_prompt_kernel, qblk=qblk, n_sub=n_sub, n_kblk=n_kblk),
        grid=(bsz, t // (n_sub * qblk)),
        in_specs=[row] + kv_specs + kv_specs + [row, _const_spec(tab.shape)],
        out_specs=row,
        out_shape=jax.ShapeDtypeStruct((bsz, t, A_WIDTH), BF16),
        compiler_params=_params(("parallel", "arbitrary"), est + (8 << 20)),
        name="attention_prompt",
    )(q, *([k] * n_kblk), *([v] * n_kblk), za, tab)


def _attn_sample_kernel(q_ref, kc_ref, vc_ref, kn_ref, vn_ref, za_ref, bias_ref, o_ref, *, nk):
    n_past, t = kc_ref.shape[1], kn_ref.shape[1]
    zeros = jnp.zeros((nk - n_past - t, A_WIDTH), BF16)
    k = jnp.concatenate([kc_ref[0].astype(BF16), kn_ref[0].astype(BF16), zeros], axis=0)
    v = jnp.concatenate([vc_ref[0].astype(BF16), vn_ref[0].astype(BF16), zeros], axis=0)
    _attention_heads([(q_ref[0], k, v, za_ref[0], None, o_ref.at[0])], bias_ref)


def _attention_sample(q, k_cache, v_cache, k_new, v_new, za, rel_bias):
    bsz, t, _ = q.shape
    n_past = k_cache.shape[1]
    nk = -(-(n_past + t) // LANES) * LANES
    nq_tab = nk - n_past
    tab = _rel_bias_table(rel_bias, nq_tab, nk, n_past)[:, :t]
    tab = _pair_tables(jnp.where(jnp.asarray(np.arange(nk) < n_past + t)[None, None], tab, NEG))

    def blk(rows):
        return pl.BlockSpec((1, rows, A_WIDTH), lambda b: (b, 0, 0))

    return pl.pallas_call(
        functools.partial(_attn_sample_kernel, nk=nk),
        grid=(bsz,),
        in_specs=[blk(t), blk(n_past), blk(n_past), blk(t), blk(t), blk(t), _const_spec(tab.shape)],
        out_specs=blk(t),
        out_shape=jax.ShapeDtypeStruct((bsz, t, A_WIDTH), BF16),
        compiler_params=_params(("parallel",), 32 << 20),
        name="attention_sample",
    )(q, k_cache, v_cache, k_new, v_new, za, tab)


_KB, _QN, _KN, _VB, _KBE, _QE, _KD = range(7)


def _gdn_kernel(xb_ref, ab_ref, zb_ref, conv0_ref, s0_ref, convw_ref, arow_ref, dtrow_ref, normw_ref,
                o_ref, s_out_ref, xbuf, s_scr, nat_scr, gcx_scr, sdec_scr, aq_scr, bo_scr,
                *, tb, n_seq, n_valid, do_conv, n_par, pipelined):
    t_idx = pl.program_id(1)
    blk = GDN_CHUNK
    two = 2 * blk
    seq_chunks = tb // blk
    n_chunks = n_seq * seq_chunks
    rows_all = n_seq * tb

    assert not (pipelined and do_conv)
    rows_out = min(blk, n_valid)

    @pl.when(t_idx == 0)
    def _():
        s_scr[...] = s0_ref[...]
        if do_conv:
            xbuf[:, 0:CONV_HIST, :] = conv0_ref[...]
        if pipelined:
            aq_scr[...] = jnp.zeros(aq_scr.shape, aq_scr.dtype)
            bo_scr[...] = jnp.zeros(bo_scr.shape, bo_scr.dtype)
            sdec_scr[...] = jnp.ones(sdec_scr.shape, sdec_scr.dtype)

    def phase2(cidx):
        seq, r0 = cidx // seq_chunks, (cidx % seq_chunks) * blk
        for p in range(HEAD_PAIRS):
            s_old = s_scr[seq, p]
            tot = bo_scr[cidx, p] + _mm(aq_scr[cidx, p], s_old)
            dec = sdec_scr[cidx, 0:1, p * LANES:(p + 1) * LANES]
            s_scr[seq, p] = s_old * dec + tot[0:two]
            o = tot[two:2 * two]
            ms = jnp.sum(o * o, axis=-1, keepdims=True) * (1.0 / B_HEAD_DIM)
            on = o * lax.rsqrt(ms + NORM_EPS) * normw_ref[...]
            o_pair = (on[0:blk] + on[blk:two])[0:rows_out]
            zb = zb_ref[seq, r0:r0 + rows_out, p * LANES:(p + 1) * LANES].astype(F32)
            o_ref[seq, r0:r0 + rows_out, p * LANES:(p + 1) * LANES] = (o_pair * _silu(zb)).astype(o_ref.dtype)

    pending = list(range(n_chunks)) if pipelined else []

    def padded_seq(ref, sq):
        x = ref[sq]
        if n_valid == tb:
            return x
        return jnp.concatenate([x, jnp.zeros((tb - n_valid, x.shape[1]), x.dtype)], axis=0)

    def padded(ref):
        parts = [padded_seq(ref, sq) for sq in range(n_seq)]
        return parts[0] if n_seq == 1 else jnp.concatenate(parts, axis=0)

    if do_conv:
        for sq in range(n_seq):
            xbuf[sq, CONV_HIST:CONV_HIST + tb, :] = padded_seq(xb_ref, sq)
        c = jnp.concatenate(
            [jnp.concatenate([_conv_qkv_cols(xbuf.at[sq], convw_ref, tb, c0) for c0 in range(0, B_QKV, CONV_COLS)],
                             axis=1) for sq in range(n_seq)], axis=0)
    else:
        c = padded(xb_ref)
    qn, kn, v = c[:, 0:B_WIDTH], c[:, B_WIDTH:2 * B_WIDTH], c[:, 2 * B_WIDTH:]

    ab = padded(ab_ref)
    g = arow_ref[...] * jax.nn.softplus(ab + dtrow_ref[...])
    beta = _sigmoid(ab)
    if n_valid != tb:
        is_token = lax.broadcasted_iota(jnp.int32, (rows_all, 1), 0) % tb < n_valid
        g = jnp.where(is_token, g, 0.0)
        beta = jnp.where(is_token, beta, 0.0)
    rt = lax.broadcasted_iota(jnp.int32, (rows_all, rows_all), 0)
    ct = lax.broadcasted_iota(jnp.int32, (rows_all, rows_all), 1)
    gc = _mm_exact01(((rt // blk) == (ct // blk)) & (rt >= ct), g)
    src = lax.broadcasted_iota(jnp.int32, (LANES, B_WIDTH), 0)
    head = lax.broadcasted_iota(jnp.int32, (LANES, B_WIDTH), 1) // B_HEAD_DIM
    gcx = _mm_exact01_right(gc, src == head)
    betax = _mm_exact01_right(beta, src == head + B_HEADS, pieces=2)
    glx = jnp.concatenate([jnp.broadcast_to(gcx[(i + 1) * blk - 1:(i + 1) * blk, :], (blk, B_WIDTH))
                           for i in range(n_chunks)], axis=0)
    egc = jnp.exp(gcx)
    kbeta = kn * betax
    nat_scr[_KB] = kbeta.astype(BF16)
    nat_scr[_QN] = qn.astype(BF16)
    nat_scr[_KN] = kn.astype(BF16)
    nat_scr[_VB] = (v * betax).astype(BF16)
    nat_scr[_KBE] = (kbeta * egc).astype(BF16)
    nat_scr[_QE] = (qn * egc).astype(BF16)
    nat_scr[_KD] = (kn * jnp.exp(glx - gcx)).astype(BF16)
    gcx_scr[...] = gcx
    sdec_rows = [jnp.exp(gcx[(i + 1) * blk - 1:(i + 1) * blk, :]) for i in range(n_chunks)]

    first = _first_head_lanes()
    r = lax.broadcasted_iota(jnp.int32, (two, two), 0)
    cc = lax.broadcasted_iota(jnp.int32, (two, two), 1)
    causal = ((r // blk) == (cc // blk)) & (r >= cc)
    strict = causal & (r > cc)
    eye = (r == cc).astype(F32)
    diag = (lax.broadcasted_iota(jnp.int32, (blk, LANES), 0)
            == lax.broadcasted_iota(jnp.int32, (blk, LANES), 1) % B_HEAD_DIM)

    def level_mask(bs):
        return ((r // bs) == (cc // bs)) & ((r % bs) >= bs // 2) & ((cc % bs) < bs // 2)

    def stacked(which, r0, p):
        x = nat_scr[which, pl.ds(r0, blk), p * LANES:(p + 1) * LANES]
        zero = jnp.zeros_like(x)
        return jnp.concatenate([jnp.where(first, x, zero), jnp.where(first, zero, x)], axis=0)

    def phase1(step, carry):
        chains = []
        for dc in range(n_par):
            cidx = step * n_par + dc
            for p in range(HEAD_PAIRS):
                chains.append({"cidx": cidx, "r0": cidx * blk, "p": p})
        for ch in chains:
            r0, p = ch["r0"], ch["p"]
            gt = gcx_scr[pl.ds(r0, blk), p * LANES:(p + 1) * LANES]
            gc_row = jnp.sum(jnp.where(diag, gt, 0.0), axis=0, keepdims=True)
            ch["decay"] = jnp.exp(jnp.where(causal, jnp.concatenate([gt, gt], axis=0) - gc_row, NEG))
            lhs = jnp.concatenate([stacked(_KB, r0, p), stacked(_QN, r0, p)], axis=0)
            gram = _mm_nt(lhs, stacked(_KN, r0, p))
            ch["m"] = jnp.where(strict, gram[0:two] * ch["decay"], 0.0)
            ch["qk"] = (gram[two:2 * two] * ch["decay"]).astype(BF16)
            ch["x"] = eye - jnp.where(level_mask(2), ch["m"], 0.0)
        bs = 4
        while bs <= blk:
            lm = level_mask(bs)
            for ch in chains:
                ch["t"] = _mm(jnp.where(lm, ch["m"], 0.0), ch["x"])
            if pending:
                phase2(pending.pop(0))
            for ch in chains:
                ch["x"] = ch["x"] - _mm(ch["x"], ch["t"])
            bs *= 2
        while pending:
            phase2(pending.pop(0))
        for ch in chains:
            r0, p = ch["r0"], ch["p"]
            rhs = jnp.concatenate([stacked(_VB, r0, p), stacked(_KBE, r0, p)], axis=1)
            ch["uw"] = _mm(ch["x"], rhs).astype(BF16)
        for ch in chains:
            r0, p = ch["r0"], ch["p"]
            kd_uw = _mm_tn(stacked(_KD, r0, p), ch["uw"])
            qk_uw = _mm(ch["qk"], ch["uw"])
            qeff = stacked(_QE, r0, p).astype(F32) - qk_uw[:, LANES:2 * LANES]
            aq_scr[ch["cidx"], p] = jnp.concatenate([-kd_uw[:, LANES:2 * LANES], qeff], axis=0).astype(BF16)
            bo_scr[ch["cidx"], p] = jnp.concatenate([kd_uw[:, 0:LANES], qk_uw[:, 0:LANES]], axis=0)
        return carry

    for step in range(n_chunks // n_par):
        phase1(step, 0)
    for i in range(n_chunks):
        sdec_scr[i] = jnp.broadcast_to(sdec_rows[i], (SUBLANES, B_WIDTH))
    if not pipelined:
        for i in range(n_chunks):
            phase2(i)

    @pl.when(t_idx == pl.num_programs(1) - 1)
    def _():
        s_out_ref[...] = s_scr[...]


def _pair_state(s):
    bsz = s.shape[0]
    s = s.reshape(bsz, HEAD_PAIRS, 2, B_HEAD_DIM, B_HEAD_DIM)
    z = jnp.zeros_like(s[:, :, 0])
    top = jnp.concatenate([s[:, :, 0], z], axis=-1)
    bot = jnp.concatenate([z, s[:, :, 1]], axis=-1)
    return jnp.concatenate([top, bot], axis=-2)


def _unpair_state(sp):
    d = B_HEAD_DIM
    s = jnp.stack([sp[:, :, 0:d, 0:d], sp[:, :, d:2 * d, d:2 * d]], axis=2)
    return s.reshape(sp.shape[0], B_HEADS, d, d)


def _gated_delta(xb, ab, zb, conv_prev, s_prev, lw, *, tb, do_conv, pipelined):
    bsz, t, _ = xb.shape
    n_valid = min(t, tb)
    assert t % n_valid == 0 and tb % GDN_CHUNK == 0 and (n_valid == tb or t == n_valid)
    conv0 = jnp.concatenate([jnp.zeros((bsz, CONV_HIST - (CONV_W - 1), B_QKV), F32), conv_prev.astype(F32)], axis=1)
    s0 = _pair_state(s_prev.astype(F32))

    nt = t // n_valid
    n_seq = max(s for s in (1, 2, 4) if bsz % s == 0) if (nt == 1 and tb == GDN_CHUNK) else 1

    def row(width, lag):
        if not pipelined:
            return pl.BlockSpec((n_seq, n_valid, width), lambda b, i: (b, i, 0))
        if lag:
            return pl.BlockSpec((n_seq, n_valid, width), lambda b, i: (b, jnp.maximum(i - 1, 0), 0))
        return pl.BlockSpec((n_seq, n_valid, width), lambda b, i: (b, jnp.minimum(i, nt - 1), 0))

    def per_batch(shape):
        nd = len(shape)
        return pl.BlockSpec((n_seq,) + shape, lambda b, i: (b,) + (0,) * nd)

    n_chunks = n_seq * tb // GDN_CHUNK
    n_par = max(p for p in (1, 2, 4) if n_chunks % p == 0)
    per_chain = (n_chunks, HEAD_PAIRS, 2 * LANES, LANES)
    est = 2 * n_valid * (B_QKV + LANES + B_WIDTH) * 4 + (tb + CONV_HIST) * B_QKV * 4 + 6 * HEAD_PAIRS * LANES * LANES * 4
    rows_all = n_seq * tb
    est += rows_all * B_WIDTH * (7 * 2 + 4 + 16 * 4) + n_chunks * HEAD_PAIRS * LANES * LANES * 8 + rows_all * rows_all * 8
    o, s_out = pl.pallas_call(
        functools.partial(_gdn_kernel, tb=tb, n_seq=n_seq, n_valid=n_valid, do_conv=do_conv, n_par=n_par,
                          pipelined=pipelined),
        grid=(bsz // n_seq, nt + 1 if pipelined else nt),
        in_specs=[row(B_QKV, False), row(LANES, False), row(B_WIDTH, True), per_batch((CONV_HIST, B_QKV)),
                  per_batch((HEAD_PAIRS, LANES, LANES)),
                  _const_spec((CONV_W, B_QKV)), _const_spec((1, LANES)), _const_spec((1, LANES)),
                  _const_spec((1, LANES))],
        out_specs=[row(B_WIDTH, True), per_batch((HEAD_PAIRS, LANES, LANES))],
        out_shape=[jax.ShapeDtypeStruct((bsz, t, B_WIDTH), BF16),
                   jax.ShapeDtypeStruct((bsz, HEAD_PAIRS, LANES, LANES), F32)],
        scratch_shapes=[pltpu.VMEM((n_seq, CONV_HIST + tb, B_QKV), F32),
                        pltpu.VMEM((n_seq, HEAD_PAIRS, LANES, LANES), F32),
                        pltpu.VMEM((7, n_seq * tb, B_WIDTH), BF16),
                        pltpu.VMEM((n_seq * tb, B_WIDTH), F32),
                        pltpu.VMEM((n_chunks, SUBLANES, B_WIDTH), F32),
                        pltpu.VMEM(per_chain, BF16), pltpu.VMEM(per_chain, F32)],
        compiler_params=_params(("parallel", "arbitrary"), est + (16 << 20)),
        name="gated_delta",
    )(xb, ab, zb, conv0, s0, lw["conv_w"], lw["a_row"], lw["dt_row"], lw["normw_row"])
    return o, _unpair_state(s_out)


def _out_kernel(x_ref, a_ref, b_ref, ga_ref, gb_ref, wa_ref, wb_ref, wo_ref, g_ref, beta_ref, lng_ref, lnb_ref,
                y_ref, *, pre_ln, n_split):
    dot = functools.partial(jnp.dot, preferred_element_type=F32)
    rows = x_ref.shape[0] // n_split

    def project(u):
        sl = slice(u * rows, (u + 1) * rows)
        ya = dot(a_ref[sl, :], wa_ref[...])
        yb = dot(b_ref[sl, :], wb_ref[...])
        mix = _sigmoid(ga_ref[sl, :].astype(F32)) * ya + _sigmoid(gb_ref[sl, :].astype(F32)) * yb
        return dot(mix.astype(BF16), wo_ref[...])

    def finish(u, out):
        sl = slice(u * rows, (u + 1) * rows)
        x = x_ref[sl, :]
        if pre_ln:
            x = _ln_rows(x, lng_ref[...], lnb_ref[...])
        y_ref[sl, :] = _ln_rows(ALPHA * x + out, g_ref[...], beta_ref[...])

    outs = {}
    for step in range(n_split + 1):
        if step < n_split:
            outs[step] = project(step)
        if step >= 1:
            finish(step - 1, outs.pop(step - 1))


def _out_projection(x, a, b, ga, gb, w, tm, pre_ln):
    n, d = x.shape

    def row(width):
        return pl.BlockSpec((tm, width), lambda i: (i, 0))

    weights = (w["w_branch_a"], w["w_branch_b"], w["w_out"], w["ln_g"], w["ln_b"], w["ln0_g"], w["ln0_b"])
    est = 2 * tm * (4 * d * 4 + 2 * A_WIDTH * 2) + 2 * (2 * A_WIDTH * d + d * d) + 6 * tm * d * 4
    return pl.pallas_call(
        functools.partial(_out_kernel, pre_ln=pre_ln, n_split=max(1, tm // LANES)),
        grid=(n // tm,),
        in_specs=[row(d), row(A_WIDTH), row(B_WIDTH), row(d), row(d)] + [_const_spec(x.shape) for x in weights],
        out_specs=row(d),
        out_shape=jax.ShapeDtypeStruct((n, d), F32),
        compiler_params=_params(("parallel",), est + (8 << 20)),
        name="out_projection",
    )(x, a, b, ga, gb, *weights)


def _layer_weights(l, w_in, conv_w, a_log, dt_bias, gdn_norm_w, w_branch_a, w_branch_b, w_out, ln_g, ln_b,
                   ln0_g, ln0_b):
    wi = w_in[l]
    c0 = 4 * A_WIDTH
    c1 = c0 + B_QKV
    c2 = c1 + B_WIDTH
    c3 = c2 + 2 * B_HEADS
    wab = jnp.pad(wi[:, c2:c3], ((0, 0), (0, LANES - 2 * B_HEADS)))
    pad_h = (0, LANES - B_HEADS)
    return {
        "wa": wi[:, 0:c0].astype(BF16), "wb": wi[:, c0:c1].astype(BF16), "wz": wi[:, c1:c2].astype(BF16),
        "wab": wab.astype(BF16), "wg": wi[:, c3:].astype(BF16),
        "conv_w": conv_w[l].astype(F32),
        "a_row": jnp.pad(-jnp.exp(a_log[l].astype(F32)), pad_h).reshape(1, LANES),
        "dt_row": jnp.pad(dt_bias[l].astype(F32), pad_h).reshape(1, LANES),
        "normw_row": jnp.tile(gdn_norm_w[l].astype(F32), 2).reshape(1, LANES),
        "w_branch_a": w_branch_a[l].astype(BF16), "w_branch_b": w_branch_b[l].astype(BF16),
        "w_out": w_out[l].astype(BF16),
        "ln_g": ln_g[l].astype(F32).reshape(1, D_MODEL), "ln_b": ln_b[l].astype(F32).reshape(1, D_MODEL),
        "ln0_g": ln0_g.astype(F32).reshape(1, D_MODEL), "ln0_b": ln0_b.astype(F32).reshape(1, D_MODEL),
    }


def _prompt_layer(h, lw, rel_bias, *, tm, tb, pre_ln):
    bsz, t, d = h.shape
    keep = min(ATT_WINDOW, t)
    q, k, v, za, xb, zb, ab, ga, gb, k_tail, v_tail, x_tail = _in_projection(
        h, lw, tm=tm, tail=keep, fuse_conv=True, pre_ln=pre_ln)
    oa = _attention_prompt(q, k, v, za, rel_bias)
    conv_zero = jnp.zeros((bsz, CONV_W - 1, B_QKV), F32)
    s_zero = jnp.zeros((bsz, B_HEADS, B_HEAD_DIM, B_HEAD_DIM), F32)
    ob, s_new = _gated_delta(xb, ab, zb, conv_zero, s_zero, lw, tb=tb, do_conv=False, pipelined=True)
    y = _out_projection(h.reshape(bsz * t, d), oa.reshape(bsz * t, A_WIDTH), ob.reshape(bsz * t, B_WIDTH),
                        ga.reshape(bsz * t, d), gb.reshape(bsz * t, d), lw, tm, pre_ln)
    new_k = k_tail.reshape(bsz, keep, A_HEADS, A_HEAD_DIM)
    new_v = v_tail.reshape(bsz, keep, A_HEADS, A_HEAD_DIM)
    return y.reshape(bsz, t, d), (new_k, new_v, x_tail[:, CONV_HIST - (CONV_W - 1):], s_new)


def _sample_layer(h, k_cache, v_cache, conv_prev, s_prev, lw, rel_bias, *, tm, pre_ln):
    bsz, t, d = h.shape
    n = bsz * t
    n_past = k_cache.shape[1]
    q, _, _, za, xb, zb, ab, ga, gb, k_new, v_new, _ = _in_projection(
        h.reshape(1, n, d), lw, tm=tm, tail=n, fuse_conv=False, pre_ln=pre_ln)

    def per_seq(x):
        return x.reshape(bsz, t, x.shape[-1])

    oa = _attention_sample(per_seq(q), k_cache.reshape(bsz, n_past, A_WIDTH), v_cache.reshape(bsz, n_past, A_WIDTH),
                           per_seq(k_new), per_seq(v_new), per_seq(za), rel_bias)
    xb_s = per_seq(xb)
    xp_tail = jnp.concatenate([conv_prev.astype(F32), xb_s], axis=1)[:, -(CONV_W - 1):]
    ob, s_new = _gated_delta(xb_s, per_seq(ab), per_seq(zb), conv_prev, s_prev, lw, tb=GDN_CHUNK, do_conv=True, pipelined=False)
    y = _out_projection(h.reshape(n, d), oa.reshape(n, A_WIDTH), ob.reshape(n, B_WIDTH),
                        ga.reshape(n, d), gb.reshape(n, d), lw, tm, pre_ln)
    new_k = per_seq(k_new).reshape(bsz, t, A_HEADS, A_HEAD_DIM)
    new_v = per_seq(v_new).reshape(bsz, t, A_HEADS, A_HEAD_DIM)
    return y.reshape(bsz, t, d), (new_k, new_v, xp_tail, s_new)


def kernel(x_prompt, x_sample, cache_attn_k, cache_attn_v, state_conv, state_gdn, ln0_g, ln0_b, w_in, rel_bias,
           conv_w, a_log, dt_bias, gdn_norm_w, w_branch_a, w_branch_b, w_out, ln_g, ln_b):
    bp, tp, d = x_prompt.shape
    bs, ts, _ = x_sample.shape
    hp, hs = x_prompt, x_sample
    outs_p, outs_s = [], []
    for l in range(DEPTH):
        lw = _layer_weights(l, w_in, conv_w, a_log, dt_bias, gdn_norm_w, w_branch_a, w_branch_b, w_out, ln_g, ln_b,
                            ln0_g, ln0_b)
        hp, st_p = _prompt_layer(hp, lw, rel_bias[l].astype(F32), tm=512, tb=256, pre_ln=(l == 0))
        hs, st_s = _sample_layer(hs, cache_attn_k[l], cache_attn_v[l], state_conv[l], state_gdn[l], lw,
                                 rel_bias[l].astype(F32), tm=256, pre_ln=(l == 0))
        outs_p.append(st_p)
        outs_s.append(st_s)

    def stacked(outs, i):
        return jnp.stack([o[i] for o in outs])

    return (hp, hs,
            stacked(outs_p, 0), stacked(outs_p, 1), stacked(outs_p, 2), stacked(outs_p, 3),
            stacked(outs_s, 0), stacked(outs_s, 1), stacked(outs_s, 2), stacked(outs_s, 3))
```

```python
import functools
import math

import numpy as np
import jax
import jax.numpy as jnp
from jax import lax
from jax.experimental import pallas as pl
from jax.experimental.pallas import tpu as pltpu

D_MODEL = 1024
DEPTH = 2
PAST_LEN = 4096
CHUNK = 64
BAND_CHUNKS = 8
ATT_WINDOW = BAND_CHUNKS * CHUNK
A_HEADS = 8
A_HEAD_DIM = 64
A_WIDTH = A_HEADS * A_HEAD_DIM
REL_MAX = 128
B_HEADS = 8
B_HEAD_DIM = 64
B_WIDTH = B_HEADS * B_HEAD_DIM
B_QKV = 3 * B_WIDTH
CONV_W = 4
GDN_CHUNK = 64
ALPHA = (2 * DEPTH) ** 0.25
LN_EPS = 1e-5
NORM_EPS = 1e-6

LANES = 128
SUBLANES = 8
HEAD_PAIRS = A_HEADS // 2
CONV_HIST = SUBLANES
NEG = -1e30
LOG2E = math.log2(math.e)
Q_SCALE = (A_HEAD_DIM ** -0.5) * LOG2E
V7X_VMEM_LIMIT = 56 * 1024 * 1024

BF16 = jnp.bfloat16
F32 = jnp.float32


def _mm(a, b):
    return jnp.dot(a.astype(BF16), b.astype(BF16), preferred_element_type=F32)


def _mm_nt(a, b):
    return lax.dot_general(a.astype(BF16), b.astype(BF16), (((1,), (1,)), ((), ())),
                           preferred_element_type=F32)


def _mm_tn(a, b):
    return lax.dot_general(a.astype(BF16), b.astype(BF16), (((0,), (0,)), ((), ())),
                           preferred_element_type=F32)


def _split3(x):
    x1 = x.astype(BF16)
    r1 = x - x1.astype(F32)
    x2 = r1.astype(BF16)
    x3 = (r1 - x2.astype(F32)).astype(BF16)
    return x1, x2, x3


def _mm_exact01(w01, x):
    w = w01.astype(BF16)
    return sum(jnp.dot(w, p, preferred_element_type=F32) for p in _split3(x))


def _mm_exact01_right(x, w01, pieces=3):
    w = w01.astype(BF16)
    return sum(jnp.dot(p, w, preferred_element_type=F32) for p in _split3(x)[:pieces])


def _sigmoid(x):
    return 0.5 + 0.5 * jnp.tanh(0.5 * x)


def _silu(x):
    h = 0.5 * x
    return h + h * jnp.tanh(h)


def _first_head_lanes():
    return lax.broadcasted_iota(jnp.int32, (1, LANES), 1) < B_HEAD_DIM


def _params(sem, est_bytes):
    limit = int(min(V7X_VMEM_LIMIT, max(32 * 1024 * 1024, est_bytes)))
    return pltpu.CompilerParams(dimension_semantics=sem, vmem_limit_bytes=limit)


def _const_spec(shape):
    nd = len(shape)
    return pl.BlockSpec(shape, lambda *_: (0,) * nd, pipeline_mode=pl.Buffered(1))


def _ln_rows(x, g, b):
    mu = jnp.mean(x, axis=-1, keepdims=True)
    xc = x - mu
    var = jnp.mean(xc * xc, axis=-1, keepdims=True)
    return xc * lax.rsqrt(var + LN_EPS) * g + b


CONV_COLS = 2 * LANES


def _l2norm_pair(xs, scale):
    first = _first_head_lanes()
    x2 = xs * xs
    zero = jnp.zeros_like(x2)
    s0 = jnp.sum(jnp.where(first, x2, zero), axis=-1, keepdims=True)
    s1 = jnp.sum(jnp.where(first, zero, x2), axis=-1, keepdims=True)
    inv = jnp.where(first, lax.rsqrt(s0 + NORM_EPS), lax.rsqrt(s1 + NORM_EPS))
    return xs * (inv * scale)


def _conv_qkv_cols(xbuf, convw_ref, rows, c0, width=CONV_COLS):
    cols = slice(c0, c0 + width)
    first_tap = CONV_HIST - (CONV_W - 1)
    acc = xbuf[first_tap:first_tap + rows, cols] * convw_ref[0:1, cols]
    for i in range(1, CONV_W):
        acc = acc + xbuf[first_tap + i:first_tap + i + rows, cols] * convw_ref[i:i + 1, cols]
    tail = xbuf[rows:rows + CONV_HIST, cols]
    xbuf[0:CONV_HIST, cols] = tail
    c = _silu(acc)
    if c0 >= 2 * B_WIDTH:
        return c
    scale = B_HEAD_DIM ** -0.5 if c0 < B_WIDTH else 1.0
    return jnp.concatenate([_l2norm_pair(c[:, j * LANES:(j + 1) * LANES], scale)
                            for j in range(width // LANES)], axis=1)


_COL_B = 4 * A_WIDTH
_COL_ZB = _COL_B + B_QKV
_COL_AB = _COL_ZB + B_WIDTH
_COL_G = _COL_AB + 2 * B_HEADS
IN_DIM = _COL_G + 2 * D_MODEL


def _inproj_kernel(x_ref, w_ref, convw_ref, lng_ref, lnb_ref,
                   q_ref, k_ref, v_ref, za_ref, xb_ref, zb_ref, ab_ref, ga_ref, gb_ref,
                   kt_ref, vt_ref, xtail_ref, xbuf, wg_scr, *, fuse_conv, pre_ln):
    tm = x_ref.shape[1]
    dot = functools.partial(jnp.dot, preferred_element_type=F32)

    @pl.when(pl.program_id(1) == 0)
    def _():
        wg_scr[...] = w_ref[:, _COL_G:IN_DIM]
        if fuse_conv:
            xbuf[0:CONV_HIST, :] = jnp.zeros((CONV_HIST, B_QKV), F32)

    x = x_ref[0]
    if pre_ln:
        x = _ln_rows(x, lng_ref[...], lnb_ref[...])
    x = x.astype(BF16)

    def attn_cols(j, half):
        cols = slice(half * CONV_COLS, (half + 1) * CONV_COLS)
        c0 = j * A_WIDTH + half * CONV_COLS
        r = dot(x, w_ref[:, c0:c0 + CONV_COLS])
        if j == 0:
            q_ref[0, :, cols] = (r * Q_SCALE).astype(BF16)
        elif j == 3:
            za_ref[0, :, cols] = r.astype(BF16)
        else:
            (k_ref, v_ref)[j - 1][0, :, cols] = r.astype(BF16)
            (kt_ref, vt_ref)[j - 1][0, :, cols] = r

    def gate_cols(j, part):
        cols = slice(part * CONV_COLS, (part + 1) * CONV_COLS)
        c0 = j * D_MODEL + part * CONV_COLS
        (ga_ref, gb_ref)[j][0, :, cols] = dot(x, wg_scr[:, c0:c0 + CONV_COLS]).astype(BF16)

    def zb_cols(half):
        cols = slice(half * CONV_COLS, (half + 1) * CONV_COLS)
        zb_ref[0, :, cols] = dot(x, w_ref[:, _COL_ZB + half * CONV_COLS:_COL_ZB + (half + 1) * CONV_COLS]).astype(BF16)

    others = ([functools.partial(attn_cols, j, half) for j in range(4) for half in range(2)]
              + [functools.partial(zb_cols, half) for half in range(2)]
              + [functools.partial(gate_cols, j, part) for j in range(2) for part in range(D_MODEL // CONV_COLS)])
    for step in range(B_QKV // CONV_COLS):
        c0 = step * CONV_COLS
        xb = dot(x, w_ref[:, _COL_B + c0:_COL_B + c0 + CONV_COLS])
        xtail_ref[0, :, c0:c0 + CONV_COLS] = xb[tm - CONV_HIST:tm, :]
        if fuse_conv:
            xbuf[CONV_HIST:CONV_HIST + tm, c0:c0 + CONV_COLS] = xb
        else:
            xb_ref[0, :, c0:c0 + CONV_COLS] = xb
        for piece in range(CONV_COLS // LANES):
            if others:
                others.pop(0)()
            if fuse_conv:
                cp = c0 + piece * LANES
                xb_ref[0, :, cp:cp + LANES] = _conv_qkv_cols(xbuf, convw_ref, tm, cp, LANES)
    for rest in others:
        rest()
    ab_ref[0] = dot(x, w_ref[:, _COL_AB:_COL_AB + LANES])


def _in_projection(h, w, *, tm, tail, fuse_conv, pre_ln):
    bsz, t, d = h.shape
    nt = t // tm
    tail_blocks = tail // tm

    def row(width):
        return pl.BlockSpec((1, tm, width), lambda b, i: (b, i, 0))

    tail_spec = pl.BlockSpec((1, tm, A_WIDTH),
                             lambda b, i: (b, jnp.maximum(i - (nt - tail_blocks), 0), 0))
    xtail_spec = pl.BlockSpec((1, CONV_HIST, B_QKV), lambda b, i: (b, 0, 0))
    out_shapes = [
        jax.ShapeDtypeStruct((bsz, t, A_WIDTH), BF16),
        jax.ShapeDtypeStruct((bsz, t, A_WIDTH), BF16),
        jax.ShapeDtypeStruct((bsz, t, A_WIDTH), BF16),
        jax.ShapeDtypeStruct((bsz, t, A_WIDTH), BF16),
        jax.ShapeDtypeStruct((bsz, t, B_QKV), F32),
        jax.ShapeDtypeStruct((bsz, t, B_WIDTH), BF16),
        jax.ShapeDtypeStruct((bsz, t, LANES), F32),
        jax.ShapeDtypeStruct((bsz, t, D_MODEL), BF16),
        jax.ShapeDtypeStruct((bsz, t, D_MODEL), BF16),
        jax.ShapeDtypeStruct((bsz, tail, A_WIDTH), F32),
        jax.ShapeDtypeStruct((bsz, tail, A_WIDTH), F32),
        jax.ShapeDtypeStruct((bsz, CONV_HIST, B_QKV), F32),
    ]
    out_specs = [row(A_WIDTH), row(A_WIDTH), row(A_WIDTH), row(A_WIDTH), row(B_QKV), row(B_WIDTH),
                 row(LANES), row(D_MODEL), row(D_MODEL), tail_spec, tail_spec, xtail_spec]
    weights = (w["w_in"], w["conv_w"], w["ln0_g"], w["ln0_b"])
    in_specs = [row(d)] + [_const_spec(x.shape) for x in weights]
    n_w = sum(int(np.prod(x.shape)) for x in weights)
    est = 2 * n_w + 2 * tm * (d * 4 + 3 * A_WIDTH * 2 + (2 * A_WIDTH + B_QKV + B_WIDTH + LANES
                                                        + 2 * D_MODEL + 2 * A_WIDTH) * 4)
    est += tm * (4 * A_WIDTH + 2 * D_MODEL + 4 * B_QKV) * 4 * 2
    return pl.pallas_call(
        functools.partial(_inproj_kernel, fuse_conv=fuse_conv, pre_ln=pre_ln),
        grid=(bsz, nt),
        in_specs=in_specs,
        out_specs=out_specs,
        out_shape=out_shapes,
        scratch_shapes=[pltpu.VMEM((CONV_HIST + tm, B_QKV), F32), pltpu.VMEM((d, 2 * D_MODEL), BF16)],
        compiler_params=_params(("parallel", "arbitrary"), est + (8 << 20)),
        name="in_projection",
    )(h, *weights)


def _attention_heads(jobs, bias_ref):
    first = _first_head_lanes()

    def scores(job, p):
        q, k = job[0], job[1]
        qp, kp = q[:, p * LANES:(p + 1) * LANES], k[:, p * LANES:(p + 1) * LANES]
        zero = jnp.zeros_like(qp)
        q2 = jnp.concatenate([jnp.where(first, qp, zero), jnp.where(first, zero, qp)], axis=0)
        return _mm_nt(q2, kp)

    def softmax(job, p, s):
        s = s + bias_ref[p]
        if job[4] is not None:
            s = jnp.where(job[4], s, NEG)
        e = jnp.exp2(s - jnp.max(s, axis=-1, keepdims=True))
        return e.astype(BF16), jnp.sum(e, axis=-1, keepdims=True)

    def output(job, p, e_den):
        _, _, v, za, _, o_ref = job
        nq = za.shape[0]
        sl = slice(p * LANES, (p + 1) * LANES)
        pv = _mm(e_den[0], v[:, sl]) / e_den[1]
        o = jnp.where(first, pv[0:nq], pv[nq:2 * nq])
        o_ref[:, sl] = (o * _silu(za[:, sl].astype(F32))).astype(o_ref.dtype)

    chains = [(job, p) for job in jobs for p in range(HEAD_PAIRS)]
    s_val, e_val = {}, {}
    for step in range(len(chains) + 2):
        if step < len(chains):
            s_val[step] = scores(*chains[step])
        if 0 <= step - 1 < len(chains):
            e_val[step - 1] = softmax(*chains[step - 1], s_val.pop(step - 1))
        if 0 <= step - 2 < len(chains):
            output(*chains[step - 2], e_val.pop(step - 2))


def _attn_prompt_kernel(q_ref, *refs, qblk, n_sub, n_kblk):
    k_refs, v_refs = refs[0:n_kblk], refs[n_kblk:2 * n_kblk]
    za_ref, bias_ref, o_ref = refs[2 * n_kblk:]
    m = pl.program_id(1)
    n_lead = n_kblk - n_sub
    band = (n_lead + 1) * qblk

    def jobs(masked):
        out = []
        for u in range(n_sub):
            k = jnp.concatenate([r[0] for r in k_refs[u:u + n_lead + 1]], axis=0)
            v = jnp.concatenate([r[0] for r in v_refs[u:u + n_lead + 1]], axis=0)
            valid = None
            if masked:
                col = lax.broadcasted_iota(jnp.int32, (1, band), 1) + (m * n_sub + u - n_lead) * qblk
                valid = col >= 0
            rows = slice(u * qblk, (u + 1) * qblk)
            out.append((q_ref[0, rows, :], k, v, za_ref[0, rows, :], valid, o_ref.at[0, rows, :]))
        return out

    @pl.when(m * n_sub >= n_lead)
    def _():
        _attention_heads(jobs(False), bias_ref)

    @pl.when(m * n_sub < n_lead)
    def _():
        _attention_heads(jobs(True), bias_ref)


def _rel_bias_table(rel_bias, nq, nk, n_before):
    h = rel_bias.shape[0]
    n_var = nq + REL_MAX
    assert nk - n_var >= 0 and n_before + nq == nk
    period = n_var + nq
    far = rel_bias[:, REL_MAX + CHUNK - 1:]
    n_low = n_var - (REL_MAX + CHUNK)
    g = jnp.concatenate([rel_bias[:, ::-1], jnp.repeat(rel_bias[:, :1], n_low, axis=1),
                         jnp.repeat(far, period - n_var, axis=1)], axis=1)
    skew = jnp.tile(g, (1, nq))[:, :nq * (period - 1)].reshape(h, nq, period - 1)[:, :, :n_var]
    const = jnp.broadcast_to(far[:, :, None], (h, nq, nk - n_var))
    return jnp.concatenate([const, skew], axis=2)


def _pair_tables(tab):
    h, nq, nk = tab.shape
    return (tab * LOG2E).astype(F32).reshape(h // 2, 2 * nq, nk)


def _attention_prompt(q, k, v, za, rel_bias, qblk=2 * CHUNK, n_sub=2):
    bsz, t, _ = q.shape
    n_lead = ATT_WINDOW // qblk
    n_kblk = n_lead + n_sub
    band = (n_lead + 1) * qblk
    i = np.arange(qblk)[:, None]
    j = np.arange(band)[None, :]
    qc, kc = i // CHUNK, j // CHUNK
    visible = (kc >= qc) & (kc <= qc + BAND_CHUNKS)
    tab = _rel_bias_table(rel_bias, qblk, band, ATT_WINDOW)
    tab = _pair_tables(jnp.where(jnp.asarray(visible)[None], tab, NEG))
    row = pl.BlockSpec((1, n_sub * qblk, A_WIDTH), lambda b, m: (b, m, 0))

    def key_block(jb):
        return pl.BlockSpec((1, qblk, A_WIDTH), lambda b, m: (b, jnp.maximum(m * n_sub - n_lead + jb, 0), 0))

    kv_specs = [key_block(jb) for jb in range(n_kblk)]
    est = tab.size * 4 + 2 * (2 * n_kblk + 2 * n_sub) * qblk * A_WIDTH * 2 + n_sub * 32 * qblk * band * 4
    return pl.pallas_call(
        functools.partial(_attn_prompt_kernel, qblk=qblk, n_sub=n_sub, n_kblk=n_kblk),
        grid=(bsz, t // (n_sub * qblk)),
        in_specs=[row] + kv_specs + kv_specs + [row, _const_spec(tab.shape)],
        out_specs=row,
        out_shape=jax.ShapeDtypeStruct((bsz, t, A_WIDTH), BF16),
        compiler_params=_params(("parallel", "arbitrary"), est + (8 << 20)),
        name="attention_prompt",
    )(q, *([k] * n_kblk), *([v] * n_kblk), za, tab)


def _attn_sample_kernel(q_ref, kc_ref, vc_ref, kn_ref, vn_ref, za_ref, bias_ref, o_ref, *, nk):
    n_seq, n_past, t = kc_ref.shape[0], kc_ref.shape[1], kn_ref.shape[1]
    zeros = jnp.zeros((nk - n_past - t, A_WIDTH), BF16)
    jobs = []
    for sq in range(n_seq):
        k = jnp.concatenate([kc_ref[sq].astype(BF16), kn_ref[sq].astype(BF16), zeros], axis=0)
        v = jnp.concatenate([vc_ref[sq].astype(BF16), vn_ref[sq].astype(BF16), zeros], axis=0)
        jobs.append((q_ref[sq], k, v, za_ref[sq], None, o_ref.at[sq]))
    _attention_heads(jobs, bias_ref)


def _attention_sample(q, k_cache, v_cache, k_new, v_new, za, rel_bias):
    bsz, t, _ = q.shape
    n_past = k_cache.shape[1]
    nk = -(-(n_past + t) // LANES) * LANES
    nq_tab = nk - n_past
    tab = _rel_bias_table(rel_bias, nq_tab, nk, n_past)[:, :t]
    tab = _pair_tables(jnp.where(jnp.asarray(np.arange(nk) < n_past + t)[None, None], tab, NEG))

    n_seq = max(s for s in (1, 2, 4) if bsz % s == 0)

    def blk(rows):
        return pl.BlockSpec((n_seq, rows, A_WIDTH), lambda b: (b, 0, 0))

    return pl.pallas_call(
        functools.partial(_attn_sample_kernel, nk=nk),
        grid=(bsz // n_seq,),
        in_specs=[blk(t), blk(n_past), blk(n_past), blk(t), blk(t), blk(t), _const_spec(tab.shape)],
        out_specs=blk(t),
        out_shape=jax.ShapeDtypeStruct((bsz, t, A_WIDTH), BF16),
        compiler_params=_params(("parallel",), 32 << 20),
        name="attention_sample",
    )(q, k_cache, v_cache, k_new, v_new, za, tab)


_KB, _QN, _KN, _VB, _KBE, _QE, _KD = range(7)


def _gdn_kernel(xb_ref, ab_ref, zb_ref, conv0_ref, s0_ref, convw_ref, arow_ref, dtrow_ref, normw_ref,
                o_ref, s_out_ref, xbuf, s_scr, nat_scr, gcx_scr, sdec_scr, aq_scr, bo_scr,
                *, tb, n_seq, n_valid, do_conv, n_par, pipelined):
    t_idx = pl.program_id(1)
    blk = GDN_CHUNK
    two = 2 * blk
    seq_chunks = tb // blk
    n_chunks = n_seq * seq_chunks
    rows_all = n_seq * tb

    assert not (pipelined and do_conv)
    rows_out = min(blk, n_valid)

    @pl.when(t_idx == 0)
    def _():
        s_scr[...] = s0_ref[...]
        if do_conv:
            xbuf[:, 0:CONV_HIST, :] = conv0_ref[...]
        if pipelined:
            aq_scr[...] = jnp.zeros(aq_scr.shape, aq_scr.dtype)
            bo_scr[...] = jnp.zeros(bo_scr.shape, bo_scr.dtype)
            sdec_scr[...] = jnp.ones(sdec_scr.shape, sdec_scr.dtype)

    def phase2(cidx):
        seq, r0 = cidx // seq_chunks, (cidx % seq_chunks) * blk
        for p in range(HEAD_PAIRS):
            s_old = s_scr[seq, p]
            tot = bo_scr[cidx, p] + _mm(aq_scr[cidx, p], s_old)
            dec = sdec_scr[cidx, 0:1, p * LANES:(p + 1) * LANES]
            s_scr[seq, p] = s_old * dec + tot[0:two]
            o = tot[two:2 * two]
            ms = jnp.sum(o * o, axis=-1, keepdims=True) * (1.0 / B_HEAD_DIM)
            on = o * lax.rsqrt(ms + NORM_EPS) * normw_ref[...]
            o_pair = (on[0:blk] + on[blk:two])[0:rows_out]
            zb = zb_ref[seq, r0:r0 + rows_out, p * LANES:(p + 1) * LANES].astype(F32)
            o_ref[seq, r0:r0 + rows_out, p * LANES:(p + 1) * LANES] = (o_pair * _silu(zb)).astype(o_ref.dtype)

    pending = list(range(n_chunks)) if pipelined else []

    def padded_seq(ref, sq):
        x = ref[sq]
        if n_valid == tb:
            return x
        return jnp.concatenate([x, jnp.zeros((tb - n_valid, x.shape[1]), x.dtype)], axis=0)

    def padded(ref):
        parts = [padded_seq(ref, sq) for sq in range(n_seq)]
        return parts[0] if n_seq == 1 else jnp.concatenate(parts, axis=0)

    if do_conv:
        for sq in range(n_seq):
            xbuf[sq, CONV_HIST:CONV_HIST + tb, :] = padded_seq(xb_ref, sq)
        c = jnp.concatenate(
            [jnp.concatenate([_conv_qkv_cols(xbuf.at[sq], convw_ref, tb, c0) for c0 in range(0, B_QKV, CONV_COLS)],
                             axis=1) for sq in range(n_seq)], axis=0)
    else:
        c = padded(xb_ref)
    qn, kn, v = c[:, 0:B_WIDTH], c[:, B_WIDTH:2 * B_WIDTH], c[:, 2 * B_WIDTH:]

    ab = padded(ab_ref)
    g = arow_ref[...] * jax.nn.softplus(ab + dtrow_ref[...])
    beta = _sigmoid(ab)
    if n_valid != tb:
        is_token = lax.broadcasted_iota(jnp.int32, (rows_all, 1), 0) % tb < n_valid
        g = jnp.where(is_token, g, 0.0)
        beta = jnp.where(is_token, beta, 0.0)
    lower = (lax.broadcasted_iota(jnp.int32, (blk, blk), 0) >= lax.broadcasted_iota(jnp.int32, (blk, blk), 1))
    gc = jnp.concatenate([_mm_exact01(lower, g[i * blk:(i + 1) * blk]) for i in range(n_chunks)],
                         axis=0)
    src = lax.broadcasted_iota(jnp.int32, (LANES, B_WIDTH), 0)
    head = lax.broadcasted_iota(jnp.int32, (LANES, B_WIDTH), 1) // B_HEAD_DIM
    gcx = _mm_exact01_right(gc, src == head)
    betax = _mm_exact01_right(beta, src == head + B_HEADS, pieces=2)
    glx = jnp.concatenate([jnp.broadcast_to(gcx[(i + 1) * blk - 1:(i + 1) * blk, :], (blk, B_WIDTH))
                           for i in range(n_chunks)], axis=0)
    egc = jnp.exp(gcx)
    kbeta = kn * betax
    nat_scr[_KB] = kbeta.astype(BF16)
    nat_scr[_QN] = qn.astype(BF16)
    nat_scr[_KN] = kn.astype(BF16)
    nat_scr[_VB] = (v * betax).astype(BF16)
    nat_scr[_KBE] = (kbeta * egc).astype(BF16)
    nat_scr[_QE] = (qn * egc).astype(BF16)
    nat_scr[_KD] = (kn * jnp.exp(glx - gcx)).astype(BF16)
    gcx_scr[...] = gcx
    sdec_rows = [jnp.exp(gcx[(i + 1) * blk - 1:(i + 1) * blk, :]) for i in range(n_chunks)]

    first = _first_head_lanes()
    r = lax.broadcasted_iota(jnp.int32, (two, two), 0)
    cc = lax.broadcasted_iota(jnp.int32, (two, two), 1)
    causal = ((r // blk) == (cc // blk)) & (r >= cc)
    strict = causal & (r > cc)
    eye = (r == cc).astype(F32)
    diag = (lax.broadcasted_iota(jnp.int32, (blk, LANES), 0)
            == lax.broadcasted_iota(jnp.int32, (blk, LANES), 1) % B_HEAD_DIM)

    def level_mask(bs):
        return ((r // bs) == (cc // bs)) & ((r % bs) >= bs // 2) & ((cc % bs) < bs // 2)

    def stacked(which, r0, p):
        x = nat_scr[which, pl.ds(r0, blk), p * LANES:(p + 1) * LANES]
        zero = jnp.zeros_like(x)
        return jnp.concatenate([jnp.where(first, x, zero), jnp.where(first, zero, x)], axis=0)

    def phase1(step, carry):
        chains = []
        for dc in range(n_par):
            cidx = step * n_par + dc
            for p in range(HEAD_PAIRS):
                chains.append({"cidx": cidx, "r0": cidx * blk, "p": p})
        for ch in chains:
            r0, p = ch["r0"], ch["p"]
            gt = gcx_scr[pl.ds(r0, blk), p * LANES:(p + 1) * LANES]
            gc_row = jnp.sum(jnp.where(diag, gt, 0.0), axis=0, keepdims=True)
            ch["decay"] = jnp.exp(jnp.where(causal, jnp.concatenate([gt, gt], axis=0) - gc_row, NEG))
            lhs = jnp.concatenate([stacked(_KB, r0, p), stacked(_QN, r0, p)], axis=0)
            gram = _mm_nt(lhs, stacked(_KN, r0, p))
            ch["m"] = jnp.where(strict, gram[0:two] * ch["decay"], 0.0)
            ch["qk"] = (gram[two:2 * two] * ch["decay"]).astype(BF16)
            ch["x"] = eye - jnp.where(level_mask(2), ch["m"], 0.0)
        bs = 4
        while bs <= blk:
            lm = level_mask(bs)
            for ch in chains:
                ch["t"] = _mm(jnp.where(lm, ch["m"], 0.0), ch["x"])
            if pending:
                phase2(pending.pop(0))
            for ch in chains:
                ch["x"] = ch["x"] - _mm(ch["x"], ch["t"])
            bs *= 2
        while pending:
            phase2(pending.pop(0))
        for ch in chains:
            r0, p = ch["r0"], ch["p"]
            rhs = jnp.concatenate([stacked(_VB, r0, p), stacked(_KBE, r0, p)], axis=1)
            ch["uw"] = _mm(ch["x"], rhs).astype(BF16)
        for ch in chains:
            r0, p = ch["r0"], ch["p"]
            kd_uw = _mm_tn(stacked(_KD, r0, p), ch["uw"])
            qk_uw = _mm(ch["qk"], ch["uw"])
            qeff = stacked(_QE, r0, p).astype(F32) - qk_uw[:, LANES:2 * LANES]
            aq_scr[ch["cidx"], p] = jnp.concatenate([-kd_uw[:, LANES:2 * LANES], qeff], axis=0).astype(BF16)
            bo_scr[ch["cidx"], p] = jnp.concatenate([kd_uw[:, 0:LANES], qk_uw[:, 0:LANES]], axis=0)
        return carry

    for step in range(n_chunks // n_par):
        phase1(step, 0)
    for i in range(n_chunks):
        sdec_scr[i] = jnp.broadcast_to(sdec_rows[i], (SUBLANES, B_WIDTH))
    if not pipelined:
        for i in range(n_chunks):
            phase2(i)

    @pl.when(t_idx == pl.num_programs(1) - 1)
    def _():
        s_out_ref[...] = s_scr[...]


def _pair_state(s):
    bsz = s.shape[0]
    s = s.reshape(bsz, HEAD_PAIRS, 2, B_HEAD_DIM, B_HEAD_DIM)
    z = jnp.zeros_like(s[:, :, 0])
    top = jnp.concatenate([s[:, :, 0], z], axis=-1)
    bot = jnp.concatenate([z, s[:, :, 1]], axis=-1)
    return jnp.concatenate([top, bot], axis=-2)


def _unpair_state(sp):
    d = B_HEAD_DIM
    s = jnp.stack([sp[:, :, 0:d, 0:d], sp[:, :, d:2 * d, d:2 * d]], axis=2)
    return s.reshape(sp.shape[0], B_HEADS, d, d)


def _gated_delta(xb, ab, zb, conv_prev, s_prev, lw, *, tb, do_conv, pipelined):
    bsz, t, _ = xb.shape
    n_valid = min(t, tb)
    assert t % n_valid == 0 and tb % GDN_CHUNK == 0 and (n_valid == tb or t == n_valid)
    conv0 = jnp.concatenate([jnp.zeros((bsz, CONV_HIST - (CONV_W - 1), B_QKV), F32), conv_prev.astype(F32)], axis=1)
    s0 = _pair_state(s_prev.astype(F32))

    nt = t // n_valid
    n_seq = max(s for s in (1, 2, 4) if bsz % s == 0) if (nt == 1 and tb == GDN_CHUNK) else 1

    def row(width, lag):
        if not pipelined:
            return pl.BlockSpec((n_seq, n_valid, width), lambda b, i: (b, i, 0))
        if lag:
            return pl.BlockSpec((n_seq, n_valid, width), lambda b, i: (b, jnp.maximum(i - 1, 0), 0))
        return pl.BlockSpec((n_seq, n_valid, width), lambda b, i: (b, jnp.minimum(i, nt - 1), 0))

    def per_batch(shape):
        nd = len(shape)
        return pl.BlockSpec((n_seq,) + shape, lambda b, i: (b,) + (0,) * nd)

    n_chunks = n_seq * tb // GDN_CHUNK
    n_par = max(p for p in (1, 2, 4) if n_chunks % p == 0)
    per_chain = (n_chunks, HEAD_PAIRS, 2 * LANES, LANES)
    est = 2 * n_valid * (B_QKV + LANES + B_WIDTH) * 4 + (tb + CONV_HIST) * B_QKV * 4 + 6 * HEAD_PAIRS * LANES * LANES * 4
    rows_all = n_seq * tb
    est += rows_all * B_WIDTH * (7 * 2 + 4 + 16 * 4) + n_chunks * HEAD_PAIRS * LANES * LANES * 8 + rows_all * rows_all * 8
    o, s_out = pl.pallas_call(
        functools.partial(_gdn_kernel, tb=tb, n_seq=n_seq, n_valid=n_valid, do_conv=do_conv, n_par=n_par,
                          pipelined=pipelined),
        grid=(bsz // n_seq, nt + 1 if pipelined else nt),
        in_specs=[row(B_QKV, False), row(LANES, False), row(B_WIDTH, True), per_batch((CONV_HIST, B_QKV)),
                  per_batch((HEAD_PAIRS, LANES, LANES)),
                  _const_spec((CONV_W, B_QKV)), _const_spec((1, LANES)), _const_spec((1, LANES)),
                  _const_spec((1, LANES))],
        out_specs=[row(B_WIDTH, True), per_batch((HEAD_PAIRS, LANES, LANES))],
        out_shape=[jax.ShapeDtypeStruct((bsz, t, B_WIDTH), BF16),
                   jax.ShapeDtypeStruct((bsz, HEAD_PAIRS, LANES, LANES), F32)],
        scratch_shapes=[pltpu.VMEM((n_seq, CONV_HIST + tb, B_QKV), F32),
                        pltpu.VMEM((n_seq, HEAD_PAIRS, LANES, LANES), F32),
                        pltpu.VMEM((7, n_seq * tb, B_WIDTH), BF16),
                        pltpu.VMEM((n_seq * tb, B_WIDTH), F32),
                        pltpu.VMEM((n_chunks, SUBLANES, B_WIDTH), F32),
                        pltpu.VMEM(per_chain, BF16), pltpu.VMEM(per_chain, F32)],
        compiler_params=_params(("parallel", "arbitrary"), est + (16 << 20)),
        name="gated_delta",
    )(xb, ab, zb, conv0, s0, lw["conv_w"], lw["a_row"], lw["dt_row"], lw["normw_row"])
    return o, _unpair_state(s_out)


def _out_kernel(x_ref, a_ref, b_ref, ga_ref, gb_ref, wa_ref, wb_ref, wo_ref, g_ref, beta_ref, lng_ref, lnb_ref,
                y_ref, *, pre_ln, n_split):
    dot = functools.partial(jnp.dot, preferred_element_type=F32)
    rows = x_ref.shape[0] // n_split

    def project(u):
        sl = slice(u * rows, (u + 1) * rows)
        ya = dot(a_ref[sl, :], wa_ref[...])
        yb = dot(b_ref[sl, :], wb_ref[...])
        mix = _sigmoid(ga_ref[sl, :].astype(F32)) * ya + _sigmoid(gb_ref[sl, :].astype(F32)) * yb
        return dot(mix.astype(BF16), wo_ref[...])

    def finish(u, out):
        sl = slice(u * rows, (u + 1) * rows)
        x = x_ref[sl, :]
        if pre_ln:
            x = _ln_rows(x, lng_ref[...], lnb_ref[...])
        y_ref[sl, :] = _ln_rows(ALPHA * x + out, g_ref[...], beta_ref[...])

    outs = {}
    for step in range(n_split + 1):
        if step < n_split:
            outs[step] = project(step)
        if step >= 1:
            finish(step - 1, outs.pop(step - 1))


def _out_projection(x, a, b, ga, gb, w, tm, pre_ln):
    n, d = x.shape

    def row(width):
        return pl.BlockSpec((tm, width), lambda i: (i, 0))

    weights = (w["w_branch_a"], w["w_branch_b"], w["w_out"], w["ln_g"], w["ln_b"], w["ln0_g"], w["ln0_b"])
    est = 2 * tm * (4 * d * 4 + 2 * A_WIDTH * 2) + 2 * (2 * A_WIDTH * d + d * d) + 6 * tm * d * 4
    return pl.pallas_call(
        functools.partial(_out_kernel, pre_ln=pre_ln, n_split=max(1, tm // LANES)),
        grid=(n // tm,),
        in_specs=[row(d), row(A_WIDTH), row(B_WIDTH), row(d), row(d)] + [_const_spec(x.shape) for x in weights],
        out_specs=row(d),
        out_shape=jax.ShapeDtypeStruct((n, d), F32),
        compiler_params=_params(("parallel",), est + (8 << 20)),
        name="out_projection",
    )(x, a, b, ga, gb, *weights)


def _layer_weights(l, w_in, conv_w, a_log, dt_bias, gdn_norm_w, w_branch_a, w_branch_b, w_out, ln_g, ln_b,
                   ln0_g, ln0_b):
    pad_h = (0, LANES - B_HEADS)
    return {
        "w_in": w_in[l].astype(BF16),
        "conv_w": conv_w[l].astype(F32),
        "a_row": jnp.pad(-jnp.exp(a_log[l].astype(F32)), pad_h).reshape(1, LANES),
        "dt_row": jnp.pad(dt_bias[l].astype(F32), pad_h).reshape(1, LANES),
        "normw_row": jnp.tile(gdn_norm_w[l].astype(F32), 2).reshape(1, LANES),
        "w_branch_a": w_branch_a[l].astype(BF16), "w_branch_b": w_branch_b[l].astype(BF16),
        "w_out": w_out[l].astype(BF16),
        "ln_g": ln_g[l].astype(F32).reshape(1, D_MODEL), "ln_b": ln_b[l].astype(F32).reshape(1, D_MODEL),
        "ln0_g": ln0_g.astype(F32).reshape(1, D_MODEL), "ln0_b": ln0_b.astype(F32).reshape(1, D_MODEL),
    }


def _prompt_layer(h, lw, rel_bias, *, tm, tb, pre_ln):
    bsz, t, d = h.shape
    keep = min(ATT_WINDOW, t)
    q, k, v, za, xb, zb, ab, ga, gb, k_tail, v_tail, x_tail = _in_projection(
        h, lw, tm=tm, tail=keep, fuse_conv=True, pre_ln=pre_ln)
    oa = _attention_prompt(q, k, v, za, rel_bias)
    conv_zero = jnp.zeros((bsz, CONV_W - 1, B_QKV), F32)
    s_zero = jnp.zeros((bsz, B_HEADS, B_HEAD_DIM, B_HEAD_DIM), F32)
    ob, s_new = _gated_delta(xb, ab, zb, conv_zero, s_zero, lw, tb=tb, do_conv=False, pipelined=True)
    y = _out_projection(h.reshape(bsz * t, d), oa.reshape(bsz * t, A_WIDTH), ob.reshape(bsz * t, B_WIDTH),
                        ga.reshape(bsz * t, d), gb.reshape(bsz * t, d), lw, tm, pre_ln)
    new_k = k_tail.reshape(bsz, keep, A_HEADS, A_HEAD_DIM)
    new_v = v_tail.reshape(bsz, keep, A_HEADS, A_HEAD_DIM)
    return y.reshape(bsz, t, d), (new_k, new_v, x_tail[:, CONV_HIST - (CONV_W - 1):], s_new)


def _sample_layer(h, k_cache, v_cache, conv_prev, s_prev, lw, rel_bias, *, tm, pre_ln):
    bsz, t, d = h.shape
    n = bsz * t
    n_past = k_cache.shape[1]
    q, _, _, za, xb, zb, ab, ga, gb, k_new, v_new, _ = _in_projection(
        h.reshape(1, n, d), lw, tm=tm, tail=n, fuse_conv=False, pre_ln=pre_ln)

    def per_seq(x):
        return x.reshape(bsz, t, x.shape[-1])

    oa = _attention_sample(per_seq(q), k_cache.reshape(bsz, n_past, A_WIDTH), v_cache.reshape(bsz, n_past, A_WIDTH),
                           per_seq(k_new), per_seq(v_new), per_seq(za), rel_bias)
    xb_s = per_seq(xb)
    xp_tail = jnp.concatenate([conv_prev.astype(F32), xb_s], axis=1)[:, -(CONV_W - 1):]
    ob, s_new = _gated_delta(xb_s, per_seq(ab), per_seq(zb), conv_prev, s_prev, lw, tb=GDN_CHUNK, do_conv=True, pipelined=False)
    y = _out_projection(h.reshape(n, d), oa.reshape(n, A_WIDTH), ob.reshape(n, B_WIDTH),
                        ga.reshape(n, d), gb.reshape(n, d), lw, tm, pre_ln)
    new_k = per_seq(k_new).reshape(bsz, t, A_HEADS, A_HEAD_DIM)
    new_v = per_seq(v_new).reshape(bsz, t, A_HEADS, A_HEAD_DIM)
    return y.reshape(bsz, t, d), (new_k, new_v, xp_tail, s_new)


def kernel(x_prompt, x_sample, cache_attn_k, cache_attn_v, state_conv, state_gdn, ln0_g, ln0_b, w_in, rel_bias,
           conv_w, a_log, dt_bias, gdn_norm_w, w_branch_a, w_branch_b, w_out, ln_g, ln_b):
    bp, tp, d = x_prompt.shape
    bs, ts, _ = x_sample.shape
    hp, hs = x_prompt, x_sample
    outs_p, outs_s = [], []
    for l in range(DEPTH):
        lw = _layer_weights(l, w_in, conv_w, a_log, dt_bias, gdn_norm_w, w_branch_a, w_branch_b, w_out, ln_g, ln_b,
                            ln0_g, ln0_b)
        hp, st_p = _prompt_layer(hp, lw, rel_bias[l].astype(F32), tm=512, tb=256, pre_ln=(l == 0))
        hs, st_s = _sample_layer(hs, cache_attn_k[l], cache_attn_v[l], state_conv[l], state_gdn[l], lw,
                                 rel_bias[l].astype(F32), tm=256, pre_ln=(l == 0))
        outs_p.append(st_p)
        outs_s.append(st_s)

    def stacked(outs, i):
        return jnp.stack([o[i] for o in outs])

    return (hp, hs,
            stacked(outs_p, 0), stacked(outs_p, 1), stacked(outs_p, 2), stacked(outs_p, 3),
            stacked(outs_s, 0), stacked(outs_s, 1), stacked(outs_s, 2), stacked(outs_s, 3))
```

```python
import functools
import math

import numpy as np
import jax
import jax.numpy as jnp
from jax import lax
from jax.experimental import pallas as pl
from jax.experimental.pallas import tpu as pltpu

D_MODEL = 1024
DEPTH = 2
PAST_LEN = 4096
CHUNK = 64
BAND_CHUNKS = 8
ATT_WINDOW = BAND_CHUNKS * CHUNK
A_HEADS = 8
A_HEAD_DIM = 64
A_WIDTH = A_HEADS * A_HEAD_DIM
REL_MAX = 128
B_HEADS = 8
B_HEAD_DIM = 64
B_WIDTH = B_HEADS * B_HEAD_DIM
B_QKV = 3 * B_WIDTH
CONV_W = 4
GDN_CHUNK = 64
ALPHA = (2 * DEPTH) ** 0.25
LN_EPS = 1e-5
NORM_EPS = 1e-6

LANES = 128
SUBLANES = 8
HEAD_PAIRS = A_HEADS // 2
CONV_HIST = SUBLANES
NEG = -1e30
LOG2E = math.log2(math.e)
Q_SCALE = (A_HEAD_DIM ** -0.5) * LOG2E
V7X_VMEM_LIMIT = 56 * 1024 * 1024

BF16 = jnp.bfloat16
F32 = jnp.float32


def _mm(a, b):
    return jnp.dot(a.astype(BF16), b.astype(BF16), preferred_element_type=F32)


def _mm_nt(a, b):
    return lax.dot_general(a.astype(BF16), b.astype(BF16), (((1,), (1,)), ((), ())),
                           preferred_element_type=F32)


def _mm_tn(a, b):
    return lax.dot_general(a.astype(BF16), b.astype(BF16), (((0,), (0,)), ((), ())),
                           preferred_element_type=F32)


def _split3(x):
    x1 = x.astype(BF16)
    r1 = x - x1.astype(F32)
    x2 = r1.astype(BF16)
    x3 = (r1 - x2.astype(F32)).astype(BF16)
    return x1, x2, x3


def _mm_exact01(w01, x):
    w = w01.astype(BF16)
    return sum(jnp.dot(w, p, preferred_element_type=F32) for p in _split3(x))


def _mm_exact01_right(x, w01, pieces=3):
    w = w01.astype(BF16)
    return sum(jnp.dot(p, w, preferred_element_type=F32) for p in _split3(x)[:pieces])


def _sigmoid(x):
    return 0.5 + 0.5 * jnp.tanh(0.5 * x)


def _silu(x):
    h = 0.5 * x
    return h + h * jnp.tanh(h)


def _first_head_lanes():
    return lax.broadcasted_iota(jnp.int32, (1, LANES), 1) < B_HEAD_DIM


def _params(sem, est_bytes):
    limit = int(min(V7X_VMEM_LIMIT, max(32 * 1024 * 1024, est_bytes)))
    return pltpu.CompilerParams(dimension_semantics=sem, vmem_limit_bytes=limit)


def _const_spec(shape):
    nd = len(shape)
    return pl.BlockSpec(shape, lambda *_: (0,) * nd, pipeline_mode=pl.Buffered(1))


def _ln_rows(x, g, b):
    mu = jnp.mean(x, axis=-1, keepdims=True)
    xc = x - mu
    var = jnp.mean(xc * xc, axis=-1, keepdims=True)
    return xc * lax.rsqrt(var + LN_EPS) * g + b


CONV_COLS = 2 * LANES


def _l2norm_pair(xs, scale):
    first = _first_head_lanes()
    x2 = xs * xs
    zero = jnp.zeros_like(x2)
    s0 = jnp.sum(jnp.where(first, x2, zero), axis=-1, keepdims=True)
    s1 = jnp.sum(jnp.where(first, zero, x2), axis=-1, keepdims=True)
    inv = jnp.where(first, lax.rsqrt(s0 + NORM_EPS), lax.rsqrt(s1 + NORM_EPS))
    return xs * (inv * scale)


def _conv_qkv_cols(xbuf, convw_ref, rows, c0, width=CONV_COLS):
    cols = slice(c0, c0 + width)
    first_tap = CONV_HIST - (CONV_W - 1)
    acc = xbuf[first_tap:first_tap + rows, cols] * convw_ref[0:1, cols]
    for i in range(1, CONV_W):
        acc = acc + xbuf[first_tap + i:first_tap + i + rows, cols] * convw_ref[i:i + 1, cols]
    tail = xbuf[rows:rows + CONV_HIST, cols]
    xbuf[0:CONV_HIST, cols] = tail
    c = _silu(acc)
    if c0 >= 2 * B_WIDTH:
        return c
    scale = B_HEAD_DIM ** -0.5 if c0 < B_WIDTH else 1.0
    return jnp.concatenate([_l2norm_pair(c[:, j * LANES:(j + 1) * LANES], scale)
                            for j in range(width // LANES)], axis=1)


_COL_B = 4 * A_WIDTH
_COL_ZB = _COL_B + B_QKV
_COL_AB = _COL_ZB + B_WIDTH
_COL_G = _COL_AB + 2 * B_HEADS
IN_DIM = _COL_G + 2 * D_MODEL


def _inproj_kernel(x_ref, w_ref, convw_ref, lng_ref, lnb_ref,
                   q_ref, k_ref, v_ref, za_ref, xb_ref, zb_ref, ab_ref, ga_ref, gb_ref,
                   kt_ref, vt_ref, xtail_ref, xbuf, wg_scr, *, fuse_conv, pre_ln):
    tm = x_ref.shape[1]
    dot = functools.partial(jnp.dot, preferred_element_type=F32)

    @pl.when(pl.program_id(1) == 0)
    def _():
        wg_scr[...] = w_ref[:, _COL_G:IN_DIM]
        if fuse_conv:
            xbuf[0:CONV_HIST, :] = jnp.zeros((CONV_HIST, B_QKV), F32)

    x = x_ref[0]
    if pre_ln:
        x = _ln_rows(x, lng_ref[...], lnb_ref[...])
    x = x.astype(BF16)

    def attn_cols(j, half):
        cols = slice(half * CONV_COLS, (half + 1) * CONV_COLS)
        c0 = j * A_WIDTH + half * CONV_COLS
        r = dot(x, w_ref[:, c0:c0 + CONV_COLS])
        if j == 0:
            q_ref[0, :, cols] = (r * Q_SCALE).astype(BF16)
        elif j == 3:
            za_ref[0, :, cols] = r.astype(BF16)
        else:
            (k_ref, v_ref)[j - 1][0, :, cols] = r.astype(BF16)
            (kt_ref, vt_ref)[j - 1][0, :, cols] = r

    def gate_cols(j, part):
        cols = slice(part * CONV_COLS, (part + 1) * CONV_COLS)
        c0 = j * D_MODEL + part * CONV_COLS
        (ga_ref, gb_ref)[j][0, :, cols] = dot(x, wg_scr[:, c0:c0 + CONV_COLS]).astype(BF16)

    def zb_cols(half):
        cols = slice(half * CONV_COLS, (half + 1) * CONV_COLS)
        zb_ref[0, :, cols] = dot(x, w_ref[:, _COL_ZB + half * CONV_COLS:_COL_ZB + (half + 1) * CONV_COLS]).astype(BF16)

    others = ([functools.partial(attn_cols, j, half) for j in range(4) for half in range(2)]
              + [functools.partial(zb_cols, half) for half in range(2)]
              + [functools.partial(gate_cols, j, part) for j in range(2) for part in range(D_MODEL // CONV_COLS)])
    for step in range(B_QKV // CONV_COLS):
        c0 = step * CONV_COLS
        xb = dot(x, w_ref[:, _COL_B + c0:_COL_B + c0 + CONV_COLS])
        xtail_ref[0, :, c0:c0 + CONV_COLS] = xb[tm - CONV_HIST:tm, :]
        if fuse_conv:
            xbuf[CONV_HIST:CONV_HIST + tm, c0:c0 + CONV_COLS] = xb
        else:
            xb_ref[0, :, c0:c0 + CONV_COLS] = xb
        for piece in range(CONV_COLS // LANES):
            if others:
                others.pop(0)()
            if fuse_conv:
                cp = c0 + piece * LANES
                xb_ref[0, :, cp:cp + LANES] = _conv_qkv_cols(xbuf, convw_ref, tm, cp, LANES)
    for rest in others:
        rest()
    ab_ref[0] = dot(x, w_ref[:, _COL_AB:_COL_AB + LANES])


def _in_projection(h, w, *, tm, tail, fuse_conv, pre_ln):
    bsz, t, d = h.shape
    nt = t // tm
    tail_blocks = tail // tm

    def row(width):
        return pl.BlockSpec((1, tm, width), lambda b, i: (b, i, 0))

    tail_spec = pl.BlockSpec((1, tm, A_WIDTH),
                             lambda b, i: (b, jnp.maximum(i - (nt - tail_blocks), 0), 0))
    xtail_spec = pl.BlockSpec((1, CONV_HIST, B_QKV), lambda b, i: (b, 0, 0))
    out_shapes = [
        jax.ShapeDtypeStruct((bsz, t, A_WIDTH), BF16),
        jax.ShapeDtypeStruct((bsz, t, A_WIDTH), BF16),
        jax.ShapeDtypeStruct((bsz, t, A_WIDTH), BF16),
        jax.ShapeDtypeStruct((bsz, t, A_WIDTH), BF16),
        jax.ShapeDtypeStruct((bsz, t, B_QKV), F32),
        jax.ShapeDtypeStruct((bsz, t, B_WIDTH), BF16),
        jax.ShapeDtypeStruct((bsz, t, LANES), F32),
        jax.ShapeDtypeStruct((bsz, t, D_MODEL), BF16),
        jax.ShapeDtypeStruct((bsz, t, D_MODEL), BF16),
        jax.ShapeDtypeStruct((bsz, tail, A_WIDTH), F32),
        jax.ShapeDtypeStruct((bsz, tail, A_WIDTH), F32),
        jax.ShapeDtypeStruct((bsz, CONV_HIST, B_QKV), F32),
    ]
    out_specs = [row(A_WIDTH), row(A_WIDTH), row(A_WIDTH), row(A_WIDTH), row(B_QKV), row(B_WIDTH),
                 row(LANES), row(D_MODEL), row(D_MODEL), tail_spec, tail_spec, xtail_spec]
    layer = w["layer"]
    weights = (w["w_in"], w["conv_w"], w["ln0_g"], w["ln0_b"])
    w_spec = pl.BlockSpec((None,) + w["w_in"].shape[1:], lambda *_: (layer, 0, 0), pipeline_mode=pl.Buffered(1))
    in_specs = [row(d), w_spec] + [_const_spec(x.shape) for x in weights[1:]]
    n_w = sum(int(np.prod(x.shape[-2:])) for x in weights)
    est = 2 * n_w + 2 * tm * (d * 4 + 3 * A_WIDTH * 2 + (2 * A_WIDTH + B_QKV + B_WIDTH + LANES
                                                        + 2 * D_MODEL + 2 * A_WIDTH) * 4)
    est += tm * (4 * A_WIDTH + 2 * D_MODEL + 4 * B_QKV) * 4 * 2
    return pl.pallas_call(
        functools.partial(_inproj_kernel, fuse_conv=fuse_conv, pre_ln=pre_ln),
        grid=(bsz, nt),
        in_specs=in_specs,
        out_specs=out_specs,
        out_shape=out_shapes,
        scratch_shapes=[pltpu.VMEM((CONV_HIST + tm, B_QKV), F32), pltpu.VMEM((d, 2 * D_MODEL), BF16)],
        compiler_params=_params(("parallel", "arbitrary"), est + (8 << 20)),
        name="in_projection",
    )(h, *weights)


def _attention_heads(jobs, bias_ref):
    first = _first_head_lanes()

    def scores(job, p):
        q, k = job[0], job[1]
        qp, kp = q[:, p * LANES:(p + 1) * LANES], k[:, p * LANES:(p + 1) * LANES]
        zero = jnp.zeros_like(qp)
        q2 = jnp.concatenate([jnp.where(first, qp, zero), jnp.where(first, zero, qp)], axis=0)
        return _mm_nt(q2, kp)

    def softmax(job, p, s):
        s = s + bias_ref[p]
        if job[4] is not None:
            s = jnp.where(job[4], s, NEG)
        e = jnp.exp2(s - jnp.max(s, axis=-1, keepdims=True))
        return e.astype(BF16), jnp.sum(e, axis=-1, keepdims=True)

    def output(job, p, e_den):
        _, _, v, za, _, o_ref = job
        nq = za.shape[0]
        sl = slice(p * LANES, (p + 1) * LANES)
        pv = _mm(e_den[0], v[:, sl]) / e_den[1]
        o = jnp.where(first, pv[0:nq], pv[nq:2 * nq])
        o_ref[:, sl] = (o * _silu(za[:, sl].astype(F32))).astype(o_ref.dtype)

    chains = [(job, p) for job in jobs for p in range(HEAD_PAIRS)]
    s_val, e_val = {}, {}
    for step in range(len(chains) + 2):
        if step < len(chains):
            s_val[step] = scores(*chains[step])
        if 0 <= step - 1 < len(chains):
            e_val[step - 1] = softmax(*chains[step - 1], s_val.pop(step - 1))
        if 0 <= step - 2 < len(chains):
            output(*chains[step - 2], e_val.pop(step - 2))


def _attn_prompt_kernel(q_ref, *refs, qblk, n_sub, n_kblk):
    k_refs, v_refs = refs[0:n_kblk], refs[n_kblk:2 * n_kblk]
    za_ref, bias_ref, o_ref = refs[2 * n_kblk:]
    m = pl.program_id(1)
    n_lead = n_kblk - n_sub
    band = (n_lead + 1) * qblk

    def jobs(masked):
        out = []
        for u in range(n_sub):
            k = jnp.concatenate([r[0] for r in k_refs[u:u + n_lead + 1]], axis=0)
            v = jnp.concatenate([r[0] for r in v_refs[u:u + n_lead + 1]], axis=0)
            valid = None
            if masked:
                col = lax.broadcasted_iota(jnp.int32, (1, band), 1) + (m * n_sub + u - n_lead) * qblk
                valid = col >= 0
            rows = slice(u * qblk, (u + 1) * qblk)
            out.append((q_ref[0, rows, :], k, v, za_ref[0, rows, :], valid, o_ref.at[0, rows, :]))
        return out

    @pl.when(m * n_sub >= n_lead)
    def _():
        _attention_heads(jobs(False), bias_ref)

    @pl.when(m * n_sub < n_lead)
    def _():
        _attention_heads(jobs(True), bias_ref)


def _rel_bias_table(rel_bias, nq, nk, n_before):
    h = rel_bias.shape[0]
    n_var = nq + REL_MAX
    assert nk - n_var >= 0 and n_before + nq == nk
    period = n_var + nq
    far = rel_bias[:, REL_MAX + CHUNK - 1:]
    n_low = n_var - (REL_MAX + CHUNK)
    g = jnp.concatenate([rel_bias[:, ::-1], jnp.repeat(rel_bias[:, :1], n_low, axis=1),
                         jnp.repeat(far, period - n_var, axis=1)], axis=1)
    skew = jnp.tile(g, (1, nq))[:, :nq * (period - 1)].reshape(h, nq, period - 1)[:, :, :n_var]
    const = jnp.broadcast_to(far[:, :, None], (h, nq, nk - n_var))
    return jnp.concatenate([const, skew], axis=2)


def _pair_tables(tab):
    h, nq, nk = tab.shape
    return (tab * LOG2E).astype(F32).reshape(h // 2, 2 * nq, nk)


def _attention_prompt(q, k, v, za, rel_bias, qblk=2 * CHUNK, n_sub=4):
    bsz, t, _ = q.shape
    n_lead = ATT_WINDOW // qblk
    n_kblk = n_lead + n_sub
    band = (n_lead + 1) * qblk
    i = np.arange(qblk)[:, None]
    j = np.arange(band)[None, :]
    qc, kc = i // CHUNK, j // CHUNK
    visible = (kc >= qc) & (kc <= qc + BAND_CHUNKS)
    tab = _rel_bias_table(rel_bias, qblk, band, ATT_WINDOW)
    tab = _pair_tables(jnp.where(jnp.asarray(visible)[None], tab, NEG))
    row = pl.BlockSpec((1, n_sub * qblk, A_WIDTH), lambda b, m: (b, m, 0))

    def key_block(jb):
        return pl.BlockSpec((1, qblk, A_WIDTH), lambda b, m: (b, jnp.maximum(m * n_sub - n_lead + jb, 0), 0))

    kv_specs = [key_block(jb) for jb in range(n_kblk)]
    est = tab.size * 4 + 2 * (2 * n_kblk + 2 * n_sub) * qblk * A_WIDTH * 2 + n_sub * 32 * qblk * band * 4
    return pl.pallas_call(
        functools.partial(_attn_prompt_kernel, qblk=qblk, n_sub=n_sub, n_kblk=n_kblk),
        grid=(bsz, t // (n_sub * qblk)),
        in_specs=[row] + kv_specs + kv_specs + [row, _const_spec(tab.shape)],
        out_specs=row,
        out_shape=jax.ShapeDtypeStruct((bsz, t, A_WIDTH), BF16),
        compiler_params=_params(("parallel", "arbitrary"), est + (8 << 20)),
        name="attention_prompt",
    )(q, *([k] * n_kblk), *([v] * n_kblk), za, tab)


def _attn_sample_kernel(q_ref, kc_ref, vc_ref, kn_ref, vn_ref, za_ref, bias_ref, o_ref, *, nk):
    n_seq, n_past, t = kc_ref.shape[0], kc_ref.shape[1], kn_ref.shape[1]
    zeros = jnp.zeros((nk - n_past - t, A_WIDTH), BF16)
    jobs = []
    for sq in range(n_seq):
        k = jnp.concatenate([kc_ref[sq].astype(BF16), kn_ref[sq].astype(BF16), zeros], axis=0)
        v = jnp.concatenate([vc_ref[sq].astype(BF16), vn_ref[sq].astype(BF16), zeros], axis=0)
        jobs.append((q_ref[sq], k, v, za_ref[sq], None, o_ref.at[sq]))
    _attention_heads(jobs, bias_ref)


def _attention_sample(q, k_cache, v_cache, layer, k_new, v_new, za, rel_bias):
    bsz, t, _ = q.shape
    n_past = k_cache.shape[2]
    nk = -(-(n_past + t) // LANES) * LANES
    nq_tab = nk - n_past
    tab = _rel_bias_table(rel_bias, nq_tab, nk, n_past)[:, :t]
    tab = _pair_tables(jnp.where(jnp.asarray(np.arange(nk) < n_past + t)[None, None], tab, NEG))

    n_seq = max(s for s in (1, 2, 4) if bsz % s == 0)

    def blk(rows):
        return pl.BlockSpec((n_seq, rows, A_WIDTH), lambda b: (b, 0, 0))

    cache = pl.BlockSpec((None, n_seq, n_past, A_WIDTH), lambda b: (layer, b, 0, 0))
    return pl.pallas_call(
        functools.partial(_attn_sample_kernel, nk=nk),
        grid=(bsz // n_seq,),
        in_specs=[blk(t), cache, cache, blk(t), blk(t), blk(t), _const_spec(tab.shape)],
        out_specs=blk(t),
        out_shape=jax.ShapeDtypeStruct((bsz, t, A_WIDTH), BF16),
        compiler_params=_params(("parallel",), 32 << 20),
        name="attention_sample",
    )(q, k_cache, v_cache, k_new, v_new, za, tab)


_KB, _QN, _KN, _VB, _KBE, _QE, _KD = range(7)


def _gdn_kernel(xb_ref, ab_ref, zb_ref, conv0_ref, s0_ref, convw_ref, arow_ref, dtrow_ref, normw_ref,
                o_ref, s_out_ref, xbuf, s_scr, nat_scr, gcx_scr, sdec_scr, aq_scr, bo_scr,
                *, tb, n_seq, n_valid, do_conv, n_par, pipelined):
    t_idx = pl.program_id(1)
    blk = GDN_CHUNK
    two = 2 * blk
    seq_chunks = tb // blk
    n_chunks = n_seq * seq_chunks
    rows_all = n_seq * tb

    assert not (pipelined and do_conv)
    rows_out = min(blk, n_valid)

    @pl.when(t_idx == 0)
    def _():
        s_scr[...] = s0_ref[...]
        if do_conv:
            xbuf[:, 0:CONV_HIST, :] = conv0_ref[...]
        if pipelined:
            aq_scr[...] = jnp.zeros(aq_scr.shape, aq_scr.dtype)
            bo_scr[...] = jnp.zeros(bo_scr.shape, bo_scr.dtype)
            sdec_scr[...] = jnp.ones(sdec_scr.shape, sdec_scr.dtype)

    def phase2(cidx):
        seq, r0 = cidx // seq_chunks, (cidx % seq_chunks) * blk
        for p in range(HEAD_PAIRS):
            s_old = s_scr[seq, p]
            tot = bo_scr[cidx, p] + _mm(aq_scr[cidx, p], s_old)
            dec = sdec_scr[cidx, 0:1, p * LANES:(p + 1) * LANES]
            s_scr[seq, p] = s_old * dec + tot[0:two]
            o = tot[two:2 * two]
            ms = jnp.sum(o * o, axis=-1, keepdims=True) * (1.0 / B_HEAD_DIM)
            on = o * lax.rsqrt(ms + NORM_EPS) * normw_ref[...]
            o_pair = (on[0:blk] + on[blk:two])[0:rows_out]
            zb = zb_ref[seq, r0:r0 + rows_out, p * LANES:(p + 1) * LANES].astype(F32)
            o_ref[seq, r0:r0 + rows_out, p * LANES:(p + 1) * LANES] = (o_pair * _silu(zb)).astype(o_ref.dtype)

    pending = list(range(n_chunks)) if pipelined else []

    def padded_seq(ref, sq):
        x = ref[sq]
        if n_valid == tb:
            return x
        return jnp.concatenate([x, jnp.zeros((tb - n_valid, x.shape[1]), x.dtype)], axis=0)

    def padded(ref):
        parts = [padded_seq(ref, sq) for sq in range(n_seq)]
        return parts[0] if n_seq == 1 else jnp.concatenate(parts, axis=0)

    if do_conv:
        for sq in range(n_seq):
            xbuf[sq, CONV_HIST:CONV_HIST + tb, :] = padded_seq(xb_ref, sq)
        c = jnp.concatenate(
            [jnp.concatenate([_conv_qkv_cols(xbuf.at[sq], convw_ref, tb, c0) for c0 in range(0, B_QKV, CONV_COLS)],
                             axis=1) for sq in range(n_seq)], axis=0)
    else:
        c = padded(xb_ref)
    qn, kn, v = c[:, 0:B_WIDTH], c[:, B_WIDTH:2 * B_WIDTH], c[:, 2 * B_WIDTH:]

    ab = padded(ab_ref)
    g = arow_ref[...] * jax.nn.softplus(ab + dtrow_ref[...])
    beta = _sigmoid(ab)
    if n_valid != tb:
        is_token = lax.broadcasted_iota(jnp.int32, (rows_all, 1), 0) % tb < n_valid
        g = jnp.where(is_token, g, 0.0)
        beta = jnp.where(is_token, beta, 0.0)
    rt = lax.broadcasted_iota(jnp.int32, (rows_all, rows_all), 0)
    ct = lax.broadcasted_iota(jnp.int32, (rows_all, rows_all), 1)
    gc = _mm_exact01(((rt // blk) == (ct // blk)) & (rt >= ct), g)
    src = lax.broadcasted_iota(jnp.int32, (LANES, B_WIDTH), 0)
    head = lax.broadcasted_iota(jnp.int32, (LANES, B_WIDTH), 1) // B_HEAD_DIM
    gcx = _mm_exact01_right(gc, src == head)
    betax = _mm_exact01_right(beta, src == head + B_HEADS, pieces=2)
    glx = jnp.concatenate([jnp.broadcast_to(gcx[(i + 1) * blk - 1:(i + 1) * blk, :], (blk, B_WIDTH))
                           for i in range(n_chunks)], axis=0)
    egc = jnp.exp(gcx)
    kbeta = kn * betax
    nat_scr[_KB] = kbeta.astype(BF16)
    nat_scr[_QN] = qn.astype(BF16)
    nat_scr[_KN] = kn.astype(BF16)
    nat_scr[_VB] = (v * betax).astype(BF16)
    nat_scr[_KBE] = (kbeta * egc).astype(BF16)
    nat_scr[_QE] = (qn * egc).astype(BF16)
    nat_scr[_KD] = (kn * jnp.exp(glx - gcx)).astype(BF16)
    gcx_scr[...] = gcx
    sdec_rows = [jnp.exp(gcx[(i + 1) * blk - 1:(i + 1) * blk, :]) for i in range(n_chunks)]

    first = _first_head_lanes()
    r = lax.broadcasted_iota(jnp.int32, (two, two), 0)
    cc = lax.broadcasted_iota(jnp.int32, (two, two), 1)
    causal = ((r // blk) == (cc // blk)) & (r >= cc)
    strict = causal & (r > cc)
    eye = (r == cc).astype(F32)
    diag = (lax.broadcasted_iota(jnp.int32, (blk, LANES), 0)
            == lax.broadcasted_iota(jnp.int32, (blk, LANES), 1) % B_HEAD_DIM)

    def level_mask(bs):
        return ((r // bs) == (cc // bs)) & ((r % bs) >= bs // 2) & ((cc % bs) < bs // 2)

    def stacked(which, r0, p):
        x = nat_scr[which, pl.ds(r0, blk), p * LANES:(p + 1) * LANES]
        zero = jnp.zeros_like(x)
        return jnp.concatenate([jnp.where(first, x, zero), jnp.where(first, zero, x)], axis=0)

    def phase1(step, carry):
        chains = []
        for dc in range(n_par):
            cidx = step * n_par + dc
            for p in range(HEAD_PAIRS):
                chains.append({"cidx": cidx, "r0": cidx * blk, "p": p})
        for ch in chains:
            r0, p = ch["r0"], ch["p"]
            gt = gcx_scr[pl.ds(r0, blk), p * LANES:(p + 1) * LANES]
            gc_row = jnp.sum(jnp.where(diag, gt, 0.0), axis=0, keepdims=True)
            ch["decay"] = jnp.exp(jnp.where(causal, jnp.concatenate([gt, gt], axis=0) - gc_row, NEG))
            lhs = jnp.concatenate([stacked(_KB, r0, p), stacked(_QN, r0, p)], axis=0)
            gram = _mm_nt(lhs, stacked(_KN, r0, p))
            ch["m"] = jnp.where(strict, gram[0:two] * ch["decay"], 0.0)
            ch["qk"] = (gram[two:2 * two] * ch["decay"]).astype(BF16)
            ch["x"] = eye - jnp.where(level_mask(2), ch["m"], 0.0)
        bs = 4
        while bs <= blk:
            lm = level_mask(bs)
            for ch in chains:
                ch["t"] = _mm(jnp.where(lm, ch["m"], 0.0), ch["x"])
            if pending:
                phase2(pending.pop(0))
            for ch in chains:
                ch["x"] = ch["x"] - _mm(ch["x"], ch["t"])
            bs *= 2
        while pending:
            phase2(pending.pop(0))
        for ch in chains:
            r0, p = ch["r0"], ch["p"]
            rhs = jnp.concatenate([stacked(_VB, r0, p), stacked(_KBE, r0, p)], axis=1)
            ch["uw"] = _mm(ch["x"], rhs).astype(BF16)
        for ch in chains:
            r0, p = ch["r0"], ch["p"]
            kd_uw = _mm_tn(stacked(_KD, r0, p), ch["uw"])
            qk_uw = _mm(ch["qk"], ch["uw"])
            qeff = stacked(_QE, r0, p).astype(F32) - qk_uw[:, LANES:2 * LANES]
            aq_scr[ch["cidx"], p] = jnp.concatenate([-kd_uw[:, LANES:2 * LANES], qeff], axis=0).astype(BF16)
            bo_scr[ch["cidx"], p] = jnp.concatenate([kd_uw[:, 0:LANES], qk_uw[:, 0:LANES]], axis=0)
        return carry

    for step in range(n_chunks // n_par):
        phase1(step, 0)
    for i in range(n_chunks):
        sdec_scr[i] = jnp.broadcast_to(sdec_rows[i], (SUBLANES, B_WIDTH))
    if not pipelined:
        for i in range(n_chunks):
            phase2(i)

    @pl.when(t_idx == pl.num_programs(1) - 1)
    def _():
        s_out_ref[...] = s_scr[...]


def _pair_state(s):
    bsz = s.shape[0]
    s = s.reshape(bsz, HEAD_PAIRS, 2, B_HEAD_DIM, B_HEAD_DIM)
    z = jnp.zeros_like(s[:, :, 0])
    top = jnp.concatenate([s[:, :, 0], z], axis=-1)
    bot = jnp.concatenate([z, s[:, :, 1]], axis=-1)
    return jnp.concatenate([top, bot], axis=-2)


def _unpair_state(sp):
    d = B_HEAD_DIM
    s = jnp.stack([sp[:, :, 0:d, 0:d], sp[:, :, d:2 * d, d:2 * d]], axis=2)
    return s.reshape(sp.shape[0], B_HEADS, d, d)


def _gated_delta(xb, ab, zb, conv_prev, s_prev, lw, *, tb, do_conv, pipelined):
    bsz, t, _ = xb.shape
    n_valid = min(t, tb)
    assert t % n_valid == 0 and tb % GDN_CHUNK == 0 and (n_valid == tb or t == n_valid)
    conv0 = jnp.concatenate([jnp.zeros((bsz, CONV_HIST - (CONV_W - 1), B_QKV), F32), conv_prev.astype(F32)], axis=1)
    s0 = _pair_state(s_prev.astype(F32))

    nt = t // n_valid
    n_seq = max(s for s in (1, 2, 4) if bsz % s == 0) if (nt == 1 and tb == GDN_CHUNK) else 1

    def row(width, lag):
        if not pipelined:
            return pl.BlockSpec((n_seq, n_valid, width), lambda b, i: (b, i, 0))
        if lag:
            return pl.BlockSpec((n_seq, n_valid, width), lambda b, i: (b, jnp.maximum(i - 1, 0), 0))
        return pl.BlockSpec((n_seq, n_valid, width), lambda b, i: (b, jnp.minimum(i, nt - 1), 0))

    def per_batch(shape):
        nd = len(shape)
        return pl.BlockSpec((n_seq,) + shape, lambda b, i: (b,) + (0,) * nd)

    n_chunks = n_seq * tb // GDN_CHUNK
    n_par = max(p for p in (1, 2, 4) if n_chunks % p == 0)
    per_chain = (n_chunks, HEAD_PAIRS, 2 * LANES, LANES)
    est = 2 * n_valid * (B_QKV + LANES + B_WIDTH) * 4 + (tb + CONV_HIST) * B_QKV * 4 + 6 * HEAD_PAIRS * LANES * LANES * 4
    rows_all = n_seq * tb
    est += rows_all * B_WIDTH * (7 * 2 + 4 + 16 * 4) + n_chunks * HEAD_PAIRS * LANES * LANES * 8 + rows_all * rows_all * 8
    o, s_out = pl.pallas_call(
        functools.partial(_gdn_kernel, tb=tb, n_seq=n_seq, n_valid=n_valid, do_conv=do_conv, n_par=n_par,
                          pipelined=pipelined),
        grid=(bsz // n_seq, nt + 1 if pipelined else nt),
        in_specs=[row(B_QKV, False), row(LANES, False), row(B_WIDTH, True), per_batch((CONV_HIST, B_QKV)),
                  per_batch((HEAD_PAIRS, LANES, LANES)),
                  _const_spec((CONV_W, B_QKV)), _const_spec((1, LANES)), _const_spec((1, LANES)),
                  _const_spec((1, LANES))],
        out_specs=[row(B_WIDTH, True), per_batch((HEAD_PAIRS, LANES, LANES))],
        out_shape=[jax.ShapeDtypeStruct((bsz, t, B_WIDTH), BF16),
                   jax.ShapeDtypeStruct((bsz, HEAD_PAIRS, LANES, LANES), F32)],
        scratch_shapes=[pltpu.VMEM((n_seq, CONV_HIST + tb, B_QKV), F32),
                        pltpu.VMEM((n_seq, HEAD_PAIRS, LANES, LANES), F32),
                        pltpu.VMEM((7, n_seq * tb, B_WIDTH), BF16),
                        pltpu.VMEM((n_seq * tb, B_WIDTH), F32),
                        pltpu.VMEM((n_chunks, SUBLANES, B_WIDTH), F32),
                        pltpu.VMEM(per_chain, BF16), pltpu.VMEM(per_chain, F32)],
        compiler_params=_params(("parallel", "arbitrary"), est + (16 << 20)),
        name="gated_delta",
    )(xb, ab, zb, conv0, s0, lw["conv_w"], lw["a_row"], lw["dt_row"], lw["normw_row"])
    return o, _unpair_state(s_out)


def _out_kernel(x_ref, a_ref, b_ref, ga_ref, gb_ref, wa_ref, wb_ref, wo_ref, g_ref, beta_ref, lng_ref, lnb_ref,
                y_ref, *, pre_ln, n_split):
    dot = functools.partial(jnp.dot, preferred_element_type=F32)
    rows = x_ref.shape[0] // n_split

    def project(u):
        sl = slice(u * rows, (u + 1) * rows)
        ya = dot(a_ref[sl, :], wa_ref[...])
        yb = dot(b_ref[sl, :], wb_ref[...])
        mix = _sigmoid(ga_ref[sl, :].astype(F32)) * ya + _sigmoid(gb_ref[sl, :].astype(F32)) * yb
        return dot(mix.astype(BF16), wo_ref[...])

    def finish(u, out):
        sl = slice(u * rows, (u + 1) * rows)
        x = x_ref[sl, :]
        if pre_ln:
            x = _ln_rows(x, lng_ref[...], lnb_ref[...])
        y_ref[sl, :] = _ln_rows(ALPHA * x + out, g_ref[...], beta_ref[...])

    outs = {}
    for step in range(n_split + 1):
        if step < n_split:
            outs[step] = project(step)
        if step >= 1:
            finish(step - 1, outs.pop(step - 1))


def _out_projection(x, a, b, ga, gb, w, tm, pre_ln):
    n, d = x.shape

    def row(width):
        return pl.BlockSpec((tm, width), lambda i: (i, 0))

    weights = (w["w_branch_a"], w["w_branch_b"], w["w_out"], w["ln_g"], w["ln_b"], w["ln0_g"], w["ln0_b"])
    est = 2 * tm * (4 * d * 4 + 2 * A_WIDTH * 2) + 2 * (2 * A_WIDTH * d + d * d) + 6 * tm * d * 4
    return pl.pallas_call(
        functools.partial(_out_kernel, pre_ln=pre_ln, n_split=max(1, tm // LANES)),
        grid=(n // tm,),
        in_specs=[row(d), row(A_WIDTH), row(B_WIDTH), row(d), row(d)] + [_const_spec(x.shape) for x in weights],
        out_specs=row(d),
        out_shape=jax.ShapeDtypeStruct((n, d), F32),
        compiler_params=_params(("parallel",), est + (8 << 20)),
        name="out_projection",
    )(x, a, b, ga, gb, *weights)


def _layer_weights(l, w_in, conv_w, a_log, dt_bias, gdn_norm_w, w_branch_a, w_branch_b, w_out, ln_g, ln_b,
                   ln0_g, ln0_b):
    pad_h = (0, LANES - B_HEADS)
    return {
        "layer": l, "w_in": w_in,
        "conv_w": conv_w[l].astype(F32),
        "a_row": jnp.pad(-jnp.exp(a_log[l].astype(F32)), pad_h).reshape(1, LANES),
        "dt_row": jnp.pad(dt_bias[l].astype(F32), pad_h).reshape(1, LANES),
        "normw_row": jnp.tile(gdn_norm_w[l].astype(F32), 2).reshape(1, LANES),
        "w_branch_a": w_branch_a[l].astype(BF16), "w_branch_b": w_branch_b[l].astype(BF16),
        "w_out": w_out[l].astype(BF16),
        "ln_g": ln_g[l].astype(F32).reshape(1, D_MODEL), "ln_b": ln_b[l].astype(F32).reshape(1, D_MODEL),
        "ln0_g": ln0_g.astype(F32).reshape(1, D_MODEL), "ln0_b": ln0_b.astype(F32).reshape(1, D_MODEL),
    }


def _prompt_layer(h, lw, rel_bias, *, tm, tb, pre_ln):
    bsz, t, d = h.shape
    keep = min(ATT_WINDOW, t)
    q, k, v, za, xb, zb, ab, ga, gb, k_tail, v_tail, x_tail = _in_projection(
        h, lw, tm=tm, tail=keep, fuse_conv=True, pre_ln=pre_ln)
    oa = _attention_prompt(q, k, v, za, rel_bias)
    conv_zero = jnp.zeros((bsz, CONV_W - 1, B_QKV), F32)
    s_zero = jnp.zeros((bsz, B_HEADS, B_HEAD_DIM, B_HEAD_DIM), F32)
    ob, s_new = _gated_delta(xb, ab, zb, conv_zero, s_zero, lw, tb=tb, do_conv=False, pipelined=True)
    y = _out_projection(h.reshape(bsz * t, d), oa.reshape(bsz * t, A_WIDTH), ob.reshape(bsz * t, B_WIDTH),
                        ga.reshape(bsz * t, d), gb.reshape(bsz * t, d), lw, tm, pre_ln)
    new_k = k_tail.reshape(bsz, keep, A_HEADS, A_HEAD_DIM)
    new_v = v_tail.reshape(bsz, keep, A_HEADS, A_HEAD_DIM)
    return y.reshape(bsz, t, d), (new_k, new_v, x_tail[:, CONV_HIST - (CONV_W - 1):], s_new)


def _sample_layer(h, k_cache, v_cache, conv_prev, s_prev, lw, rel_bias, *, tm, pre_ln):
    bsz, t, d = h.shape
    n = bsz * t
    q, _, _, za, xb, zb, ab, ga, gb, k_new, v_new, _ = _in_projection(
        h.reshape(1, n, d), lw, tm=tm, tail=n, fuse_conv=False, pre_ln=pre_ln)

    def per_seq(x):
        return x.reshape(bsz, t, x.shape[-1])

    oa = _attention_sample(per_seq(q), k_cache, v_cache, lw["layer"],
                           per_seq(k_new), per_seq(v_new), per_seq(za), rel_bias)
    xb_s = per_seq(xb)
    xp_tail = jnp.concatenate([conv_prev.astype(F32), xb_s], axis=1)[:, -(CONV_W - 1):]
    ob, s_new = _gated_delta(xb_s, per_seq(ab), per_seq(zb), conv_prev, s_prev, lw, tb=GDN_CHUNK, do_conv=True, pipelined=False)
    y = _out_projection(h.reshape(n, d), oa.reshape(n, A_WIDTH), ob.reshape(n, B_WIDTH),
                        ga.reshape(n, d), gb.reshape(n, d), lw, tm, pre_ln)
    new_k = per_seq(k_new).reshape(bsz, t, A_HEADS, A_HEAD_DIM)
    new_v = per_seq(v_new).reshape(bsz, t, A_HEADS, A_HEAD_DIM)
    return y.reshape(bsz, t, d), (new_k, new_v, xp_tail, s_new)


def kernel(x_prompt, x_sample, cache_attn_k, cache_attn_v, state_conv, state_gdn, ln0_g, ln0_b, w_in, rel_bias,
           conv_w, a_log, dt_bias, gdn_norm_w, w_branch_a, w_branch_b, w_out, ln_g, ln_b):
    bp, tp, d = x_prompt.shape
    bs, ts, _ = x_sample.shape
    hp, hs = x_prompt, x_sample
    w_in_bf16 = w_in.astype(BF16)
    n_past = cache_attn_k.shape[2]
    k_cache = cache_attn_k.reshape(DEPTH, bs, n_past, A_WIDTH)
    v_cache = cache_attn_v.reshape(DEPTH, bs, n_past, A_WIDTH)
    outs_p, outs_s = [], []
    for l in range(DEPTH):
        lw = _layer_weights(l, w_in_bf16, conv_w, a_log, dt_bias, gdn_norm_w, w_branch_a, w_branch_b, w_out, ln_g, ln_b,
                            ln0_g, ln0_b)
        hp, st_p = _prompt_layer(hp, lw, rel_bias[l].astype(F32), tm=512, tb=256, pre_ln=(l == 0))
        hs, st_s = _sample_layer(hs, k_cache, v_cache, state_conv[l], state_gdn[l], lw,
                                 rel_bias[l].astype(F32), tm=256, pre_ln=(l == 0))
        outs_p.append(st_p)
        outs_s.append(st_s)

    def stacked(outs, i):
        return jnp.stack([o[i] for o in outs])

    return (hp, hs,
            stacked(outs_p, 0), stacked(outs_p, 1), stacked(outs_p, 2), stacked(outs_p, 3),
            stacked(outs_s, 0), stacked(outs_s, 1), stacked(outs_s, 2), stacked(outs_s, 3))
```

```python
import functools
import math

import numpy as np
import jax
import jax.numpy as jnp
from jax import lax
from jax.experimental import pallas as pl
from jax.experimental.pallas import tpu as pltpu

D_MODEL = 1024
DEPTH = 2
PAST_LEN = 4096
CHUNK = 64
BAND_CHUNKS = 8
ATT_WINDOW = BAND_CHUNKS * CHUNK
A_HEADS = 8
A_HEAD_DIM = 64
A_WIDTH = A_HEADS * A_HEAD_DIM
REL_MAX = 128
B_HEADS = 8
B_HEAD_DIM = 64
B_WIDTH = B_HEADS * B_HEAD_DIM
B_QKV = 3 * B_WIDTH
CONV_W = 4
GDN_CHUNK = 64
ALPHA = (2 * DEPTH) ** 0.25
LN_EPS = 1e-5
NORM_EPS = 1e-6

LANES = 128
SUBLANES = 8
HEAD_PAIRS = A_HEADS // 2
CONV_HIST = SUBLANES
NEG = -1e30
LOG2E = math.log2(math.e)
Q_SCALE = (A_HEAD_DIM ** -0.5) * LOG2E
V7X_VMEM_LIMIT = 56 * 1024 * 1024

PROJ_ROWS_PROMPT = 512
PROJ_ROWS_SAMPLE = 256
GDN_ROWS = 4 * GDN_CHUNK

BF16 = jnp.bfloat16
F32 = jnp.float32


def _mm(a, b):
    return jnp.dot(a.astype(BF16), b.astype(BF16), preferred_element_type=F32)


def _mm_nt(a, b):
    return lax.dot_general(a.astype(BF16), b.astype(BF16), (((1,), (1,)), ((), ())),
                           preferred_element_type=F32)


def _mm_tn(a, b):
    return lax.dot_general(a.astype(BF16), b.astype(BF16), (((0,), (0,)), ((), ())),
                           preferred_element_type=F32)


def _split3(x):
    x1 = x.astype(BF16)
    r1 = x - x1.astype(F32)
    x2 = r1.astype(BF16)
    x3 = (r1 - x2.astype(F32)).astype(BF16)
    return x1, x2, x3


def _mm_exact01(w01, x):
    w = w01.astype(BF16)
    return sum(jnp.dot(w, p, preferred_element_type=F32) for p in _split3(x))


def _mm_exact01_right(x, w01, pieces=3):
    w = w01.astype(BF16)
    return sum(jnp.dot(p, w, preferred_element_type=F32) for p in _split3(x)[:pieces])


def _sigmoid(x):
    return 0.5 + 0.5 * jnp.tanh(0.5 * x)


def _silu(x):
    h = 0.5 * x
    return h + h * jnp.tanh(h)


def _first_head_lanes():
    return lax.broadcasted_iota(jnp.int32, (1, LANES), 1) < B_HEAD_DIM


def _params(sem, est_bytes):
    limit = int(min(V7X_VMEM_LIMIT, max(32 * 1024 * 1024, est_bytes)))
    return pltpu.CompilerParams(dimension_semantics=sem, vmem_limit_bytes=limit)


def _const_spec(shape):
    nd = len(shape)
    return pl.BlockSpec(shape, lambda *_: (0,) * nd, pipeline_mode=pl.Buffered(1))


def _ln_rows(x, g, b):
    mu = jnp.mean(x, axis=-1, keepdims=True)
    xc = x - mu
    var = jnp.mean(xc * xc, axis=-1, keepdims=True)
    return xc * lax.rsqrt(var + LN_EPS) * g + b


CONV_COLS = 2 * LANES


def _l2norm_pair(xs, scale):
    first = _first_head_lanes()
    x2 = xs * xs
    zero = jnp.zeros_like(x2)
    s0 = jnp.sum(jnp.where(first, x2, zero), axis=-1, keepdims=True)
    s1 = jnp.sum(jnp.where(first, zero, x2), axis=-1, keepdims=True)
    inv = jnp.where(first, lax.rsqrt(s0 + NORM_EPS), lax.rsqrt(s1 + NORM_EPS))
    return xs * (inv * scale)


def _conv_qkv_cols(xbuf, convw_ref, rows, c0, width=CONV_COLS):
    cols = slice(c0, c0 + width)
    first_tap = CONV_HIST - (CONV_W - 1)
    acc = xbuf[first_tap:first_tap + rows, cols] * convw_ref[0:1, cols]
    for i in range(1, CONV_W):
        acc = acc + xbuf[first_tap + i:first_tap + i + rows, cols] * convw_ref[i:i + 1, cols]
    tail = xbuf[rows:rows + CONV_HIST, cols]
    xbuf[0:CONV_HIST, cols] = tail
    c = _silu(acc)
    if c0 >= 2 * B_WIDTH:
        return c
    scale = B_HEAD_DIM ** -0.5 if c0 < B_WIDTH else 1.0
    return jnp.concatenate([_l2norm_pair(c[:, j * LANES:(j + 1) * LANES], scale)
                            for j in range(width // LANES)], axis=1)


_COL_B = 4 * A_WIDTH
_COL_ZB = _COL_B + B_QKV
_COL_AB = _COL_ZB + B_WIDTH
_COL_G = _COL_AB + 2 * B_HEADS
IN_DIM = _COL_G + 2 * D_MODEL


def _inproj_kernel(x_ref, w_ref, convw_ref, lng_ref, lnb_ref,
                   q_ref, k_ref, v_ref, za_ref, xb_ref, zb_ref, ab_ref, ga_ref, gb_ref,
                   kt_ref, vt_ref, xtail_ref, xbuf, wg_scr, *, fuse_conv, pre_ln):
    tm = x_ref.shape[1]
    dot = functools.partial(jnp.dot, preferred_element_type=F32)

    @pl.when(pl.program_id(1) == 0)
    def _():
        wg_scr[...] = w_ref[:, _COL_G:IN_DIM]
        if fuse_conv:
            xbuf[0:CONV_HIST, :] = jnp.zeros((CONV_HIST, B_QKV), F32)

    x = x_ref[0]
    if pre_ln:
        x = _ln_rows(x, lng_ref[...], lnb_ref[...])
    x = x.astype(BF16)

    def attn_cols(j, half):
        cols = slice(half * CONV_COLS, (half + 1) * CONV_COLS)
        c0 = j * A_WIDTH + half * CONV_COLS
        r = dot(x, w_ref[:, c0:c0 + CONV_COLS])
        if j == 0:
            q_ref[0, :, cols] = (r * Q_SCALE).astype(BF16)
        elif j == 3:
            za_ref[0, :, cols] = r.astype(BF16)
        else:
            (k_ref, v_ref)[j - 1][0, :, cols] = r.astype(BF16)
            (kt_ref, vt_ref)[j - 1][0, :, cols] = r

    def gate_cols(j, part):
        cols = slice(part * CONV_COLS, (part + 1) * CONV_COLS)
        c0 = j * D_MODEL + part * CONV_COLS
        (ga_ref, gb_ref)[j][0, :, cols] = dot(x, wg_scr[:, c0:c0 + CONV_COLS]).astype(BF16)

    def zb_cols(half):
        cols = slice(half * CONV_COLS, (half + 1) * CONV_COLS)
        zb_ref[0, :, cols] = dot(x, w_ref[:, _COL_ZB + half * CONV_COLS:_COL_ZB + (half + 1) * CONV_COLS]).astype(BF16)

    others = ([functools.partial(attn_cols, j, half) for j in range(4) for half in range(2)]
              + [functools.partial(zb_cols, half) for half in range(2)]
              + [functools.partial(gate_cols, j, part) for j in range(2) for part in range(D_MODEL // CONV_COLS)])
    for step in range(B_QKV // CONV_COLS):
        c0 = step * CONV_COLS
        xb = dot(x, w_ref[:, _COL_B + c0:_COL_B + c0 + CONV_COLS])
        xtail_ref[0, :, c0:c0 + CONV_COLS] = xb[tm - CONV_HIST:tm, :]
        if fuse_conv:
            xbuf[CONV_HIST:CONV_HIST + tm, c0:c0 + CONV_COLS] = xb
        else:
            xb_ref[0, :, c0:c0 + CONV_COLS] = xb
        for piece in range(CONV_COLS // LANES):
            if others:
                others.pop(0)()
            if fuse_conv:
                cp = c0 + piece * LANES
                xb_ref[0, :, cp:cp + LANES] = _conv_qkv_cols(xbuf, convw_ref, tm, cp, LANES)
    for rest in others:
        rest()
    ab_ref[0] = dot(x, w_ref[:, _COL_AB:_COL_AB + LANES])


def _in_projection(h, w, *, tm, tail, fuse_conv, pre_ln):
    bsz, t, d = h.shape
    nt = t // tm
    tail_blocks = tail // tm

    def row(width):
        return pl.BlockSpec((1, tm, width), lambda b, i: (b, i, 0))

    tail_spec = pl.BlockSpec((1, tm, A_WIDTH),
                             lambda b, i: (b, jnp.maximum(i - (nt - tail_blocks), 0), 0))
    xtail_spec = pl.BlockSpec((1, CONV_HIST, B_QKV), lambda b, i: (b, 0, 0))
    out_shapes = [
        jax.ShapeDtypeStruct((bsz, t, A_WIDTH), BF16),
        jax.ShapeDtypeStruct((bsz, t, A_WIDTH), BF16),
        jax.ShapeDtypeStruct((bsz, t, A_WIDTH), BF16),
        jax.ShapeDtypeStruct((bsz, t, A_WIDTH), BF16),
        jax.ShapeDtypeStruct((bsz, t, B_QKV), F32),
        jax.ShapeDtypeStruct((bsz, t, B_WIDTH), BF16),
        jax.ShapeDtypeStruct((bsz, t, LANES), F32),
        jax.ShapeDtypeStruct((bsz, t, D_MODEL), BF16),
        jax.ShapeDtypeStruct((bsz, t, D_MODEL), BF16),
        jax.ShapeDtypeStruct((bsz, tail, A_WIDTH), F32),
        jax.ShapeDtypeStruct((bsz, tail, A_WIDTH), F32),
        jax.ShapeDtypeStruct((bsz, CONV_HIST, B_QKV), F32),
    ]
    out_specs = [row(A_WIDTH), row(A_WIDTH), row(A_WIDTH), row(A_WIDTH), row(B_QKV), row(B_WIDTH),
                 row(LANES), row(D_MODEL), row(D_MODEL), tail_spec, tail_spec, xtail_spec]
    layer = w["layer"]
    weights = (w["w_in"], w["conv_w"], w["ln0_g"], w["ln0_b"])
    w_spec = pl.BlockSpec((None,) + w["w_in"].shape[1:], lambda *_: (layer, 0, 0), pipeline_mode=pl.Buffered(1))
    in_specs = [row(d), w_spec] + [_const_spec(x.shape) for x in weights[1:]]
    n_w = sum(int(np.prod(x.shape[-2:])) for x in weights)
    est = 2 * n_w + 2 * tm * (d * 4 + 3 * A_WIDTH * 2 + (2 * A_WIDTH + B_QKV + B_WIDTH + LANES
                                                        + 2 * D_MODEL + 2 * A_WIDTH) * 4)
    est += tm * (4 * A_WIDTH + 2 * D_MODEL + 4 * B_QKV) * 4 * 2
    return pl.pallas_call(
        functools.partial(_inproj_kernel, fuse_conv=fuse_conv, pre_ln=pre_ln),
        grid=(bsz, nt),
        in_specs=in_specs,
        out_specs=out_specs,
        out_shape=out_shapes,
        scratch_shapes=[pltpu.VMEM((CONV_HIST + tm, B_QKV), F32), pltpu.VMEM((d, 2 * D_MODEL), BF16)],
        compiler_params=_params(("parallel", "arbitrary"), est + (8 << 20)),
        name="in_projection",
    )(h, *weights)


def _attention_heads(jobs, bias_ref):
    first = _first_head_lanes()

    def scores(job, p):
        q, k = job[0], job[1]
        qp, kp = q[:, p * LANES:(p + 1) * LANES], k[:, p * LANES:(p + 1) * LANES]
        zero = jnp.zeros_like(qp)
        q2 = jnp.concatenate([jnp.where(first, qp, zero), jnp.where(first, zero, qp)], axis=0)
        return _mm_nt(q2, kp)

    def softmax(job, p, s):
        s = s + bias_ref[p]
        if job[4] is not None:
            s = jnp.where(job[4], s, NEG)
        e = jnp.exp2(s - jnp.max(s, axis=-1, keepdims=True))
        return e.astype(BF16), jnp.sum(e, axis=-1, keepdims=True)

    def output(job, p, e_den):
        _, _, v, za, _, o_ref = job
        nq = za.shape[0]
        sl = slice(p * LANES, (p + 1) * LANES)
        pv = _mm(e_den[0], v[:, sl]) / e_den[1]
        o = jnp.where(first, pv[0:nq], pv[nq:2 * nq])
        o_ref[:, sl] = (o * _silu(za[:, sl].astype(F32))).astype(o_ref.dtype)

    chains = [(job, p) for job in jobs for p in range(HEAD_PAIRS)]
    s_val, e_val = {}, {}
    for step in range(len(chains) + 2):
        if step < len(chains):
            s_val[step] = scores(*chains[step])
        if 0 <= step - 1 < len(chains):
            e_val[step - 1] = softmax(*chains[step - 1], s_val.pop(step - 1))
        if 0 <= step - 2 < len(chains):
            output(*chains[step - 2], e_val.pop(step - 2))


def _attn_prompt_kernel(q_ref, *refs, qblk, n_sub, n_kblk):
    k_refs, v_refs = refs[0:n_kblk], refs[n_kblk:2 * n_kblk]
    za_ref, bias_ref, o_ref = refs[2 * n_kblk:]
    m = pl.program_id(1)
    n_lead = n_kblk - n_sub
    band = (n_lead + 1) * qblk

    def jobs(masked):
        out = []
        for u in range(n_sub):
            k = jnp.concatenate([r[0] for r in k_refs[u:u + n_lead + 1]], axis=0)
            v = jnp.concatenate([r[0] for r in v_refs[u:u + n_lead + 1]], axis=0)
            valid = None
            if masked:
                col = lax.broadcasted_iota(jnp.int32, (1, band), 1) + (m * n_sub + u - n_lead) * qblk
                valid = col >= 0
            rows = slice(u * qblk, (u + 1) * qblk)
            out.append((q_ref[0, rows, :], k, v, za_ref[0, rows, :], valid, o_ref.at[0, rows, :]))
        return out

    @pl.when(m * n_sub >= n_lead)
    def _():
        _attention_heads(jobs(False), bias_ref)

    @pl.when(m * n_sub < n_lead)
    def _():
        _attention_heads(jobs(True), bias_ref)


def _rel_bias_table(rel_bias, nq, nk, n_before):
    h = rel_bias.shape[0]
    n_var = nq + REL_MAX
    assert nk - n_var >= 0 and n_before + nq == nk
    period = n_var + nq
    far = rel_bias[:, REL_MAX + CHUNK - 1:]
    n_low = n_var - (REL_MAX + CHUNK)
    g = jnp.concatenate([rel_bias[:, ::-1], jnp.repeat(rel_bias[:, :1], n_low, axis=1),
                         jnp.repeat(far, period - n_var, axis=1)], axis=1)
    skew = jnp.tile(g, (1, nq))[:, :nq * (period - 1)].reshape(h, nq, period - 1)[:, :, :n_var]
    const = jnp.broadcast_to(far[:, :, None], (h, nq, nk - n_var))
    return jnp.concatenate([const, skew], axis=2)


def _pair_tables(tab):
    h, nq, nk = tab.shape
    return (tab * LOG2E).astype(F32).reshape(h // 2, 2 * nq, nk)


def _attention_prompt(q, k, v, za, rel_bias, qblk=2 * CHUNK, n_sub=4):
    bsz, t, _ = q.shape
    n_lead = ATT_WINDOW // qblk
    n_kblk = n_lead + n_sub
    band = (n_lead + 1) * qblk
    i = np.arange(qblk)[:, None]
    j = np.arange(band)[None, :]
    qc, kc = i // CHUNK, j // CHUNK
    visible = (kc >= qc) & (kc <= qc + BAND_CHUNKS)
    tab = _rel_bias_table(rel_bias, qblk, band, ATT_WINDOW)
    tab = _pair_tables(jnp.where(jnp.asarray(visible)[None], tab, NEG))
    row = pl.BlockSpec((1, n_sub * qblk, A_WIDTH), lambda b, m: (b, m, 0))

    def key_block(jb):
        return pl.BlockSpec((1, qblk, A_WIDTH), lambda b, m: (b, jnp.maximum(m * n_sub - n_lead + jb, 0), 0))

    kv_specs = [key_block(jb) for jb in range(n_kblk)]
    est = tab.size * 4 + 2 * (2 * n_kblk + 2 * n_sub) * qblk * A_WIDTH * 2 + n_sub * 32 * qblk * band * 4
    return pl.pallas_call(
        functools.partial(_attn_prompt_kernel, qblk=qblk, n_sub=n_sub, n_kblk=n_kblk),
        grid=(bsz, t // (n_sub * qblk)),
        in_specs=[row] + kv_specs + kv_specs + [row, _const_spec(tab.shape)],
        out_specs=row,
        out_shape=jax.ShapeDtypeStruct((bsz, t, A_WIDTH), BF16),
        compiler_params=_params(("parallel", "arbitrary"), est + (8 << 20)),
        name="attention_prompt",
    )(q, *([k] * n_kblk), *([v] * n_kblk), za, tab)


def _attn_sample_kernel(q_ref, kc_ref, vc_ref, kn_ref, vn_ref, za_ref, bias_ref, o_ref, *, nk):
    n_seq, n_past, t = kc_ref.shape[0], kc_ref.shape[1], kn_ref.shape[1]
    zeros = jnp.zeros((nk - n_past - t, A_WIDTH), BF16)
    jobs = []
    for sq in range(n_seq):
        k = jnp.concatenate([kc_ref[sq].astype(BF16), kn_ref[sq].astype(BF16), zeros], axis=0)
        v = jnp.concatenate([vc_ref[sq].astype(BF16), vn_ref[sq].astype(BF16), zeros], axis=0)
        jobs.append((q_ref[sq], k, v, za_ref[sq], None, o_ref.at[sq]))
    _attention_heads(jobs, bias_ref)


def _attention_sample(q, k_cache, v_cache, layer, k_new, v_new, za, rel_bias):
    bsz, t, _ = q.shape
    n_past = k_cache.shape[2]
    nk = -(-(n_past + t) // LANES) * LANES
    nq_tab = nk - n_past
    tab = _rel_bias_table(rel_bias, nq_tab, nk, n_past)[:, :t]
    tab = _pair_tables(jnp.where(jnp.asarray(np.arange(nk) < n_past + t)[None, None], tab, NEG))

    n_seq = max(s for s in (1, 2, 4) if bsz % s == 0)

    def blk(rows):
        return pl.BlockSpec((n_seq, rows, A_WIDTH), lambda b: (b, 0, 0))

    cache = pl.BlockSpec((None, n_seq, n_past, A_WIDTH), lambda b: (layer, b, 0, 0))
    return pl.pallas_call(
        functools.partial(_attn_sample_kernel, nk=nk),
        grid=(bsz // n_seq,),
        in_specs=[blk(t), cache, cache, blk(t), blk(t), blk(t), _const_spec(tab.shape)],
        out_specs=blk(t),
        out_shape=jax.ShapeDtypeStruct((bsz, t, A_WIDTH), BF16),
        compiler_params=_params(("parallel",), 32 << 20),
        name="attention_sample",
    )(q, k_cache, v_cache, k_new, v_new, za, tab)


_KB, _QN, _KN, _VB, _KBE, _QE, _KD = range(7)


def _gdn_kernel(xb_ref, ab_ref, zb_ref, conv0_ref, s0_ref, convw_ref, arow_ref, dtrow_ref, normw_ref,
                o_ref, s_out_ref, xbuf, s_scr, nat_scr, gcx_scr, sdec_scr, aq_scr, bo_scr,
                *, tb, n_seq, n_valid, do_conv, n_par, pipelined):
    t_idx = pl.program_id(1)
    blk = GDN_CHUNK
    two = 2 * blk
    seq_chunks = tb // blk
    n_chunks = n_seq * seq_chunks
    rows_all = n_seq * tb

    assert not (pipelined and do_conv)
    rows_out = min(blk, n_valid)

    @pl.when(t_idx == 0)
    def _():
        s_scr[...] = s0_ref[...]
        if do_conv:
            xbuf[:, 0:CONV_HIST, :] = conv0_ref[...]
        if pipelined:
            aq_scr[...] = jnp.zeros(aq_scr.shape, aq_scr.dtype)
            bo_scr[...] = jnp.zeros(bo_scr.shape, bo_scr.dtype)
            sdec_scr[...] = jnp.ones(sdec_scr.shape, sdec_scr.dtype)

    def phase2(cidx):
        seq, r0 = cidx // seq_chunks, (cidx % seq_chunks) * blk
        for p in range(HEAD_PAIRS):
            s_old = s_scr[seq, p]
            tot = bo_scr[cidx, p] + _mm(aq_scr[cidx, p], s_old)
            dec = sdec_scr[cidx, 0:1, p * LANES:(p + 1) * LANES]
            s_scr[seq, p] = s_old * dec + tot[0:two]
            o = tot[two:2 * two]
            ms = jnp.sum(o * o, axis=-1, keepdims=True) * (1.0 / B_HEAD_DIM)
            on = o * lax.rsqrt(ms + NORM_EPS) * normw_ref[...]
            o_pair = (on[0:blk] + on[blk:two])[0:rows_out]
            zb = zb_ref[seq, r0:r0 + rows_out, p * LANES:(p + 1) * LANES].astype(F32)
            o_ref[seq, r0:r0 + rows_out, p * LANES:(p + 1) * LANES] = (o_pair * _silu(zb)).astype(o_ref.dtype)

    pending = list(range(n_chunks)) if pipelined else []

    def padded_seq(ref, sq):
        x = ref[sq]
        if n_valid == tb:
            return x
        return jnp.concatenate([x, jnp.zeros((tb - n_valid, x.shape[1]), x.dtype)], axis=0)

    def padded(ref):
        parts = [padded_seq(ref, sq) for sq in range(n_seq)]
        return parts[0] if n_seq == 1 else jnp.concatenate(parts, axis=0)

    if do_conv:
        for sq in range(n_seq):
            xbuf[sq, CONV_HIST:CONV_HIST + tb, :] = padded_seq(xb_ref, sq)
        c = jnp.concatenate(
            [jnp.concatenate([_conv_qkv_cols(xbuf.at[sq], convw_ref, tb, c0) for c0 in range(0, B_QKV, CONV_COLS)],
                             axis=1) for sq in range(n_seq)], axis=0)
    else:
        c = padded(xb_ref)
    qn, kn, v = c[:, 0:B_WIDTH], c[:, B_WIDTH:2 * B_WIDTH], c[:, 2 * B_WIDTH:]

    ab = padded(ab_ref)
    g = arow_ref[...] * jax.nn.softplus(ab + dtrow_ref[...])
    beta = _sigmoid(ab)
    if n_valid != tb:
        is_token = lax.broadcasted_iota(jnp.int32, (rows_all, 1), 0) % tb < n_valid
        g = jnp.where(is_token, g, 0.0)
        beta = jnp.where(is_token, beta, 0.0)
    rt = lax.broadcasted_iota(jnp.int32, (rows_all, rows_all), 0)
    ct = lax.broadcasted_iota(jnp.int32, (rows_all, rows_all), 1)
    gc = _mm_exact01(((rt // blk) == (ct // blk)) & (rt >= ct), g)
    src = lax.broadcasted_iota(jnp.int32, (LANES, B_WIDTH), 0)
    head = lax.broadcasted_iota(jnp.int32, (LANES, B_WIDTH), 1) // B_HEAD_DIM
    gcx = _mm_exact01_right(gc, src == head)
    betax = _mm_exact01_right(beta, src == head + B_HEADS, pieces=2)
    glx = jnp.concatenate([jnp.broadcast_to(gcx[(i + 1) * blk - 1:(i + 1) * blk, :], (blk, B_WIDTH))
                           for i in range(n_chunks)], axis=0)
    egc = jnp.exp(gcx)
    kbeta = kn * betax
    nat_scr[_KB] = kbeta.astype(BF16)
    nat_scr[_QN] = qn.astype(BF16)
    nat_scr[_KN] = kn.astype(BF16)
    nat_scr[_VB] = (v * betax).astype(BF16)
    nat_scr[_KBE] = (kbeta * egc).astype(BF16)
    nat_scr[_QE] = (qn * egc).astype(BF16)
    nat_scr[_KD] = (kn * jnp.exp(glx - gcx)).astype(BF16)
    gcx_scr[...] = gcx
    sdec_rows = [jnp.exp(gcx[(i + 1) * blk - 1:(i + 1) * blk, :]) for i in range(n_chunks)]

    first = _first_head_lanes()
    r = lax.broadcasted_iota(jnp.int32, (two, two), 0)
    cc = lax.broadcasted_iota(jnp.int32, (two, two), 1)
    causal = ((r // blk) == (cc // blk)) & (r >= cc)
    strict = causal & (r > cc)
    eye = (r == cc).astype(F32)
    diag = (lax.broadcasted_iota(jnp.int32, (blk, LANES), 0)
            == lax.broadcasted_iota(jnp.int32, (blk, LANES), 1) % B_HEAD_DIM)

    def level_mask(bs):
        return ((r // bs) == (cc // bs)) & ((r % bs) >= bs // 2) & ((cc % bs) < bs // 2)

    def stacked(which, r0, p):
        x = nat_scr[which, pl.ds(r0, blk), p * LANES:(p + 1) * LANES]
        zero = jnp.zeros_like(x)
        return jnp.concatenate([jnp.where(first, x, zero), jnp.where(first, zero, x)], axis=0)

    def phase1(step, carry):
        chains = []
        for dc in range(n_par):
            cidx = step * n_par + dc
            for p in range(HEAD_PAIRS):
                chains.append({"cidx": cidx, "r0": cidx * blk, "p": p})
        for ch in chains:
            r0, p = ch["r0"], ch["p"]
            gt = gcx_scr[pl.ds(r0, blk), p * LANES:(p + 1) * LANES]
            gc_row = jnp.sum(jnp.where(diag, gt, 0.0), axis=0, keepdims=True)
            ch["decay"] = jnp.exp(jnp.where(causal, jnp.concatenate([gt, gt], axis=0) - gc_row, NEG))
            lhs = jnp.concatenate([stacked(_KB, r0, p), stacked(_QN, r0, p)], axis=0)
            gram = _mm_nt(lhs, stacked(_KN, r0, p))
            ch["m"] = jnp.where(strict, gram[0:two] * ch["decay"], 0.0)
            ch["qk"] = (gram[two:2 * two] * ch["decay"]).astype(BF16)
            ch["x"] = eye - jnp.where(level_mask(2), ch["m"], 0.0)
        bs = 4
        while bs <= blk:
            lm = level_mask(bs)
            for ch in chains:
                ch["t"] = _mm(jnp.where(lm, ch["m"], 0.0), ch["x"])
            if pending:
                phase2(pending.pop(0))
            for ch in chains:
                ch["x"] = ch["x"] - _mm(ch["x"], ch["t"])
            bs *= 2
        while pending:
            phase2(pending.pop(0))
        for ch in chains:
            r0, p = ch["r0"], ch["p"]
            rhs = jnp.concatenate([stacked(_VB, r0, p), stacked(_KBE, r0, p)], axis=1)
            ch["uw"] = _mm(ch["x"], rhs).astype(BF16)
        for ch in chains:
            r0, p = ch["r0"], ch["p"]
            kd_uw = _mm_tn(stacked(_KD, r0, p), ch["uw"])
            qk_uw = _mm(ch["qk"], ch["uw"])
            qeff = stacked(_QE, r0, p).astype(F32) - qk_uw[:, LANES:2 * LANES]
            aq_scr[ch["cidx"], p] = jnp.concatenate([-kd_uw[:, LANES:2 * LANES], qeff], axis=0).astype(BF16)
            bo_scr[ch["cidx"], p] = jnp.concatenate([kd_uw[:, 0:LANES], qk_uw[:, 0:LANES]], axis=0)
        return carry

    for step in range(n_chunks // n_par):
        phase1(step, 0)
    for i in range(n_chunks):
        sdec_scr[i] = jnp.broadcast_to(sdec_rows[i], (SUBLANES, B_WIDTH))
    if not pipelined:
        for i in range(n_chunks):
            phase2(i)

    @pl.when(t_idx == pl.num_programs(1) - 1)
    def _():
        s_out_ref[...] = s_scr[...]


def _pair_state(s):
    bsz = s.shape[0]
    s = s.reshape(bsz, HEAD_PAIRS, 2, B_HEAD_DIM, B_HEAD_DIM)
    z = jnp.zeros_like(s[:, :, 0])
    top = jnp.concatenate([s[:, :, 0], z], axis=-1)
    bot = jnp.concatenate([z, s[:, :, 1]], axis=-1)
    return jnp.concatenate([top, bot], axis=-2)


def _unpair_state(sp):
    d = B_HEAD_DIM
    s = jnp.stack([sp[:, :, 0:d, 0:d], sp[:, :, d:2 * d, d:2 * d]], axis=2)
    return s.reshape(sp.shape[0], B_HEADS, d, d)


def _gated_delta(xb, ab, zb, conv_prev, s_prev, lw, *, tb, do_conv, pipelined):
    bsz, t, _ = xb.shape
    n_valid = min(t, tb)
    assert t % n_valid == 0 and tb % GDN_CHUNK == 0 and (n_valid == tb or t == n_valid)
    conv0 = jnp.concatenate([jnp.zeros((bsz, CONV_HIST - (CONV_W - 1), B_QKV), F32), conv_prev.astype(F32)], axis=1)
    s0 = _pair_state(s_prev.astype(F32))

    nt = t // n_valid
    n_seq = max(s for s in (1, 2, 4) if bsz % s == 0) if (nt == 1 and tb == GDN_CHUNK) else 1

    def row(width, lag):
        if not pipelined:
            return pl.BlockSpec((n_seq, n_valid, width), lambda b, i: (b, i, 0))
        if lag:
            return pl.BlockSpec((n_seq, n_valid, width), lambda b, i: (b, jnp.maximum(i - 1, 0), 0))
        return pl.BlockSpec((n_seq, n_valid, width), lambda b, i: (b, jnp.minimum(i, nt - 1), 0))

    def per_batch(shape):
        nd = len(shape)
        return pl.BlockSpec((n_seq,) + shape, lambda b, i: (b,) + (0,) * nd)

    n_chunks = n_seq * tb // GDN_CHUNK
    n_par = max(p for p in (1, 2, 4) if n_chunks % p == 0)
    per_chain = (n_chunks, HEAD_PAIRS, 2 * LANES, LANES)
    est = 2 * n_valid * (B_QKV + LANES + B_WIDTH) * 4 + (tb + CONV_HIST) * B_QKV * 4 + 6 * HEAD_PAIRS * LANES * LANES * 4
    rows_all = n_seq * tb
    est += rows_all * B_WIDTH * (7 * 2 + 4 + 16 * 4) + n_chunks * HEAD_PAIRS * LANES * LANES * 8 + rows_all * rows_all * 8
    o, s_out = pl.pallas_call(
        functools.partial(_gdn_kernel, tb=tb, n_seq=n_seq, n_valid=n_valid, do_conv=do_conv, n_par=n_par,
                          pipelined=pipelined),
        grid=(bsz // n_seq, nt + 1 if pipelined else nt),
        in_specs=[row(B_QKV, False), row(LANES, False), row(B_WIDTH, True), per_batch((CONV_HIST, B_QKV)),
                  per_batch((HEAD_PAIRS, LANES, LANES)),
                  _const_spec((CONV_W, B_QKV)), _const_spec((1, LANES)), _const_spec((1, LANES)),
                  _const_spec((1, LANES))],
        out_specs=[row(B_WIDTH, True), per_batch((HEAD_PAIRS, LANES, LANES))],
        out_shape=[jax.ShapeDtypeStruct((bsz, t, B_WIDTH), BF16),
                   jax.ShapeDtypeStruct((bsz, HEAD_PAIRS, LANES, LANES), F32)],
        scratch_shapes=[pltpu.VMEM((n_seq, CONV_HIST + tb, B_QKV), F32),
                        pltpu.VMEM((n_seq, HEAD_PAIRS, LANES, LANES), F32),
                        pltpu.VMEM((7, n_seq * tb, B_WIDTH), BF16),
                        pltpu.VMEM((n_seq * tb, B_WIDTH), F32),
                        pltpu.VMEM((n_chunks, SUBLANES, B_WIDTH), F32),
                        pltpu.VMEM(per_chain, BF16), pltpu.VMEM(per_chain, F32)],
        compiler_params=_params(("parallel", "arbitrary"), est + (16 << 20)),
        name="gated_delta",
    )(xb, ab, zb, conv0, s0, lw["conv_w"], lw["a_row"], lw["dt_row"], lw["normw_row"])
    return o, _unpair_state(s_out)


def _out_kernel(x_ref, a_ref, b_ref, ga_ref, gb_ref, wa_ref, wb_ref, wo_ref, g_ref, beta_ref, lng_ref, lnb_ref,
                y_ref, *, pre_ln, n_split):
    dot = functools.partial(jnp.dot, preferred_element_type=F32)
    rows = x_ref.shape[0] // n_split

    def project(u):
        sl = slice(u * rows, (u + 1) * rows)
        ya = dot(a_ref[sl, :], wa_ref[...])
        yb = dot(b_ref[sl, :], wb_ref[...])
        mix = _sigmoid(ga_ref[sl, :].astype(F32)) * ya + _sigmoid(gb_ref[sl, :].astype(F32)) * yb
        return dot(mix.astype(BF16), wo_ref[...])

    def finish(u, out):
        sl = slice(u * rows, (u + 1) * rows)
        x = x_ref[sl, :]
        if pre_ln:
            x = _ln_rows(x, lng_ref[...], lnb_ref[...])
        y_ref[sl, :] = _ln_rows(ALPHA * x + out, g_ref[...], beta_ref[...])

    outs = {}
    for step in range(n_split + 1):
        if step < n_split:
            outs[step] = project(step)
        if step >= 1:
            finish(step - 1, outs.pop(step - 1))


def _out_projection(x, a, b, ga, gb, w, tm, pre_ln):
    n, d = x.shape

    def row(width):
        return pl.BlockSpec((tm, width), lambda i: (i, 0))

    weights = (w["w_branch_a"], w["w_branch_b"], w["w_out"], w["ln_g"], w["ln_b"], w["ln0_g"], w["ln0_b"])
    est = 2 * tm * (4 * d * 4 + 2 * A_WIDTH * 2) + 2 * (2 * A_WIDTH * d + d * d) + 6 * tm * d * 4
    return pl.pallas_call(
        functools.partial(_out_kernel, pre_ln=pre_ln, n_split=max(1, tm // LANES)),
        grid=(n // tm,),
        in_specs=[row(d), row(A_WIDTH), row(B_WIDTH), row(d), row(d)] + [_const_spec(x.shape) for x in weights],
        out_specs=row(d),
        out_shape=jax.ShapeDtypeStruct((n, d), F32),
        compiler_params=_params(("parallel",), est + (8 << 20)),
        name="out_projection",
    )(x, a, b, ga, gb, *weights)


def _layer_weights(l, w_in, conv_w, a_log, dt_bias, gdn_norm_w, w_branch_a, w_branch_b, w_out, ln_g, ln_b,
                   ln0_g, ln0_b):
    pad_h = (0, LANES - B_HEADS)
    return {
        "layer": l, "w_in": w_in,
        "conv_w": conv_w[l].astype(F32),
        "a_row": jnp.pad(-jnp.exp(a_log[l].astype(F32)), pad_h).reshape(1, LANES),
        "dt_row": jnp.pad(dt_bias[l].astype(F32), pad_h).reshape(1, LANES),
        "normw_row": jnp.tile(gdn_norm_w[l].astype(F32), 2).reshape(1, LANES),
        "w_branch_a": w_branch_a[l].astype(BF16), "w_branch_b": w_branch_b[l].astype(BF16),
        "w_out": w_out[l].astype(BF16),
        "ln_g": ln_g[l].astype(F32).reshape(1, D_MODEL), "ln_b": ln_b[l].astype(F32).reshape(1, D_MODEL),
        "ln0_g": ln0_g.astype(F32).reshape(1, D_MODEL), "ln0_b": ln0_b.astype(F32).reshape(1, D_MODEL),
    }


def _prompt_layer(h, lw, rel_bias, *, tm, tb, pre_ln):
    bsz, t, d = h.shape
    keep = min(ATT_WINDOW, t)
    q, k, v, za, xb, zb, ab, ga, gb, k_tail, v_tail, x_tail = _in_projection(
        h, lw, tm=tm, tail=keep, fuse_conv=True, pre_ln=pre_ln)
    oa = _attention_prompt(q, k, v, za, rel_bias)
    conv_zero = jnp.zeros((bsz, CONV_W - 1, B_QKV), F32)
    s_zero = jnp.zeros((bsz, B_HEADS, B_HEAD_DIM, B_HEAD_DIM), F32)
    ob, s_new = _gated_delta(xb, ab, zb, conv_zero, s_zero, lw, tb=tb, do_conv=False, pipelined=True)
    y = _out_projection(h.reshape(bsz * t, d), oa.reshape(bsz * t, A_WIDTH), ob.reshape(bsz * t, B_WIDTH),
                        ga.reshape(bsz * t, d), gb.reshape(bsz * t, d), lw, tm, pre_ln)
    new_k = k_tail.reshape(bsz, keep, A_HEADS, A_HEAD_DIM)
    new_v = v_tail.reshape(bsz, keep, A_HEADS, A_HEAD_DIM)
    return y.reshape(bsz, t, d), (new_k, new_v, x_tail[:, CONV_HIST - (CONV_W - 1):], s_new)


def _sample_layer(h, k_cache, v_cache, conv_prev, s_prev, lw, rel_bias, *, tm, pre_ln):
    bsz, t, d = h.shape
    n = bsz * t
    q, _, _, za, xb, zb, ab, ga, gb, k_new, v_new, _ = _in_projection(
        h.reshape(1, n, d), lw, tm=tm, tail=n, fuse_conv=False, pre_ln=pre_ln)

    def per_seq(x):
        return x.reshape(bsz, t, x.shape[-1])

    oa = _attention_sample(per_seq(q), k_cache, v_cache, lw["layer"],
                           per_seq(k_new), per_seq(v_new), per_seq(za), rel_bias)
    xb_s = per_seq(xb)
    xp_tail = jnp.concatenate([conv_prev.astype(F32), xb_s], axis=1)[:, -(CONV_W - 1):]
    ob, s_new = _gated_delta(xb_s, per_seq(ab), per_seq(zb), conv_prev, s_prev, lw, tb=GDN_CHUNK, do_conv=True, pipelined=False)
    y = _out_projection(h.reshape(n, d), oa.reshape(n, A_WIDTH), ob.reshape(n, B_WIDTH),
                        ga.reshape(n, d), gb.reshape(n, d), lw, tm, pre_ln)
    new_k = per_seq(k_new).reshape(bsz, t, A_HEADS, A_HEAD_DIM)
    new_v = per_seq(v_new).reshape(bsz, t, A_HEADS, A_HEAD_DIM)
    return y.reshape(bsz, t, d), (new_k, new_v, xp_tail, s_new)


def kernel(x_prompt, x_sample, cache_attn_k, cache_attn_v, state_conv, state_gdn, ln0_g, ln0_b, w_in, rel_bias,
           conv_w, a_log, dt_bias, gdn_norm_w, w_branch_a, w_branch_b, w_out, ln_g, ln_b):
    bp, tp, d = x_prompt.shape
    bs, ts, _ = x_sample.shape
    hp, hs = x_prompt, x_sample
    w_in_bf16 = w_in.astype(BF16)
    n_past = cache_attn_k.shape[2]
    k_cache = cache_attn_k.reshape(DEPTH, bs, n_past, A_WIDTH).astype(BF16)
    v_cache = cache_attn_v.reshape(DEPTH, bs, n_past, A_WIDTH).astype(BF16)
    outs_p, outs_s = [], []
    for l in range(DEPTH):
        lw = _layer_weights(l, w_in_bf16, conv_w, a_log, dt_bias, gdn_norm_w, w_branch_a, w_branch_b, w_out, ln_g, ln_b,
                            ln0_g, ln0_b)
        hp, st_p = _prompt_layer(hp, lw, rel_bias[l].astype(F32), tm=PROJ_ROWS_PROMPT, tb=GDN_ROWS, pre_ln=(l == 0))
        hs, st_s = _sample_layer(hs, k_cache, v_cache, state_conv[l], state_gdn[l], lw,
                                 rel_bias[l].astype(F32), tm=PROJ_ROWS_SAMPLE, pre_ln=(l == 0))
        outs_p.append(st_p)
        outs_s.append(st_s)

    def stacked(outs, i):
        return jnp.stack([o[i] for o in outs])

    return (hp, hs,
            stacked(outs_p, 0), stacked(outs_p, 1), stacked(outs_p, 2), stacked(outs_p, 3),
            stacked(outs_s, 0), stacked(outs_s, 1), stacked(outs_s, 2), stacked(outs_s, 3))
```

```python
import functools
import math

import numpy as np
import jax
import jax.numpy as jnp
from jax import lax
from jax.experimental import pallas as pl
from jax.experimental.pallas import tpu as pltpu

D_MODEL = 1024
DEPTH = 2
PAST_LEN = 4096
CHUNK = 64
BAND_CHUNKS = 8
ATT_WINDOW = BAND_CHUNKS * CHUNK
A_HEADS = 8
A_HEAD_DIM = 64
A_WIDTH = A_HEADS * A_HEAD_DIM
REL_MAX = 128
B_HEADS = 8
B_HEAD_DIM = 64
B_WIDTH = B_HEADS * B_HEAD_DIM
B_QKV = 3 * B_WIDTH
CONV_W = 4
GDN_CHUNK = 64
ALPHA = (2 * DEPTH) ** 0.25
LN_EPS = 1e-5
NORM_EPS = 1e-6

LANES = 128
SUBLANES = 8
HEAD_PAIRS = A_HEADS // 2
CONV_HIST = SUBLANES
NEG = -1e30
LOG2E = math.log2(math.e)
Q_SCALE = (A_HEAD_DIM ** -0.5) * LOG2E
V7X_VMEM_LIMIT = 56 * 1024 * 1024

BF16 = jnp.bfloat16
F32 = jnp.float32


def _mm(a, b):
    return jnp.dot(a.astype(BF16), b.astype(BF16), preferred_element_type=F32)


def _mm_nt(a, b):
    return lax.dot_general(a.astype(BF16), b.astype(BF16), (((1,), (1,)), ((), ())),
                           preferred_element_type=F32)


def _mm_tn(a, b):
    return lax.dot_general(a.astype(BF16), b.astype(BF16), (((0,), (0,)), ((), ())),
                           preferred_element_type=F32)


def _split3(x):
    x1 = x.astype(BF16)
    r1 = x - x1.astype(F32)
    x2 = r1.astype(BF16)
    x3 = (r1 - x2.astype(F32)).astype(BF16)
    return x1, x2, x3


def _mm_exact01(w01, x):
    w = w01.astype(BF16)
    return sum(jnp.dot(w, p, preferred_element_type=F32) for p in _split3(x))


def _mm_exact01_right(x, w01, pieces=3):
    w = w01.astype(BF16)
    return sum(jnp.dot(p, w, preferred_element_type=F32) for p in _split3(x)[:pieces])


def _sigmoid(x):
    return 0.5 + 0.5 * jnp.tanh(0.5 * x)


def _silu(x):
    h = 0.5 * x
    return h + h * jnp.tanh(h)


def _first_head_lanes():
    return lax.broadcasted_iota(jnp.int32, (1, LANES), 1) < B_HEAD_DIM


def _params(sem, est_bytes):
    limit = int(min(V7X_VMEM_LIMIT, max(32 * 1024 * 1024, est_bytes)))
    return pltpu.CompilerParams(dimension_semantics=sem, vmem_limit_bytes=limit)


def _const_spec(shape):
    nd = len(shape)
    return pl.BlockSpec(shape, lambda *_: (0,) * nd, pipeline_mode=pl.Buffered(1))


def _ln_rows(x, g, b):
    mu = jnp.mean(x, axis=-1, keepdims=True)
    xc = x - mu
    var = jnp.mean(xc * xc, axis=-1, keepdims=True)
    return xc * lax.rsqrt(var + LN_EPS) * g + b


CONV_COLS = 2 * LANES


def _l2norm_pair(xs, scale):
    first = _first_head_lanes()
    x2 = xs * xs
    zero = jnp.zeros_like(x2)
    s0 = jnp.sum(jnp.where(first, x2, zero), axis=-1, keepdims=True)
    s1 = jnp.sum(jnp.where(first, zero, x2), axis=-1, keepdims=True)
    inv = jnp.where(first, lax.rsqrt(s0 + NORM_EPS), lax.rsqrt(s1 + NORM_EPS))
    return xs * (inv * scale)


def _conv_qkv_cols(xbuf, convw_ref, rows, c0, width=CONV_COLS):
    cols = slice(c0, c0 + width)
    first_tap = CONV_HIST - (CONV_W - 1)
    acc = xbuf[first_tap:first_tap + rows, cols] * convw_ref[0:1, cols]
    for i in range(1, CONV_W):
        acc = acc + xbuf[first_tap + i:first_tap + i + rows, cols] * convw_ref[i:i + 1, cols]
    tail = xbuf[rows:rows + CONV_HIST, cols]
    xbuf[0:CONV_HIST, cols] = tail
    c = _silu(acc)
    if c0 >= 2 * B_WIDTH:
        return c
    scale = B_HEAD_DIM ** -0.5 if c0 < B_WIDTH else 1.0
    return jnp.concatenate([_l2norm_pair(c[:, j * LANES:(j + 1) * LANES], scale)
                            for j in range(width // LANES)], axis=1)


_COL_B = 4 * A_WIDTH
_COL_ZB = _COL_B + B_QKV
_COL_AB = _COL_ZB + B_WIDTH
_COL_G = _COL_AB + 2 * B_HEADS
IN_DIM = _COL_G + 2 * D_MODEL


def _inproj_kernel(x_ref, w_ref, convw_ref, lng_ref, lnb_ref,
                   q_ref, k_ref, v_ref, za_ref, xb_ref, zb_ref, ab_ref, ga_ref, gb_ref,
                   kt_ref, vt_ref, xtail_ref, xbuf, wg_scr, *, fuse_conv, pre_ln):
    tm = x_ref.shape[1]
    dot = functools.partial(jnp.dot, preferred_element_type=F32)

    @pl.when(pl.program_id(1) == 0)
    def _():
        wg_scr[...] = w_ref[:, _COL_G:IN_DIM]
        if fuse_conv:
            xbuf[0:CONV_HIST, :] = jnp.zeros((CONV_HIST, B_QKV), F32)

    x = x_ref[0]
    if pre_ln:
        x = _ln_rows(x, lng_ref[...], lnb_ref[...])
    x = x.astype(BF16)

    def attn_cols(j, half):
        cols = slice(half * CONV_COLS, (half + 1) * CONV_COLS)
        c0 = j * A_WIDTH + half * CONV_COLS
        r = dot(x, w_ref[:, c0:c0 + CONV_COLS])
        if j == 0:
            q_ref[0, :, cols] = (r * Q_SCALE).astype(BF16)
        elif j == 3:
            za_ref[0, :, cols] = r.astype(BF16)
        else:
            (k_ref, v_ref)[j - 1][0, :, cols] = r.astype(BF16)
            (kt_ref, vt_ref)[j - 1][0, :, cols] = r

    def gate_cols(j, part):
        cols = slice(part * CONV_COLS, (part + 1) * CONV_COLS)
        c0 = j * D_MODEL + part * CONV_COLS
        (ga_ref, gb_ref)[j][0, :, cols] = dot(x, wg_scr[:, c0:c0 + CONV_COLS]).astype(BF16)

    def zb_cols(half):
        cols = slice(half * CONV_COLS, (half + 1) * CONV_COLS)
        zb_ref[0, :, cols] = dot(x, w_ref[:, _COL_ZB + half * CONV_COLS:_COL_ZB + (half + 1) * CONV_COLS]).astype(BF16)

    others = ([functools.partial(attn_cols, j, half) for j in range(4) for half in range(2)]
              + [functools.partial(zb_cols, half) for half in range(2)]
              + [functools.partial(gate_cols, j, part) for j in range(2) for part in range(D_MODEL // CONV_COLS)])
    for step in range(B_QKV // CONV_COLS):
        c0 = step * CONV_COLS
        xb = dot(x, w_ref[:, _COL_B + c0:_COL_B + c0 + CONV_COLS])
        xtail_ref[0, :, c0:c0 + CONV_COLS] = xb[tm - CONV_HIST:tm, :]
        if fuse_conv:
            xbuf[CONV_HIST:CONV_HIST + tm, c0:c0 + CONV_COLS] = xb
        else:
            xb_ref[0, :, c0:c0 + CONV_COLS] = xb
        for piece in range(CONV_COLS // LANES):
            if others:
                others.pop(0)()
            if fuse_conv:
                cp = c0 + piece * LANES
                xb_ref[0, :, cp:cp + LANES] = _conv_qkv_cols(xbuf, convw_ref, tm, cp, LANES)
    for rest in others:
        rest()
    ab_ref[0] = dot(x, w_ref[:, _COL_AB:_COL_AB + LANES])


def _in_projection(h, w, *, tm, tail, fuse_conv, pre_ln):
    bsz, t, d = h.shape
    nt = t // tm
    tail_blocks = tail // tm

    def row(width):
        return pl.BlockSpec((1, tm, width), lambda b, i: (b, i, 0))

    tail_spec = pl.BlockSpec((1, tm, A_WIDTH),
                             lambda b, i: (b, jnp.maximum(i - (nt - tail_blocks), 0), 0))
    xtail_spec = pl.BlockSpec((1, CONV_HIST, B_QKV), lambda b, i: (b, 0, 0))
    out_shapes = [
        jax.ShapeDtypeStruct((bsz, t, A_WIDTH), BF16),
        jax.ShapeDtypeStruct((bsz, t, A_WIDTH), BF16),
        jax.ShapeDtypeStruct((bsz, t, A_WIDTH), BF16),
        jax.ShapeDtypeStruct((bsz, t, A_WIDTH), BF16),
        jax.ShapeDtypeStruct((bsz, t, B_QKV), F32),
        jax.ShapeDtypeStruct((bsz, t, B_WIDTH), BF16),
        jax.ShapeDtypeStruct((bsz, t, LANES), F32),
        jax.ShapeDtypeStruct((bsz, t, D_MODEL), BF16),
        jax.ShapeDtypeStruct((bsz, t, D_MODEL), BF16),
        jax.ShapeDtypeStruct((bsz, tail, A_WIDTH), F32),
        jax.ShapeDtypeStruct((bsz, tail, A_WIDTH), F32),
        jax.ShapeDtypeStruct((bsz, CONV_HIST, B_QKV), F32),
    ]
    out_specs = [row(A_WIDTH), row(A_WIDTH), row(A_WIDTH), row(A_WIDTH), row(B_QKV), row(B_WIDTH),
                 row(LANES), row(D_MODEL), row(D_MODEL), tail_spec, tail_spec, xtail_spec]
    layer = w["layer"]
    weights = (w["w_in"], w["conv_w"], w["ln0_g"], w["ln0_b"])
    w_spec = pl.BlockSpec((None,) + w["w_in"].shape[1:], lambda *_: (layer, 0, 0), pipeline_mode=pl.Buffered(1))
    in_specs = [row(d), w_spec] + [_const_spec(x.shape) for x in weights[1:]]
    n_w = sum(int(np.prod(x.shape[-2:])) for x in weights)
    est = 2 * n_w + 2 * tm * (d * 4 + 3 * A_WIDTH * 2 + (2 * A_WIDTH + B_QKV + B_WIDTH + LANES
                                                        + 2 * D_MODEL + 2 * A_WIDTH) * 4)
    est += tm * (4 * A_WIDTH + 2 * D_MODEL + 4 * B_QKV) * 4 * 2
    return pl.pallas_call(
        functools.partial(_inproj_kernel, fuse_conv=fuse_conv, pre_ln=pre_ln),
        grid=(bsz, nt),
        in_specs=in_specs,
        out_specs=out_specs,
        out_shape=out_shapes,
        scratch_shapes=[pltpu.VMEM((CONV_HIST + tm, B_QKV), F32), pltpu.VMEM((d, 2 * D_MODEL), BF16)],
        compiler_params=_params(("parallel", "arbitrary"), est + (8 << 20)),
        name="in_projection",
    )(h, *weights)


def _attention_heads(jobs, bias_ref):
    first = _first_head_lanes()

    def scores(job, p):
        q, k = job[0], job[1]
        qp, kp = q[:, p * LANES:(p + 1) * LANES], k[:, p * LANES:(p + 1) * LANES]
        zero = jnp.zeros_like(qp)
        q2 = jnp.concatenate([jnp.where(first, qp, zero), jnp.where(first, zero, qp)], axis=0)
        return _mm_nt(q2, kp)

    def softmax(job, p, s):
        s = s + bias_ref[p]
        if job[4] is not None:
            s = jnp.where(job[4], s, NEG)
        e = jnp.exp2(s - jnp.max(s, axis=-1, keepdims=True))
        return e.astype(BF16), jnp.sum(e, axis=-1, keepdims=True)

    def output(job, p, e_den):
        _, _, v, za, _, o_ref = job
        nq = za.shape[0]
        sl = slice(p * LANES, (p + 1) * LANES)
        pv = _mm(e_den[0], v[:, sl]) / e_den[1]
        o = jnp.where(first, pv[0:nq], pv[nq:2 * nq])
        o_ref[:, sl] = (o * _silu(za[:, sl].astype(F32))).astype(o_ref.dtype)

    chains = [(job, p) for job in jobs for p in range(HEAD_PAIRS)]
    s_val, e_val = {}, {}
    for step in range(len(chains) + 2):
        if step < len(chains):
            s_val[step] = scores(*chains[step])
        if 0 <= step - 1 < len(chains):
            e_val[step - 1] = softmax(*chains[step - 1], s_val.pop(step - 1))
        if 0 <= step - 2 < len(chains):
            output(*chains[step - 2], e_val.pop(step - 2))


def _attn_prompt_kernel(q_ref, *refs, qblk, n_sub, n_kblk):
    k_refs, v_refs = refs[0:n_kblk], refs[n_kblk:2 * n_kblk]
    za_ref, bias_ref, o_ref = refs[2 * n_kblk:]
    m = pl.program_id(1)
    n_lead = n_kblk - n_sub
    band = (n_lead + 1) * qblk

    def jobs(masked):
        out = []
        for u in range(n_sub):
            k = jnp.concatenate([r[0] for r in k_refs[u:u + n_lead + 1]], axis=0)
            v = jnp.concatenate([r[0] for r in v_refs[u:u + n_lead + 1]], axis=0)
            valid = None
            if masked:
                col = lax.broadcasted_iota(jnp.int32, (1, band), 1) + (m * n_sub + u - n_lead) * qblk
                valid = col >= 0
            rows = slice(u * qblk, (u + 1) * qblk)
            out.append((q_ref[0, rows, :], k, v, za_ref[0, rows, :], valid, o_ref.at[0, rows, :]))
        return out

    @pl.when(m * n_sub >= n_lead)
    def _():
        _attention_heads(jobs(False), bias_ref)

    @pl.when(m * n_sub < n_lead)
    def _():
        _attention_heads(jobs(True), bias_ref)


def _rel_bias_table(rel_bias, nq, nk, n_before):
    h = rel_bias.shape[0]
    n_var = nq + REL_MAX
    assert nk - n_var >= 0 and n_before + nq == nk
    period = n_var + nq
    far = rel_bias[:, REL_MAX + CHUNK - 1:]
    n_low = n_var - (REL_MAX + CHUNK)
    g = jnp.concatenate([rel_bias[:, ::-1], jnp.repeat(rel_bias[:, :1], n_low, axis=1),
                         jnp.repeat(far, period - n_var, axis=1)], axis=1)
    skew = jnp.tile(g, (1, nq))[:, :nq * (period - 1)].reshape(h, nq, period - 1)[:, :, :n_var]
    const = jnp.broadcast_to(far[:, :, None], (h, nq, nk - n_var))
    return jnp.concatenate([const, skew], axis=2)


def _pair_tables(tab):
    h, nq, nk = tab.shape
    return (tab * LOG2E).astype(F32).reshape(h // 2, 2 * nq, nk)


ATT_QBLK = 2 * CHUNK
ATT_BAND = ATT_WINDOW + ATT_QBLK


def _prompt_bias_tables(rel_bias):
    n_layers, h, n_rel = rel_bias.shape
    i = np.arange(ATT_QBLK)[:, None]
    j = np.arange(ATT_BAND)[None, :]
    qc, kc = i // CHUNK, j // CHUNK
    visible = (kc >= qc) & (kc <= qc + BAND_CHUNKS)
    tab = _rel_bias_table(rel_bias.reshape(n_layers * h, n_rel), ATT_QBLK, ATT_BAND, ATT_WINDOW)
    tab = _pair_tables(jnp.where(jnp.asarray(visible)[None], tab, NEG))
    return tab.reshape(n_layers, h // 2, 2 * ATT_QBLK, ATT_BAND)


def _sample_bias_tables(rel_bias, t, n_past):
    n_layers, h, n_rel = rel_bias.shape
    nk = -(-(n_past + t) // LANES) * LANES
    tab = _rel_bias_table(rel_bias.reshape(n_layers * h, n_rel), nk - n_past, nk, n_past)[:, :t]
    tab = _pair_tables(jnp.where(jnp.asarray(np.arange(nk) < n_past + t)[None, None], tab, NEG))
    return tab.reshape(n_layers, h // 2, 2 * t, nk)


def _attention_prompt(q, k, v, za, tab, n_sub=4):
    bsz, t, _ = q.shape
    qblk, band = ATT_QBLK, ATT_BAND
    n_lead = ATT_WINDOW // qblk
    n_kblk = n_lead + n_sub
    row = pl.BlockSpec((1, n_sub * qblk, A_WIDTH), lambda b, m: (b, m, 0))

    def key_block(jb):
        return pl.BlockSpec((1, qblk, A_WIDTH), lambda b, m: (b, jnp.maximum(m * n_sub - n_lead + jb, 0), 0))

    kv_specs = [key_block(jb) for jb in range(n_kblk)]
    est = tab.size * 4 + 2 * (2 * n_kblk + 2 * n_sub) * qblk * A_WIDTH * 2 + n_sub * 32 * qblk * band * 4
    return pl.pallas_call(
        functools.partial(_attn_prompt_kernel, qblk=qblk, n_sub=n_sub, n_kblk=n_kblk),
        grid=(bsz, t // (n_sub * qblk)),
        in_specs=[row] + kv_specs + kv_specs + [row, _const_spec(tab.shape)],
        out_specs=row,
        out_shape=jax.ShapeDtypeStruct((bsz, t, A_WIDTH), BF16),
        compiler_params=_params(("parallel", "arbitrary"), est + (8 << 20)),
        name="attention_prompt",
    )(q, *([k] * n_kblk), *([v] * n_kblk), za, tab)


def _attn_sample_kernel(q_ref, kc_ref, vc_ref, kn_ref, vn_ref, za_ref, bias_ref, o_ref, *, nk):
    n_seq, n_past, t = kc_ref.shape[0], kc_ref.shape[1], kn_ref.shape[1]
    zeros = jnp.zeros((nk - n_past - t, A_WIDTH), BF16)
    jobs = []
    for sq in range(n_seq):
        k = jnp.concatenate([kc_ref[sq].astype(BF16), kn_ref[sq].astype(BF16), zeros], axis=0)
        v = jnp.concatenate([vc_ref[sq].astype(BF16), vn_ref[sq].astype(BF16), zeros], axis=0)
        jobs.append((q_ref[sq], k, v, za_ref[sq], None, o_ref.at[sq]))
    _attention_heads(jobs, bias_ref)


def _attention_sample(q, k_cache, v_cache, layer, k_new, v_new, za, tab):
    bsz, t, _ = q.shape
    n_past = k_cache.shape[2]
    nk = tab.shape[-1]

    n_seq = max(s for s in (1, 2, 4) if bsz % s == 0)

    def blk(rows):
        return pl.BlockSpec((n_seq, rows, A_WIDTH), lambda b: (b, 0, 0))

    cache = pl.BlockSpec((None, n_seq, n_past, A_WIDTH), lambda b: (layer, b, 0, 0))
    return pl.pallas_call(
        functools.partial(_attn_sample_kernel, nk=nk),
        grid=(bsz // n_seq,),
        in_specs=[blk(t), cache, cache, blk(t), blk(t), blk(t), _const_spec(tab.shape)],
        out_specs=blk(t),
        out_shape=jax.ShapeDtypeStruct((bsz, t, A_WIDTH), BF16),
        compiler_params=_params(("parallel",), 32 << 20),
        name="attention_sample",
    )(q, k_cache, v_cache, k_new, v_new, za, tab)


_KB, _QN, _KN, _VB, _KBE, _QE, _KD = range(7)


def _gdn_kernel(xb_ref, ab_ref, zb_ref, conv0_ref, s0_ref, convw_ref, arow_ref, dtrow_ref, normw_ref,
                o_ref, s_out_ref, xbuf, s_scr, nat_scr, gcx_scr, sdec_scr, aq_scr, bo_scr,
                *, tb, n_seq, n_valid, do_conv, n_par, pipelined):
    t_idx = pl.program_id(1)
    blk = GDN_CHUNK
    two = 2 * blk
    seq_chunks = tb // blk
    n_chunks = n_seq * seq_chunks
    rows_all = n_seq * tb

    assert not (pipelined and do_conv)
    rows_out = min(blk, n_valid)

    @pl.when(t_idx == 0)
    def _():
        s_scr[...] = s0_ref[...]
        if do_conv:
            xbuf[:, 0:CONV_HIST, :] = conv0_ref[...]
        if pipelined:
            aq_scr[...] = jnp.zeros(aq_scr.shape, aq_scr.dtype)
            bo_scr[...] = jnp.zeros(bo_scr.shape, bo_scr.dtype)
            sdec_scr[...] = jnp.ones(sdec_scr.shape, sdec_scr.dtype)

    def phase2(cidx):
        seq, r0 = cidx // seq_chunks, (cidx % seq_chunks) * blk
        for p in range(HEAD_PAIRS):
            s_old = s_scr[seq, p]
            tot = bo_scr[cidx, p] + _mm(aq_scr[cidx, p], s_old)
            dec = sdec_scr[cidx, 0:1, p * LANES:(p + 1) * LANES]
            s_scr[seq, p] = s_old * dec + tot[0:two]
            o = tot[two:2 * two]
            ms = jnp.sum(o * o, axis=-1, keepdims=True) * (1.0 / B_HEAD_DIM)
            on = o * lax.rsqrt(ms + NORM_EPS) * normw_ref[...]
            o_pair = (on[0:blk] + on[blk:two])[0:rows_out]
            zb = zb_ref[seq, r0:r0 + rows_out, p * LANES:(p + 1) * LANES].astype(F32)
            o_ref[seq, r0:r0 + rows_out, p * LANES:(p + 1) * LANES] = (o_pair * _silu(zb)).astype(o_ref.dtype)

    pending = list(range(n_chunks)) if pipelined else []

    def padded_seq(ref, sq):
        x = ref[sq]
        if n_valid == tb:
            return x
        return jnp.concatenate([x, jnp.zeros((tb - n_valid, x.shape[1]), x.dtype)], axis=0)

    def padded(ref):
        parts = [padded_seq(ref, sq) for sq in range(n_seq)]
        return parts[0] if n_seq == 1 else jnp.concatenate(parts, axis=0)

    if do_conv:
        for sq in range(n_seq):
            xbuf[sq, CONV_HIST:CONV_HIST + tb, :] = padded_seq(xb_ref, sq)
        c = jnp.concatenate(
            [jnp.concatenate([_conv_qkv_cols(xbuf.at[sq], convw_ref, tb, c0) for c0 in range(0, B_QKV, CONV_COLS)],
                             axis=1) for sq in range(n_seq)], axis=0)
    else:
        c = padded(xb_ref)
    qn, kn, v = c[:, 0:B_WIDTH], c[:, B_WIDTH:2 * B_WIDTH], c[:, 2 * B_WIDTH:]

    ab = padded(ab_ref)
    g = arow_ref[...] * jax.nn.softplus(ab + dtrow_ref[...])
    beta = _sigmoid(ab)
    if n_valid != tb:
        is_token = lax.broadcasted_iota(jnp.int32, (rows_all, 1), 0) % tb < n_valid
        g = jnp.where(is_token, g, 0.0)
        beta = jnp.where(is_token, beta, 0.0)
    rt = lax.broadcasted_iota(jnp.int32, (rows_all, rows_all), 0)
    ct = lax.broadcasted_iota(jnp.int32, (rows_all, rows_all), 1)
    gc = _mm_exact01(((rt // blk) == (ct // blk)) & (rt >= ct), g)
    src = lax.broadcasted_iota(jnp.int32, (LANES, B_WIDTH), 0)
    head = lax.broadcasted_iota(jnp.int32, (LANES, B_WIDTH), 1) // B_HEAD_DIM
    gcx = _mm_exact01_right(gc, src == head)
    betax = _mm_exact01_right(beta, src == head + B_HEADS, pieces=2)
    glx = jnp.concatenate([jnp.broadcast_to(gcx[(i + 1) * blk - 1:(i + 1) * blk, :], (blk, B_WIDTH))
                           for i in range(n_chunks)], axis=0)
    egc = jnp.exp(gcx)
    kbeta = kn * betax
    nat_scr[_KB] = kbeta.astype(BF16)
    nat_scr[_QN] = qn.astype(BF16)
    nat_scr[_KN] = kn.astype(BF16)
    nat_scr[_VB] = (v * betax).astype(BF16)
    nat_scr[_KBE] = (kbeta * egc).astype(BF16)
    nat_scr[_QE] = (qn * egc).astype(BF16)
    nat_scr[_KD] = (kn * jnp.exp(glx - gcx)).astype(BF16)
    gcx_scr[...] = gcx
    sdec_rows = [jnp.exp(gcx[(i + 1) * blk - 1:(i + 1) * blk, :]) for i in range(n_chunks)]

    first = _first_head_lanes()
    r = lax.broadcasted_iota(jnp.int32, (two, two), 0)
    cc = lax.broadcasted_iota(jnp.int32, (two, two), 1)
    causal = ((r // blk) == (cc // blk)) & (r >= cc)
    strict = causal & (r > cc)
    eye = (r == cc).astype(F32)
    diag = (lax.broadcasted_iota(jnp.int32, (blk, LANES), 0)
            == lax.broadcasted_iota(jnp.int32, (blk, LANES), 1) % B_HEAD_DIM)

    def level_mask(bs):
        return ((r // bs) == (cc // bs)) & ((r % bs) >= bs // 2) & ((cc % bs) < bs // 2)

    def stacked(which, r0, p):
        x = nat_scr[which, pl.ds(r0, blk), p * LANES:(p + 1) * LANES]
        zero = jnp.zeros_like(x)
        return jnp.concatenate([jnp.where(first, x, zero), jnp.where(first, zero, x)], axis=0)

    def phase1(step, carry):
        chains = []
        for dc in range(n_par):
            cidx = step * n_par + dc
            for p in range(HEAD_PAIRS):
                chains.append({"cidx": cidx, "r0": cidx * blk, "p": p})
        for ch in chains:
            r0, p = ch["r0"], ch["p"]
            gt = gcx_scr[pl.ds(r0, blk), p * LANES:(p + 1) * LANES]
            gc_row = jnp.sum(jnp.where(diag, gt, 0.0), axis=0, keepdims=True)
            ch["decay"] = jnp.exp(jnp.where(causal, jnp.concatenate([gt, gt], axis=0) - gc_row, NEG))
            lhs = jnp.concatenate([stacked(_KB, r0, p), stacked(_QN, r0, p)], axis=0)
            gram = _mm_nt(lhs, stacked(_KN, r0, p))
            ch["m"] = jnp.where(strict, gram[0:two] * ch["decay"], 0.0)
            ch["qk"] = (gram[two:2 * two] * ch["decay"]).astype(BF16)
            ch["x"] = eye - jnp.where(level_mask(2), ch["m"], 0.0)
        bs = 4
        while bs <= blk:
            lm = level_mask(bs)
            for ch in chains:
                ch["t"] = _mm(jnp.where(lm, ch["m"], 0.0), ch["x"])
            if pending:
                phase2(pending.pop(0))
            for ch in chains:
                ch["x"] = ch["x"] - _mm(ch["x"], ch["t"])
            bs *= 2
        while pending:
            phase2(pending.pop(0))
        for ch in chains:
            r0, p = ch["r0"], ch["p"]
            rhs = jnp.concatenate([stacked(_VB, r0, p), stacked(_KBE, r0, p)], axis=1)
            ch["uw"] = _mm(ch["x"], rhs).astype(BF16)
        for ch in chains:
            r0, p = ch["r0"], ch["p"]
            kd_uw = _mm_tn(stacked(_KD, r0, p), ch["uw"])
            qk_uw = _mm(ch["qk"], ch["uw"])
            qeff = stacked(_QE, r0, p).astype(F32) - qk_uw[:, LANES:2 * LANES]
            aq_scr[ch["cidx"], p] = jnp.concatenate([-kd_uw[:, LANES:2 * LANES], qeff], axis=0).astype(BF16)
            bo_scr[ch["cidx"], p] = jnp.concatenate([kd_uw[:, 0:LANES], qk_uw[:, 0:LANES]], axis=0)
        return carry

    for step in range(n_chunks // n_par):
        phase1(step, 0)
    for i in range(n_chunks):
        sdec_scr[i] = jnp.broadcast_to(sdec_rows[i], (SUBLANES, B_WIDTH))
    if not pipelined:
        for i in range(n_chunks):
            phase2(i)

    @pl.when(t_idx == pl.num_programs(1) - 1)
    def _():
        s_out_ref[...] = s_scr[...]


def _pair_state(s):
    bsz = s.shape[0]
    s = s.reshape(bsz, HEAD_PAIRS, 2, B_HEAD_DIM, B_HEAD_DIM)
    z = jnp.zeros_like(s[:, :, 0])
    top = jnp.concatenate([s[:, :, 0], z], axis=-1)
    bot = jnp.concatenate([z, s[:, :, 1]], axis=-1)
    return jnp.concatenate([top, bot], axis=-2)


def _unpair_state(sp):
    d = B_HEAD_DIM
    s = jnp.stack([sp[:, :, 0:d, 0:d], sp[:, :, d:2 * d, d:2 * d]], axis=2)
    return s.reshape(sp.shape[0], B_HEADS, d, d)


def _gated_delta(xb, ab, zb, conv_prev, s_prev, lw, *, tb, do_conv, pipelined):
    bsz, t, _ = xb.shape
    n_valid = min(t, tb)
    assert t % n_valid == 0 and tb % GDN_CHUNK == 0 and (n_valid == tb or t == n_valid)
    conv0 = jnp.concatenate([jnp.zeros((bsz, CONV_HIST - (CONV_W - 1), B_QKV), F32), conv_prev.astype(F32)], axis=1)
    s0 = _pair_state(s_prev.astype(F32))

    nt = t // n_valid
    n_seq = max(s for s in (1, 2, 4) if bsz % s == 0) if (nt == 1 and tb == GDN_CHUNK) else 1

    def row(width, lag):
        if not pipelined:
            return pl.BlockSpec((n_seq, n_valid, width), lambda b, i: (b, i, 0))
        if lag:
            return pl.BlockSpec((n_seq, n_valid, width), lambda b, i: (b, jnp.maximum(i - 1, 0), 0))
        return pl.BlockSpec((n_seq, n_valid, width), lambda b, i: (b, jnp.minimum(i, nt - 1), 0))

    def per_batch(shape):
        nd = len(shape)
        return pl.BlockSpec((n_seq,) + shape, lambda b, i: (b,) + (0,) * nd)

    n_chunks = n_seq * tb // GDN_CHUNK
    n_par = max(p for p in (1, 2, 4) if n_chunks % p == 0)
    per_chain = (n_chunks, HEAD_PAIRS, 2 * LANES, LANES)
    est = 2 * n_valid * (B_QKV + LANES + B_WIDTH) * 4 + (tb + CONV_HIST) * B_QKV * 4 + 6 * HEAD_PAIRS * LANES * LANES * 4
    rows_all = n_seq * tb
    est += rows_all * B_WIDTH * (7 * 2 + 4 + 16 * 4) + n_chunks * HEAD_PAIRS * LANES * LANES * 8 + rows_all * rows_all * 8
    o, s_out = pl.pallas_call(
        functools.partial(_gdn_kernel, tb=tb, n_seq=n_seq, n_valid=n_valid, do_conv=do_conv, n_par=n_par,
                          pipelined=pipelined),
        grid=(bsz // n_seq, nt + 1 if pipelined else nt),
        in_specs=[row(B_QKV, False), row(LANES, False), row(B_WIDTH, True), per_batch((CONV_HIST, B_QKV)),
                  per_batch((HEAD_PAIRS, LANES, LANES)),
                  _const_spec((CONV_W, B_QKV)), _const_spec((1, LANES)), _const_spec((1, LANES)),
                  _const_spec((1, LANES))],
        out_specs=[row(B_WIDTH, True), per_batch((HEAD_PAIRS, LANES, LANES))],
        out_shape=[jax.ShapeDtypeStruct((bsz, t, B_WIDTH), BF16),
                   jax.ShapeDtypeStruct((bsz, HEAD_PAIRS, LANES, LANES), F32)],
        scratch_shapes=[pltpu.VMEM((n_seq, CONV_HIST + tb, B_QKV), F32),
                        pltpu.VMEM((n_seq, HEAD_PAIRS, LANES, LANES), F32),
                        pltpu.VMEM((7, n_seq * tb, B_WIDTH), BF16),
                        pltpu.VMEM((n_seq * tb, B_WIDTH), F32),
                        pltpu.VMEM((n_chunks, SUBLANES, B_WIDTH), F32),
                        pltpu.VMEM(per_chain, BF16), pltpu.VMEM(per_chain, F32)],
        compiler_params=_params(("parallel", "arbitrary"), est + (16 << 20)),
        name="gated_delta",
    )(xb, ab, zb, conv0, s0, lw["conv_w"], lw["a_row"], lw["dt_row"], lw["normw_row"])
    return o, _unpair_state(s_out)


def _out_kernel(x_ref, a_ref, b_ref, ga_ref, gb_ref, wa_ref, wb_ref, wo_ref, g_ref, beta_ref, lng_ref, lnb_ref,
                y_ref, *, pre_ln, n_split):
    dot = functools.partial(jnp.dot, preferred_element_type=F32)
    rows = x_ref.shape[0] // n_split

    def project(u):
        sl = slice(u * rows, (u + 1) * rows)
        ya = dot(a_ref[sl, :], wa_ref[...])
        yb = dot(b_ref[sl, :], wb_ref[...])
        mix = _sigmoid(ga_ref[sl, :].astype(F32)) * ya + _sigmoid(gb_ref[sl, :].astype(F32)) * yb
        return dot(mix.astype(BF16), wo_ref[...])

    def finish(u, out):
        sl = slice(u * rows, (u + 1) * rows)
        x = x_ref[sl, :]
        if pre_ln:
            x = _ln_rows(x, lng_ref[...], lnb_ref[...])
        y_ref[sl, :] = _ln_rows(ALPHA * x + out, g_ref[...], beta_ref[...])

    outs = {}
    for step in range(n_split + 1):
        if step < n_split:
            outs[step] = project(step)
        if step >= 1:
            finish(step - 1, outs.pop(step - 1))


def _out_projection(x, a, b, ga, gb, w, tm, pre_ln):
    n, d = x.shape

    def row(width):
        return pl.BlockSpec((tm, width), lambda i: (i, 0))

    weights = (w["w_branch_a"], w["w_branch_b"], w["w_out"], w["ln_g"], w["ln_b"], w["ln0_g"], w["ln0_b"])
    est = 2 * tm * (4 * d * 4 + 2 * A_WIDTH * 2) + 2 * (2 * A_WIDTH * d + d * d) + 6 * tm * d * 4
    return pl.pallas_call(
        functools.partial(_out_kernel, pre_ln=pre_ln, n_split=max(1, tm // LANES)),
        grid=(n // tm,),
        in_specs=[row(d), row(A_WIDTH), row(B_WIDTH), row(d), row(d)] + [_const_spec(x.shape) for x in weights],
        out_specs=row(d),
        out_shape=jax.ShapeDtypeStruct((n, d), F32),
        compiler_params=_params(("parallel",), est + (8 << 20)),
        name="out_projection",
    )(x, a, b, ga, gb, *weights)


def _layer_weights(l, w_in, conv_w, a_log, dt_bias, gdn_norm_w, w_branch_a, w_branch_b, w_out, ln_g, ln_b,
                   ln0_g, ln0_b):
    pad_h = (0, LANES - B_HEADS)
    return {
        "layer": l, "w_in": w_in,
        "conv_w": conv_w[l].astype(F32),
        "a_row": jnp.pad(-jnp.exp(a_log[l].astype(F32)), pad_h).reshape(1, LANES),
        "dt_row": jnp.pad(dt_bias[l].astype(F32), pad_h).reshape(1, LANES),
        "normw_row": jnp.tile(gdn_norm_w[l].astype(F32), 2).reshape(1, LANES),
        "w_branch_a": w_branch_a[l].astype(BF16), "w_branch_b": w_branch_b[l].astype(BF16),
        "w_out": w_out[l].astype(BF16),
        "ln_g": ln_g[l].astype(F32).reshape(1, D_MODEL), "ln_b": ln_b[l].astype(F32).reshape(1, D_MODEL),
        "ln0_g": ln0_g.astype(F32).reshape(1, D_MODEL), "ln0_b": ln0_b.astype(F32).reshape(1, D_MODEL),
    }


def _prompt_layer(h, lw, tab, *, tm, tb, pre_ln):
    bsz, t, d = h.shape
    keep = min(ATT_WINDOW, t)
    q, k, v, za, xb, zb, ab, ga, gb, k_tail, v_tail, x_tail = _in_projection(
        h, lw, tm=tm, tail=keep, fuse_conv=True, pre_ln=pre_ln)
    oa = _attention_prompt(q, k, v, za, tab)
    conv_zero = jnp.zeros((bsz, CONV_W - 1, B_QKV), F32)
    s_zero = jnp.zeros((bsz, B_HEADS, B_HEAD_DIM, B_HEAD_DIM), F32)
    ob, s_new = _gated_delta(xb, ab, zb, conv_zero, s_zero, lw, tb=tb, do_conv=False, pipelined=True)
    y = _out_projection(h.reshape(bsz * t, d), oa.reshape(bsz * t, A_WIDTH), ob.reshape(bsz * t, B_WIDTH),
                        ga.reshape(bsz * t, d), gb.reshape(bsz * t, d), lw, tm, pre_ln)
    new_k = k_tail.reshape(bsz, keep, A_HEADS, A_HEAD_DIM)
    new_v = v_tail.reshape(bsz, keep, A_HEADS, A_HEAD_DIM)
    return y.reshape(bsz, t, d), (new_k, new_v, x_tail[:, CONV_HIST - (CONV_W - 1):], s_new)


def _sample_layer(h, k_cache, v_cache, conv_prev, s_prev, lw, tab, *, tm, pre_ln):
    bsz, t, d = h.shape
    n = bsz * t
    q, _, _, za, xb, zb, ab, ga, gb, k_new, v_new, _ = _in_projection(
        h.reshape(1, n, d), lw, tm=tm, tail=n, fuse_conv=False, pre_ln=pre_ln)

    def per_seq(x):
        return x.reshape(bsz, t, x.shape[-1])

    oa = _attention_sample(per_seq(q), k_cache, v_cache, lw["layer"],
                           per_seq(k_new), per_seq(v_new), per_seq(za), tab)
    xb_s = per_seq(xb)
    xp_tail = jnp.concatenate([conv_prev.astype(F32), xb_s], axis=1)[:, -(CONV_W - 1):]
    ob, s_new = _gated_delta(xb_s, per_seq(ab), per_seq(zb), conv_prev, s_prev, lw, tb=GDN_CHUNK, do_conv=True, pipelined=False)
    y = _out_projection(h.reshape(n, d), oa.reshape(n, A_WIDTH), ob.reshape(n, B_WIDTH),
                        ga.reshape(n, d), gb.reshape(n, d), lw, tm, pre_ln)
    new_k = per_seq(k_new).reshape(bsz, t, A_HEADS, A_HEAD_DIM)
    new_v = per_seq(v_new).reshape(bsz, t, A_HEADS, A_HEAD_DIM)
    return y.reshape(bsz, t, d), (new_k, new_v, xp_tail, s_new)


def kernel(x_prompt, x_sample, cache_attn_k, cache_attn_v, state_conv, state_gdn, ln0_g, ln0_b, w_in, rel_bias,
           conv_w, a_log, dt_bias, gdn_norm_w, w_branch_a, w_branch_b, w_out, ln_g, ln_b):
    bp, tp, d = x_prompt.shape
    bs, ts, _ = x_sample.shape
    hp, hs = x_prompt, x_sample
    tabs_p = _prompt_bias_tables(rel_bias.astype(F32))
    tabs_s = _sample_bias_tables(rel_bias.astype(F32), ts, cache_attn_k.shape[2])
    w_in_bf16 = w_in.astype(BF16)
    n_past = cache_attn_k.shape[2]
    k_cache = cache_attn_k.reshape(DEPTH, bs, n_past, A_WIDTH)
    v_cache = cache_attn_v.reshape(DEPTH, bs, n_past, A_WIDTH)
    outs_p, outs_s = [], []
    for l in range(DEPTH):
        lw = _layer_weights(l, w_in_bf16, conv_w, a_log, dt_bias, gdn_norm_w, w_branch_a, w_branch_b, w_out, ln_g, ln_b,
                            ln0_g, ln0_b)
        hp, st_p = _prompt_layer(hp, lw, tabs_p[l], tm=512, tb=256, pre_ln=(l == 0))
        hs, st_s = _sample_layer(hs, k_cache, v_cache, state_conv[l], state_gdn[l], lw,
                                 tabs_s[l], tm=256, pre_ln=(l == 0))
        outs_p.append(st_p)
        outs_s.append(st_s)

    def stacked(outs, i):
        return jnp.stack([o[i] for o in outs])

    return (hp, hs,
            stacked(outs_p, 0), stacked(outs_p, 1), stacked(outs_p, 2), stacked(outs_p, 3),
            stacked(outs_s, 0), stacked(outs_s, 1), stacked(outs_s, 2), stacked(outs_s, 3))
```
